```python
import jax, jax.numpy as jnp
from jax import lax
import numpy as np

D_MODEL = 1024
BATCH = 8
SEQ = 4096
DEPTH = 4

EPS = 1e-6
BLOCK = 128
HEAD_DIM = 128
N_HEADS_A = 4
N_HEADS_B = 4
N_IDX_HEADS = 8
IDX_DIM = 64
TOPK_MAX = 256
N_HEADS_C = 4
QK_NOPE_DIM = 128
QK_ROPE_DIM = 64
V_DIM_C = 128
Q_LORA = 384
KV_LORA = 256
ROPE_THETA = 10000.0
SSM_D_INNER = 1024
SSM_HEAD_DIM = 64
SSM_HEADS = SSM_D_INNER // SSM_HEAD_DIM
SSM_GROUPS = 4
SSM_STATE = 128
CONV_WIDTH = 4
SSM_CHUNK = 128
CONV_DIM = SSM_D_INNER + 2 * SSM_GROUPS * SSM_STATE
D_FF = 4 * D_MODEL

N_EVEN = (DEPTH + 1) // 2
N_ODD = DEPTH // 2

EVEN_COLS = (N_HEADS_A * HEAD_DIM, HEAD_DIM, HEAD_DIM,
             N_IDX_HEADS * IDX_DIM, IDX_DIM, N_IDX_HEADS,
             N_HEADS_B * HEAD_DIM, N_HEADS_B * HEAD_DIM, N_HEADS_B * HEAD_DIM, N_HEADS_B)
ODD_COLS = (Q_LORA, KV_LORA, QK_ROPE_DIM,
            SSM_D_INNER, CONV_DIM, SSM_HEADS)
W_IN_EVEN = sum(EVEN_COLS)
W_IN_ODD = sum(ODD_COLS)
W_OUT_EVEN = N_HEADS_A * HEAD_DIM + N_HEADS_B * HEAD_DIM
W_OUT_ODD = N_HEADS_C * V_DIM_C + SSM_D_INNER

kernel_name = 'hybrid_dsa_fox_mla_mamba2_trunk'


def split_cols(t, sizes):
    out, off = [], 0
    for n in sizes:
        out.append(t[..., off:off + n])
        off += n
    return out


def rms_norm(x, g):
    x32 = x.astype(jnp.float32)
    y = x32 * lax.rsqrt(jnp.mean(x32 * x32, axis=-1, keepdims=True) + EPS)
    return (y * g.astype(jnp.float32)).astype(x.dtype)


def rope(x, positions):
    half = QK_ROPE_DIM // 2
    inv_freq = ROPE_THETA ** (-jnp.arange(half, dtype=jnp.float32) / half)
    ang = positions.astype(jnp.float32)[..., None] * inv_freq
    cos, sin = jnp.cos(ang)[:, :, None, :], jnp.sin(ang)[:, :, None, :]
    x32 = x.astype(jnp.float32)
    x1, x2 = x32[..., :half], x32[..., half:]
    return jnp.concatenate([x1 * cos - x2 * sin, x2 * cos + x1 * sin], axis=-1).astype(x.dtype)


def to_blocks(t, nb):
    return t.reshape(t.shape[0], nb, BLOCK, *t.shape[2:]).swapaxes(0, 1)


def causal_softmax_attention(q, k, v, log_decay=None):
    B, S, H, Dq = q.shape
    nb = S // BLOCK
    scale = Dq ** -0.5
    kpos = jnp.arange(S)
    ck = None if log_decay is None else log_decay.transpose(0, 2, 1)

    def block(args):
        qi, start = args
        logits = jnp.einsum('bqhd,bkhd->bhqk', qi, k).astype(jnp.float32) * scale
        if ck is not None:
            ci = lax.dynamic_slice_in_dim(ck, start, BLOCK, axis=2)
            logits = logits + ci[..., None] - ck[:, :, None, :]
        qpos = start + jnp.arange(BLOCK)
        mask = kpos[None, :] <= qpos[:, None]
        logits = jnp.where(mask, logits, -jnp.inf)
        p = jax.nn.softmax(logits, axis=-1).astype(v.dtype)
        return jnp.einsum('bhqk,bkhd->bqhd', p, v)

    out = lax.map(block, (to_blocks(q, nb), jnp.arange(nb) * BLOCK))
    return out.swapaxes(0, 1).reshape(B, S, H, v.shape[-1])


def dsa_attention(q, k, v, q_idx, k_idx, w_idx, topk):
    B, S, H, D = q.shape
    nb = S // BLOCK
    scale = D ** -0.5
    kpos = jnp.arange(S)
    gather = jax.vmap(lambda t, i: t[i])

    def block(args):
        qi, qii, wi, start = args
        s_h = jnp.einsum('bqjd,bkd->bjqk', qii, k_idx).astype(jnp.float32)
        score = jnp.einsum('bqj,bjqk->bqk', wi.astype(jnp.float32), jax.nn.relu(s_h))
        qpos = start + jnp.arange(BLOCK)
        admissible = kpos[None, :] <= qpos[:, None]
        score = jnp.where(admissible[None], score, -jnp.inf)
        _, idx = lax.top_k(score, topk)
        valid = idx <= qpos[None, :, None]
        k_sel = gather(k, idx)
        v_sel = gather(v, idx)
        logits = jnp.einsum('bqhd,bqkd->bhqk', qi, k_sel).astype(jnp.float32) * scale
        logits = jnp.where(valid[:, None], logits, -jnp.inf)
        p = jax.nn.softmax(logits, axis=-1).astype(v.dtype)
        return jnp.einsum('bhqk,bqkd->bqhd', p, v_sel)

    xs = (to_blocks(q, nb), to_blocks(q_idx, nb), to_blocks(w_idx, nb), jnp.arange(nb) * BLOCK)
    out = lax.map(block, xs)
    return out.swapaxes(0, 1).reshape(B, S, H, D)


def causal_depthwise_conv(x, w, b):
    C = x.shape[-1]
    xp = jnp.pad(x, ((0, 0), (CONV_WIDTH - 1, 0), (0, 0)))
    y = lax.conv_general_dilated(xp, w[:, None, :].astype(x.dtype), window_strides=(1,), padding='VALID',
                                 dimension_numbers=('NWC', 'WIO', 'NWC'), feature_group_count=C)
    return y + b.astype(x.dtype)


def ssd_chunked(x, dt, A, Bm, Cm):
    Bsz, S, H, P = x.shape
    G, N = Bm.shape[2], Bm.shape[3]
    R = H // G
    Q = SSM_CHUNK
    nc = S // Q
    xd = (x.astype(jnp.float32) * dt[..., None]).reshape(Bsz, nc, Q, G, R, P)
    a = (dt * A).reshape(Bsz, nc, Q, G, R)
    Bc = Bm.astype(jnp.float32).reshape(Bsz, nc, Q, G, N)
    Cc = Cm.astype(jnp.float32).reshape(Bsz, nc, Q, G, N)
    a_cum = jnp.cumsum(a, axis=2)
    acT = jnp.moveaxis(a_cum, 2, -1)
    causal = jnp.tril(jnp.ones((Q, Q), dtype=bool))
    Lmat = jnp.exp(jnp.where(causal, acT[..., :, None] - acT[..., None, :], -jnp.inf))
    CB = jnp.einsum('bclgn,bcsgn->bcgls', Cc, Bc)
    y_diag = jnp.einsum('bcgrls,bcsgrp->bclgrp', CB[:, :, :, None] * Lmat, xd)
    decay_to_end = jnp.exp(a_cum[:, :, -1:] - a_cum)
    states = jnp.einsum('bclgn,bclgrp->bcgrpn', Bc, xd * decay_to_end[..., None])
    chunk_decay = jnp.exp(a_cum[:, :, -1])

    def step(h, inp):
        st, dec = inp
        return h * dec[..., None, None] + st, h

    h0 = jnp.zeros((Bsz, G, R, P, N), jnp.float32)
    _, h_start = lax.scan(step, h0, (jnp.moveaxis(states, 1, 0), jnp.moveaxis(chunk_decay, 1, 0)))
    h_start = jnp.moveaxis(h_start, 0, 1)
    y_off = jnp.einsum('bclgn,bcgrpn->bclgrp', Cc, h_start) * jnp.exp(a_cum)[..., None]
    return (y_diag + y_off).reshape(Bsz, S, H, P)


def even_mixer(h, w_in, b_f, qn_a, kn_a, qn_b, kn_b, w_out):
    B, S, _ = h.shape
    topk = min(TOPK_MAX, S // 4)
    qa, ka, va, qi, ki, wi, qb, kb, vb, fb = split_cols(h @ w_in, EVEN_COLS)
    qa = rms_norm(qa.reshape(B, S, N_HEADS_A, HEAD_DIM), qn_a)
    ka = rms_norm(ka, kn_a)
    qi = qi.reshape(B, S, N_IDX_HEADS, IDX_DIM)
    oa = dsa_attention(qa, ka, va, qi, ki, wi, topk)
    qb = rms_norm(qb.reshape(B, S, N_HEADS_B, HEAD_DIM), qn_b)
    kb = rms_norm(kb.reshape(B, S, N_HEADS_B, HEAD_DIM), kn_b)
    vb = vb.reshape(B, S, N_HEADS_B, HEAD_DIM)
    log_f = jax.nn.log_sigmoid((fb + b_f).astype(jnp.float32))
    ob = causal_softmax_attention(qb, kb, vb, jnp.cumsum(log_f, axis=1))
    o = jnp.concatenate([oa.reshape(B, S, -1), ob.reshape(B, S, -1)], axis=-1)
    return o @ w_out


def odd_mixer(h, positions, w_in, cq_norm, ckv_norm, w_uq, w_ukv, qn_c, kn_c,
              conv_w, conv_b, dt_bias, a_log, d_skip, gate_norm, w_out):
    B, S, _ = h.shape
    cq, ckv, kr, z, xbc, dt = split_cols(h @ w_in, ODD_COLS)
    q = (rms_norm(cq, cq_norm) @ w_uq).reshape(B, S, N_HEADS_C, QK_NOPE_DIM + QK_ROPE_DIM)
    kv = (rms_norm(ckv, ckv_norm) @ w_ukv).reshape(B, S, N_HEADS_C, QK_NOPE_DIM + V_DIM_C)
    k_nope, v = kv[..., :QK_NOPE_DIM], kv[..., QK_NOPE_DIM:]
    k = jnp.concatenate([k_nope, jnp.broadcast_to(kr[:, :, None, :], (B, S, N_HEADS_C, QK_ROPE_DIM))], axis=-1)
    q = rms_norm(q, qn_c)
    k = rms_norm(k, kn_c)
    q = jnp.concatenate([q[..., :QK_NOPE_DIM], rope(q[..., QK_NOPE_DIM:], positions)], axis=-1)
    k = jnp.concatenate([k[..., :QK_NOPE_DIM], rope(k[..., QK_NOPE_DIM:], positions)], axis=-1)
    oc = causal_softmax_attention(q, k, v)
    xbc = jax.nn.silu(causal_depthwise_conv(xbc, conv_w, conv_b))
    xs, Bm, Cm = split_cols(xbc, (SSM_D_INNER, SSM_GROUPS * SSM_STATE, SSM_GROUPS * SSM_STATE))
    xs = xs.reshape(B, S, SSM_HEADS, SSM_HEAD_DIM)
    Bm = Bm.reshape(B, S, SSM_GROUPS, SSM_STATE)
    Cm = Cm.reshape(B, S, SSM_GROUPS, SSM_STATE)
    dt = jax.nn.softplus((dt + dt_bias).astype(jnp.float32))
    A = -jnp.exp(a_log.astype(jnp.float32))
    y = ssd_chunked(xs, dt, A, Bm, Cm) + d_skip.astype(jnp.float32)[:, None] * xs.astype(jnp.float32)
    y = y.reshape(B, S, SSM_D_INNER) * jax.nn.silu(z.astype(jnp.float32))
    y = rms_norm(y.reshape(B, S, SSM_GROUPS, -1), gate_norm.reshape(SSM_GROUPS, -1))
    y = y.reshape(B, S, SSM_D_INNER).astype(h.dtype)
    o = jnp.concatenate([oc.reshape(B, S, -1), y], axis=-1)
    return o @ w_out


def squared_relu_mlp(h, w1, w2):
    return jnp.square(jax.nn.relu(h @ w1)) @ w2


def setup_inputs(seed: int = 0) -> dict:
    key = jax.random.key(seed)
    ks = jax.random.split(key, 32)
    f32 = jnp.float32

    def nrm(k, shape, scale):
        return jax.random.normal(k, shape, f32) * scale

    def gain(k, shape):
        return 1.0 + 0.02 * jax.random.normal(k, shape, f32)

    dt_init = jnp.exp(jax.random.uniform(ks[20], (N_ODD, SSM_HEADS), f32, np.log(1e-3), np.log(1e-1)))
    return {
        'x': nrm(ks[0], (BATCH, SEQ, D_MODEL), 1.0),
        'positions': jnp.tile(jnp.arange(SEQ, dtype=jnp.int32)[None, :], (BATCH, 1)),
        'ev_norm': gain(ks[1], (N_EVEN, D_MODEL)),
        'ev_w_in': nrm(ks[2], (N_EVEN, D_MODEL, W_IN_EVEN), D_MODEL ** -0.5),
        'ev_b_f': jax.random.uniform(ks[3], (N_EVEN, N_HEADS_B), f32, 1.0, 4.0),
        'ev_qn_a': gain(ks[4], (N_EVEN, HEAD_DIM)),
        'ev_kn_a': gain(ks[5], (N_EVEN, HEAD_DIM)),
        'ev_qn_b': gain(ks[6], (N_EVEN, HEAD_DIM)),
        'ev_kn_b': gain(ks[7], (N_EVEN, HEAD_DIM)),
        'ev_w_out': nrm(ks[8], (N_EVEN, W_OUT_EVEN, D_MODEL), W_OUT_EVEN ** -0.5),
        'od_norm': gain(ks[9], (N_ODD, D_MODEL)),
        'od_w_in': nrm(ks[10], (N_ODD, D_MODEL, W_IN_ODD), D_MODEL ** -0.5),
        'od_cq_norm': gain(ks[11], (N_ODD, Q_LORA)),
        'od_ckv_norm': gain(ks[12], (N_ODD, KV_LORA)),
        'od_w_uq': nrm(ks[13], (N_ODD, Q_LORA, N_HEADS_C * (QK_NOPE_DIM + QK_ROPE_DIM)), Q_LORA ** -0.5),
        'od_w_ukv': nrm(ks[14], (N_ODD, KV_LORA, N_HEADS_C * (QK_NOPE_DIM + V_DIM_C)), KV_LORA ** -0.5),
        'od_qn_c': gain(ks[15], (N_ODD, QK_NOPE_DIM + QK_ROPE_DIM)),
        'od_kn_c': gain(ks[16], (N_ODD, QK_NOPE_DIM + QK_ROPE_DIM)),
        'od_conv_w': nrm(ks[17], (N_ODD, CONV_WIDTH, CONV_DIM), CONV_WIDTH ** -0.5),
        'od_conv_b': nrm(ks[18], (N_ODD, CONV_DIM), 0.02),
        'od_dt_bias': dt_init + jnp.log(-jnp.expm1(-dt_init)),
        'od_a_log': jnp.log(jax.random.uniform(ks[21], (N_ODD, SSM_HEADS), f32, 1.0, 16.0)),
        'od_d_skip': 1.0 + 0.1 * jax.random.normal(ks[22], (N_ODD, SSM_HEADS), f32),
        'od_gate_norm': gain(ks[23], (N_ODD, SSM_D_INNER)),
        'od_w_out': nrm(ks[24], (N_ODD, W_OUT_ODD, D_MODEL), W_OUT_ODD ** -0.5),
        'mlp_norm': gain(ks[25], (DEPTH, D_MODEL)),
        'mlp_w1': nrm(ks[26], (DEPTH, D_MODEL, D_FF), D_MODEL ** -0.5),
        'mlp_w2': nrm(ks[27], (DEPTH, D_FF, D_MODEL), D_FF ** -0.5),
    }


def reference(x, positions, ev_norm, ev_w_in, ev_b_f, ev_qn_a, ev_kn_a, ev_qn_b, ev_kn_b, ev_w_out,
              od_norm, od_w_in, od_cq_norm, od_ckv_norm, od_w_uq, od_w_ukv, od_qn_c, od_kn_c,
              od_conv_w, od_conv_b, od_dt_bias, od_a_log, od_d_skip, od_gate_norm, od_w_out,
              mlp_norm, mlp_w1, mlp_w2):
    for layer in range(DEPTH):
        i = layer // 2
        if layer % 2 == 0:
            x = x + even_mixer(rms_norm(x, ev_norm[i]), ev_w_in[i], ev_b_f[i], ev_qn_a[i], ev_kn_a[i],
                               ev_qn_b[i], ev_kn_b[i], ev_w_out[i])
        else:
            x = x + odd_mixer(rms_norm(x, od_norm[i]), positions, od_w_in[i], od_cq_norm[i], od_ckv_norm[i],
                              od_w_uq[i], od_w_ukv[i], od_qn_c[i], od_kn_c[i], od_conv_w[i], od_conv_b[i],
                              od_dt_bias[i], od_a_log[i], od_d_skip[i], od_gate_norm[i], od_w_out[i])
        x = x + squared_relu_mlp(rms_norm(x, mlp_norm[layer]), mlp_w1[layer], mlp_w2[layer])
    return x
```

```python
import functools

import jax
import jax.numpy as jnp
from jax import lax
from jax.experimental import pallas as pl
from jax.experimental.pallas import tpu as pltpu

F32 = jnp.float32
BF16 = jnp.bfloat16
EPS = 1e-6
NEG_INF = float("-inf")
F32_MAX = 3.4028234663852886e38

D_MODEL = 1024
HEAD_DIM = 128
N_HEADS = 4
N_IDX_HEADS = 8
IDX_DIM = 64
TOPK_MAX = 256
QK_NOPE = 128
QK_ROPE = 64
Q_LORA = 384
KV_LORA = 256
ROPE_THETA = 10000.0
SSM_INNER = 1024
SSM_HEAD_DIM = 64
SSM_HEADS = 16
SSM_GROUPS = 4
SSM_STATE = 128
CONV_WIDTH = 4
CONV_DIM = SSM_INNER + 2 * SSM_GROUPS * SSM_STATE
D_FF = 4 * D_MODEL

LANES = 128
AUG = 2 * LANES
VMEM_LIMIT = 56 * 1024 * 1024

TM_PROJ = 512
TM_MLP = 512
TF_MLP = 512
TQ_FLASH = 512
TK_FLASH = 512
TQ_DSA = 128
TK_DSA = 512
SSD_CHUNK = 128
BISECT_ROUND = 28
BISECT_MAX_ROUNDS = 8

EV_QA, EV_KA, EV_VA, EV_QI, EV_QB, EV_KB, EV_VB, EV_MISC, EV_END = (
    0, 512, 640, 768, 1280, 1792, 2304, 2816, 2944)
MISC_W = 64
MISC_F = 72
OD_CQ, OD_CKV, OD_KR, OD_DT, OD_Z, OD_XBC, OD_END = (0, 384, 640, 768, 896, 1920, 3968)


def _cparams(*sem):
    return pltpu.CompilerParams(dimension_semantics=sem, vmem_limit_bytes=VMEM_LIMIT)


def _resident(shape):
    nd = len(shape)
    return pl.BlockSpec(shape, lambda *_: (0,) * nd, pipeline_mode=pl.Buffered(1))


def _rms(x, g, n=None):
    n = x.shape[-1] if n is None else n
    ss = jnp.sum(x * x, axis=-1, keepdims=True)
    return x * lax.rsqrt(ss * (1.0 / n) + EPS) * g


def _dot(a, b):
    return jnp.dot(a, b, preferred_element_type=F32)


def _dot_nt(a, b):
    return lax.dot_general(a, b, (((1,), (1,)), ((), ())), preferred_element_type=F32)


def _dot_hi(a, b):
    return jnp.dot(a, b, preferred_element_type=F32, precision=lax.Precision.HIGHEST)


def _sigmoid(x):
    return 1.0 / (1.0 + jnp.exp(-x))


def _softplus(x):
    return jnp.maximum(x, 0.0) + jnp.log(1.0 + jnp.exp(-jnp.abs(x)))


def _lane_iota(shape):
    return lax.broadcasted_iota(jnp.int32, shape, len(shape) - 1)


def _tri(n):
    r = lax.broadcasted_iota(jnp.int32, (n, n), 0)
    c = lax.broadcasted_iota(jnp.int32, (n, n), 1)
    return r >= c


def _split3(c):
    hi = c.astype(BF16).astype(F32)
    mid = (c - hi).astype(BF16).astype(F32)
    lo = (c - hi - mid).astype(BF16).astype(F32)
    return hi, mid, lo


def _even_proj_kernel(x_ref, g_ref, w_ref, bf_ref, qna_ref, kna_ref, qnb_ref, knb_ref,
                      qa_ref, ka_ref, va_ref, qi_ref, kd_ref, wi_ref, qb_ref, kb_ref, vb_ref,
                      carry_ref):
    tm = x_ref.shape[1]

    @pl.when(pl.program_id(1) == 0)
    def _():
        carry_ref[...] = jnp.zeros_like(carry_ref)

    xn = _rms(x_ref[0], g_ref[...]).astype(BF16)

    def proj(lo, hi):
        return _dot(xn, w_ref[:, lo:hi])

    scale = HEAD_DIM ** -0.5
    qa = proj(EV_QA, EV_KA)
    for h in range(N_HEADS):
        sl = slice(h * HEAD_DIM, (h + 1) * HEAD_DIM)
        qa_ref[0, :, sl] = (_rms(qa[:, sl], qna_ref[...]) * scale).astype(BF16)
    ka_ref[0] = _rms(proj(EV_KA, EV_VA), kna_ref[...]).astype(BF16)
    va_ref[0] = proj(EV_VA, EV_QI).astype(BF16)
    qi_ref[0] = proj(EV_QI, EV_QB).astype(BF16)
    vb_ref[0] = proj(EV_VB, EV_MISC).astype(BF16)

    misc = proj(EV_MISC, EV_END)
    lane = _lane_iota((tm, LANES))
    kd_ref[0] = jnp.where(lane < IDX_DIM, misc, pltpu.roll(misc, IDX_DIM, 1)).astype(BF16)
    wi_ref[0] = misc

    zf = misc + bf_ref[...]
    logf = jnp.minimum(zf, 0.0) - jnp.log(1.0 + jnp.exp(-jnp.abs(zf)))
    logf = jnp.where((lane >= MISC_F) & (lane < MISC_F + N_HEADS), logf, 0.0)
    csum = _dot_hi(_tri(tm).astype(F32), logf) + carry_ref[...]
    carry_ref[...] = csum[tm - 1:tm, :]

    qb = proj(EV_QB, EV_KB)
    kb = proj(EV_KB, EV_VB)
    one = jnp.ones((tm, LANES), F32)
    zero = jnp.zeros((tm, LANES), F32)
    for h in range(N_HEADS):
        sl = slice(h * HEAD_DIM, (h + 1) * HEAD_DIM)
        c = jnp.broadcast_to(csum[:, MISC_F + h:MISC_F + h + 1], (tm, LANES))
        hi, mid, lo = _split3(c)
        aq = jnp.where(lane == 0, hi, jnp.where(lane == 1, mid, jnp.where(lane == 2, lo,
             jnp.where(lane < 6, one, zero))))
        ak = jnp.where(lane < 3, one, jnp.where(lane == 3, -hi, jnp.where(lane == 4, -mid,
             jnp.where(lane == 5, -lo, zero))))
        qb_ref[0, :, h * AUG:h * AUG + LANES] = (_rms(qb[:, sl], qnb_ref[...]) * scale).astype(BF16)
        qb_ref[0, :, h * AUG + LANES:(h + 1) * AUG] = aq.astype(BF16)
        kb_ref[0, :, h * AUG:h * AUG + LANES] = _rms(kb[:, sl], knb_ref[...]).astype(BF16)
        kb_ref[0, :, h * AUG + LANES:(h + 1) * AUG] = ak.astype(BF16)


def _even_proj(x, g, w, bf, qna, kna, qnb, knb):
    B, S, D = x.shape
    tm = TM_PROJ
    tok = lambda c: pl.BlockSpec((1, tm, c), lambda b, j: (b, j, 0))
    row = lambda c: pl.BlockSpec((1, c), lambda b, j: (0, 0))
    out_cols = (512, 128, 128, 512, 128, 128, N_HEADS * AUG, N_HEADS * AUG, 512)
    out_dt = (BF16, BF16, BF16, BF16, BF16, F32, BF16, BF16, BF16)
    return pl.pallas_call(
        _even_proj_kernel,
        grid=(B, S // tm),
        in_specs=[tok(D), row(D), _resident(w.shape), row(LANES), row(LANES), row(LANES),
                  row(LANES), row(LANES)],
        out_specs=[tok(c) for c in out_cols],
        out_shape=[jax.ShapeDtypeStruct((B, S, c), dt) for c, dt in zip(out_cols, out_dt)],
        scratch_shapes=[pltpu.VMEM((1, LANES), F32)],
        compiler_params=_cparams("arbitrary", "arbitrary"),
        name="even_proj",
    )(x, g, w, bf, qna, kna, qnb, knb)


def _flash_kernel(q_ref, k_ref, v_ref, o_ref, m_ref, l_ref, acc_ref, *, tq, tk):
    i = pl.program_id(2)
    q = q_ref[0]
    m_ref[...] = jnp.full_like(m_ref, NEG_INF)
    l_ref[...] = jnp.zeros_like(l_ref)
    acc_ref[...] = jnp.zeros_like(acc_ref)

    def step(j, masked):
        start = pl.multiple_of(j * tk, tk)
        s = _dot_nt(q, k_ref[0, pl.ds(start, tk), :])
        if masked:
            row = i * tq + lax.broadcasted_iota(jnp.int32, (tq, tk), 0)
            col = j * tk + lax.broadcasted_iota(jnp.int32, (tq, tk), 1)
            s = jnp.where(col <= row, s, NEG_INF)
        m_prev = m_ref[...]
        m_new = jnp.maximum(m_prev, jnp.max(s, axis=-1, keepdims=True))
        alpha = jnp.exp(m_prev - m_new)
        p = jnp.exp(s - m_new)
        l_ref[...] = alpha * l_ref[...] + jnp.sum(p, axis=-1, keepdims=True)
        acc_ref[...] = alpha * acc_ref[...] + _dot(p.astype(BF16), v_ref[0, pl.ds(start, tk), :])
        m_ref[...] = m_new

    ratio = tq // tk
    nfull = i * ratio

    def body(j, c):
        step(j, False)
        return c

    lax.fori_loop(0, nfull, body, 0)
    for d in range(ratio):
        step(nfull + d, True)
    o_ref[0] = (acc_ref[...] / l_ref[...]).astype(o_ref.dtype)


def _flash(q, k, v):
    B, S, _ = q.shape
    tq, tk = TQ_FLASH, TK_FLASH
    return pl.pallas_call(
        functools.partial(_flash_kernel, tq=tq, tk=tk),
        grid=(B, N_HEADS, S // tq),
        in_specs=[pl.BlockSpec((1, tq, AUG), lambda b, h, i: (b, i, h)),
                  pl.BlockSpec((1, S, AUG), lambda b, h, i: (b, 0, h)),
                  pl.BlockSpec((1, S, HEAD_DIM), lambda b, h, i: (b, 0, h))],
        out_specs=pl.BlockSpec((1, tq, HEAD_DIM), lambda b, h, i: (b, i, h)),
        out_shape=jax.ShapeDtypeStruct((B, S, N_HEADS * HEAD_DIM), BF16),
        scratch_shapes=[pltpu.VMEM((tq, 1), F32), pltpu.VMEM((tq, 1), F32),
                        pltpu.VMEM((tq, HEAD_DIM), F32)],
        compiler_params=_cparams("arbitrary", "arbitrary", "arbitrary"),
        name="flash_attn",
    )(q, k, v)


def _dsa_kernel(qa_ref, ka_ref, va_ref, qi_ref, kd_ref, wi_ref, o_ref,
                sc_ref, qstk_ref, m_ref, l_ref, acc_ref, *, tkc, topk):
    tq = qa_ref.shape[1]
    i = pl.program_id(1)
    nkc = (i * tq + tq + tkc - 1) // tkc
    kf = float(topk)
    lane = _lane_iota((tq, LANES))
    rowi = i * tq + lax.broadcasted_iota(jnp.int32, (tq, tkc), 0)
    coli = lax.broadcasted_iota(jnp.int32, (tq, tkc), 1)

    q8 = qi_ref[0]
    for j in range(N_IDX_HEADS):
        grp = q8[:, LANES * (j // 2):LANES * (j // 2 + 1)]
        keep = (lane < IDX_DIM) if j % 2 == 0 else (lane >= IDX_DIM)
        qstk_ref[tq * j:tq * (j + 1), :] = jnp.where(keep, grp, jnp.zeros_like(grp))
    wmisc = wi_ref[0]
    wcols = [wmisc[:, MISC_W + j:MISC_W + j + 1] for j in range(N_IDX_HEADS)]

    def chunk(ref, kc):
        return ref[0, pl.ds(pl.multiple_of(kc * tkc, tkc), tkc), :]

    def p1(kc, carry):
        rmax, rmin = carry
        s8 = _dot_nt(qstk_ref[...], chunk(kd_ref, kc))
        score = wcols[0] * jnp.maximum(s8[0:tq], 0.0)
        for j in range(1, N_IDX_HEADS):
            score = score + wcols[j] * jnp.maximum(s8[tq * j:tq * (j + 1)], 0.0)
        adm = (kc * tkc + coli) <= rowi
        sc_ref[kc] = jnp.where(adm, score, NEG_INF)
        rmax = jnp.maximum(rmax, jnp.max(jnp.where(adm, score, NEG_INF), axis=-1, keepdims=True))
        rmin = jnp.minimum(rmin, jnp.min(jnp.where(adm, score, -NEG_INF), axis=-1, keepdims=True))
        return rmax, rmin

    rmax, rmin = lax.fori_loop(0, nkc, p1, (jnp.full((tq, 1), NEG_INF, F32),
                                            jnp.full((tq, 1), -NEG_INF, F32)))

    def row_count(pred):
        def cb(kc, c):
            return c + jnp.sum(pred(sc_ref[kc], kc).astype(F32), axis=-1, keepdims=True)
        return lax.fori_loop(0, nkc, cb, jnp.zeros((tq, 1), F32))

    def any_row(flag):
        return jnp.max(flag.astype(jnp.int32))

    def select_threshold():
        hi0 = rmax + (jnp.abs(rmax) * 1.2e-7 + 1e-37)

        def bis_cond(st):
            it, _, _, _, unres = st
            return (unres > 0) & (it < BISECT_ROUND)

        def bis_body(st):
            it, lo, hi, cnt, _ = st
            mid = lo + (hi - lo) * 0.5
            c = row_count(lambda blk, kc: blk >= mid)
            up = c >= kf
            lo = jnp.where(up, mid, lo)
            hi = jnp.where(up, hi, mid)
            cnt = jnp.where(up, c, cnt)
            return it + 1, lo, hi, cnt, any_row(cnt != kf)

        def resolve_ties(lo, cnt):
            def vmin(kc, v):
                blk = sc_ref[kc]
                return jnp.minimum(v, jnp.min(jnp.where(blk >= lo, blk, -NEG_INF), axis=-1,
                                              keepdims=True))
            v = lax.fori_loop(0, nkc, vmin, jnp.full((tq, 1), -NEG_INF, F32))
            cgt = row_count(lambda blk, kc: blk > v)
            tie = (cnt != kf) & (cgt < kf)
            need = kf - cgt
            nbits = max(1, int(sc_ref.shape[0] * tkc - 1).bit_length())

            def jb(_, st):
                jlo, jhi = st
                jm = lax.shift_right_arithmetic(jlo + jhi, jnp.ones_like(jlo))
                c = row_count(lambda blk, kc: (blk == v) & ((kc * tkc + coli) <= jm))
                ok = c >= need
                return jnp.where(ok, jlo, jm), jnp.where(ok, jm, jhi)

            _, jcut = lax.fori_loop(0, nbits, jb, (jnp.full((tq, 1), -1, jnp.int32),
                                                   jnp.full((tq, 1), sc_ref.shape[0] * tkc - 1,
                                                            jnp.int32)))

            def drop(kc, c):
                blk = sc_ref[kc]
                cut = tie & (blk == v) & ((kc * tkc + coli) > jcut)
                sc_ref[kc] = jnp.where(cut, NEG_INF, blk)
                return c
            lax.fori_loop(0, nkc, drop, 0)
            return jnp.where(tie, v, lo), jnp.where(tie, kf, cnt)

        def round_cond(st):
            r, _, _, _, unres = st
            return (unres > 0) & (r < BISECT_MAX_ROUNDS)

        def round_body(st):
            r, lo, hi, cnt, unres = st
            _, lo, hi, cnt, unres = lax.while_loop(bis_cond, bis_body, (0, lo, hi, cnt, unres))
            lo, cnt = lax.cond(unres > 0, resolve_ties, lambda a, b: (a, b), lo, cnt)
            return r + 1, lo, hi, cnt, any_row(cnt != kf)

        cnt0 = (rowi[:, 0:1] + 1).astype(F32)
        st = lax.while_loop(round_cond, round_body, (0, rmin, hi0, cnt0, any_row(cnt0 != kf)))
        return st[1]

    thr = lax.cond((i + 1) * tq > topk, select_threshold,
                   lambda: jnp.full((tq, 1), -F32_MAX, F32))

    qa = qa_ref[0]
    qs = jnp.concatenate([qa[:, h * HEAD_DIM:(h + 1) * HEAD_DIM] for h in range(N_HEADS)], axis=0)
    m_ref[...] = jnp.full_like(m_ref, NEG_INF)
    l_ref[...] = jnp.zeros_like(l_ref)
    acc_ref[...] = jnp.zeros_like(acc_ref)

    def p3(kc, c):
        s = _dot_nt(qs, chunk(ka_ref, kc))
        sel = sc_ref[kc] >= thr
        ps = []
        for h in range(N_HEADS):
            rs = slice(h * tq, (h + 1) * tq)
            sh = jnp.where(sel, s[rs], NEG_INF)
            m_prev = m_ref[rs]
            m_new = jnp.maximum(m_prev, jnp.max(sh, axis=-1, keepdims=True))
            m_fin = jnp.where(m_new == NEG_INF, 0.0, m_new)
            alpha = jnp.exp(m_prev - m_fin)
            p = jnp.exp(sh - m_fin)
            l_ref[rs] = alpha * l_ref[rs] + jnp.sum(p, axis=-1, keepdims=True)
            acc_ref[rs] = alpha * acc_ref[rs]
            m_ref[rs] = m_new
            ps.append(p.astype(BF16))
        acc_ref[...] += _dot(jnp.concatenate(ps, axis=0), chunk(va_ref, kc))
        return c

    lax.fori_loop(0, nkc, p3, 0)
    for h in range(N_HEADS):
        rs = slice(h * tq, (h + 1) * tq)
        o_ref[0, :, h * HEAD_DIM:(h + 1) * HEAD_DIM] = (acc_ref[rs] / l_ref[rs]).astype(o_ref.dtype)


def _dsa(qa, ka, va, qi, kd, wi):
    B, S, _ = qa.shape
    tq, tkc = TQ_DSA, min(TK_DSA, S)
    topk = min(TOPK_MAX, S // 4)
    qblk = lambda c: pl.BlockSpec((1, tq, c), lambda b, i: (b, i, 0))
    full = lambda c: pl.BlockSpec((1, S, c), lambda b, i: (b, 0, 0))
    return pl.pallas_call(
        functools.partial(_dsa_kernel, tkc=tkc, topk=topk),
        grid=(B, S // tq),
        in_specs=[qblk(512), full(128), full(128), qblk(512), full(128), qblk(128)],
        out_specs=qblk(512),
        out_shape=jax.ShapeDtypeStruct((B, S, N_HEADS * HEAD_DIM), BF16),
        scratch_shapes=[pltpu.VMEM((S // tkc, tq, tkc), F32),
                        pltpu.VMEM((N_IDX_HEADS * tq, LANES), BF16),
                        pltpu.VMEM((N_HEADS * tq, 1), F32), pltpu.VMEM((N_HEADS * tq, 1), F32),
                        pltpu.VMEM((N_HEADS * tq, HEAD_DIM), F32)],
        compiler_params=_cparams("arbitrary", "arbitrary"),
        name="dsa_attn",
    )(qa, ka, va, qi, kd, wi)


def _rope_table_kernel(pos_ref, inv_ref, cos_ref, sin_ref):
    tm = pos_ref.shape[1]
    lane = _lane_iota((tm, LANES))
    ang = pos_ref[0].astype(F32) * inv_ref[...]
    half = QK_ROPE // 2
    cos_ref[0] = jnp.where(lane < QK_ROPE, jnp.cos(ang), 0.0)
    sn = jnp.sin(ang)
    sin_ref[0] = jnp.where(lane < half, -sn, jnp.where(lane < QK_ROPE, sn, 0.0))


def _rope_tables(positions):
    B, S = positions.shape
    tm = TM_PROJ
    half = QK_ROPE // 2
    inv = ROPE_THETA ** (-jnp.arange(half, dtype=F32) / half)
    inv = jnp.concatenate([inv, inv, jnp.zeros((LANES - QK_ROPE,), F32)])[None, :]
    return pl.pallas_call(
        _rope_table_kernel,
        grid=(B, S // tm),
        in_specs=[pl.BlockSpec((1, tm, 1), lambda b, j: (b, j, 0)),
                  pl.BlockSpec((1, LANES), lambda b, j: (0, 0))],
        out_specs=[pl.BlockSpec((1, tm, LANES), lambda b, j: (b, j, 0))] * 2,
        out_shape=[jax.ShapeDtypeStruct((B, S, LANES), F32)] * 2,
        compiler_params=_cparams("arbitrary", "arbitrary"),
        name="rope_tables",
    )(positions[:, :, None], inv)


def _rope(t, cos, sin):
    lane = _lane_iota(t.shape)
    half = QK_ROPE // 2
    swap = jnp.where(lane < half, pltpu.roll(t, LANES - half, 1), pltpu.roll(t, half, 1))
    return t * cos + swap * sin


def _odd_proj_kernel(x_ref, g_ref, w_ref, cqn_ref, ckvn_ref, wuq_ref, wukv_ref, qnc_ref, knc_ref,
                     cos_ref, sin_ref, q_ref, k_ref, v_ref, z_ref, xbc_ref, dt_ref):
    xn = _rms(x_ref[0], g_ref[...]).astype(BF16)

    def proj(lo, hi):
        return _dot(xn, w_ref[:, lo:hi])

    cos = cos_ref[0]
    sin = sin_ref[0]
    n_qk = QK_NOPE + QK_ROPE
    scale = n_qk ** -0.5

    cq = _rms(proj(OD_CQ, OD_CKV), cqn_ref[...]).astype(BF16)
    q = _dot(cq, wuq_ref[...])
    ckv = _rms(proj(OD_CKV, OD_KR), ckvn_ref[...]).astype(BF16)
    kv = _dot(ckv, wukv_ref[...])
    kr = proj(OD_KR, OD_DT)
    kr_ss = jnp.sum(kr * kr, axis=-1, keepdims=True)
    kr_rot = _rope(kr * knc_ref[:, LANES:], cos, sin)
    for h in range(N_HEADS):
        qh = q[:, h * AUG:(h + 1) * AUG]
        qh = _rms(qh, qnc_ref[...], n_qk) * scale
        q_ref[0, :, h * AUG:h * AUG + LANES] = qh[:, :LANES].astype(BF16)
        q_ref[0, :, h * AUG + LANES:(h + 1) * AUG] = _rope(qh[:, LANES:], cos, sin).astype(BF16)
        kn = kv[:, h * AUG:h * AUG + LANES]
        r = lax.rsqrt((jnp.sum(kn * kn, axis=-1, keepdims=True) + kr_ss) * (1.0 / n_qk) + EPS)
        k_ref[0, :, h * AUG:h * AUG + LANES] = (kn * r * knc_ref[:, :LANES]).astype(BF16)
        k_ref[0, :, h * AUG + LANES:(h + 1) * AUG] = (kr_rot * r).astype(BF16)
        v_ref[0, :, h * HEAD_DIM:(h + 1) * HEAD_DIM] = kv[:, h * AUG + LANES:(h + 1) * AUG].astype(BF16)

    dt_ref[0] = proj(OD_DT, OD_Z)
    z_ref[0] = proj(OD_Z, OD_XBC)
    xbc_ref[0] = proj(OD_XBC, OD_END)


def _odd_proj(x, g, w, cqn, ckvn, wuq, wukv, qnc, knc, cos, sin):
    B, S, D = x.shape
    tm = TM_PROJ
    tok = lambda c: pl.BlockSpec((1, tm, c), lambda b, j: (b, j, 0))
    row = lambda c: pl.BlockSpec((1, c), lambda b, j: (0, 0))
    out_cols = (N_HEADS * AUG, N_HEADS * AUG, N_HEADS * HEAD_DIM, SSM_INNER, CONV_DIM, LANES)
    out_dt = (BF16, BF16, BF16, F32, F32, F32)
    return pl.pallas_call(
        _odd_proj_kernel,
        grid=(B, S // tm),
        in_specs=[tok(D), row(D), _resident(w.shape), row(Q_LORA), row(KV_LORA),
                  _resident(wuq.shape), _resident(wukv.shape), row(AUG), row(AUG),
                  tok(LANES), tok(LANES)],
        out_specs=[tok(c) for c in out_cols],
        out_shape=[jax.ShapeDtypeStruct((B, S, c), dt) for c, dt in zip(out_cols, out_dt)],
        compiler_params=_cparams("arbitrary", "arbitrary"),
        name="odd_proj",
    )(x, g, w, cqn, ckvn, wuq, wukv, qnc, knc, cos, sin)


def _ssd_kernel(xbc_ref, z_ref, dt_ref, cw_ref, cb_ref, dtb_ref, alog_ref, dskip_ref, gn_ref,
                y_ref, xe_ref, st_ref):
    q = xbc_ref.shape[1]
    halo = 8

    @pl.when(pl.program_id(1) == 0)
    def _():
        xe_ref[0:halo, :] = jnp.zeros((halo, CONV_DIM), F32)
        st_ref[...] = jnp.zeros_like(st_ref)

    xe_ref[halo:halo + q, :] = xbc_ref[0]
    conv = cb_ref[...]
    for w in range(CONV_WIDTH):
        off = halo - (CONV_WIDTH - 1) + w
        conv = conv + xe_ref[off:off + q, :] * cw_ref[w:w + 1, :]
    xe_ref[0:halo, :] = xe_ref[q:q + halo, :]
    act = conv * _sigmoid(conv)
    xs = act[:, :SSM_INNER]
    gs = SSM_STATE
    bm = [act[:, SSM_INNER + g * gs:SSM_INNER + (g + 1) * gs] for g in range(SSM_GROUPS)]
    cm = [act[:, SSM_INNER + (SSM_GROUPS + g) * gs:SSM_INNER + (SSM_GROUPS + g + 1) * gs]
          for g in range(SSM_GROUPS)]

    lane = _lane_iota((q, LANES))
    dt = _softplus(dt_ref[0] + dtb_ref[...])
    a = jnp.where(lane < SSM_HEADS, dt * (-jnp.exp(alog_ref[...])), 0.0)
    tri = _tri(q)
    trif = tri.astype(F32)
    acum = _dot_hi(trif, a)
    acum_t = lax.dot_general(a.T, trif, (((1,), (1,)), ((), ())), preferred_element_type=F32,
                             precision=lax.Precision.HIGHEST)
    last = acum[q - 1:q, :]
    dte = jnp.exp(last - acum)
    eac = jnp.exp(acum)
    cdec = jnp.exp(last)

    def expand(mat):
        rows = mat.shape[0]
        ln = _lane_iota((rows, LANES))
        parts = []
        for p in range(SSM_HEADS // 2):
            lo = jnp.broadcast_to(mat[:, 2 * p:2 * p + 1], (rows, LANES))
            hi = jnp.broadcast_to(mat[:, 2 * p + 1:2 * p + 2], (rows, LANES))
            parts.append(jnp.where(ln < SSM_HEAD_DIM, lo, hi))
        return jnp.concatenate(parts, axis=1)

    xd = xs * expand(dt)
    eac_x = expand(eac)
    xdd = (xd * expand(dte)).astype(BF16)
    xd16 = xd.astype(BF16)
    zero16 = jnp.zeros((q, LANES), BF16)

    rpg = SSM_HEADS // SSM_GROUPS
    gw = rpg * SSM_HEAD_DIM
    y_parts = []
    new_states = []
    for g in range(SSM_GROUPS):
        b16 = bm[g].astype(BF16)
        c16 = cm[g].astype(BF16)
        cb = _dot_nt(c16, b16)
        st_old = st_ref[:, g * gw:(g + 1) * gw]
        y_off = _dot(c16, st_old.astype(BF16)) * eac_x[:, g * gw:(g + 1) * gw]
        y_diag = []
        for pp in range(rpg // 2):
            p = g * (rpg // 2) + pp
            xp = xd16[:, p * LANES:(p + 1) * LANES]
            acc = None
            for e in range(2):
                h = 2 * p + e
                diff = acum[:, h:h + 1] - acum_t[h:h + 1, :]
                lm = jnp.exp(jnp.where(tri, diff, NEG_INF))
                gmat = (cb * lm).astype(BF16)
                keep = (lane < SSM_HEAD_DIM) if e == 0 else (lane >= SSM_HEAD_DIM)
                part = _dot(gmat, jnp.where(keep, xp, zero16))
                acc = part if acc is None else acc + part
            y_diag.append(acc)
        y_parts.append(jnp.concatenate(y_diag, axis=1) + y_off)
        st_new = _dot(bm[g].T.astype(BF16), xdd[:, g * gw:(g + 1) * gw])
        new_states.append(st_new)
    cdec_x = expand(cdec)
    for g in range(SSM_GROUPS):
        sl = slice(g * gw, (g + 1) * gw)
        st_ref[:, sl] = st_ref[:, sl] * cdec_x[:, sl] + new_states[g]

    y = jnp.concatenate(y_parts, axis=1) + dskip_ref[...] * xs
    zz = z_ref[0]
    y = y * (zz * _sigmoid(zz))
    for g in range(SSM_GROUPS):
        sl = slice(g * gw, (g + 1) * gw)
        y_ref[0, :, sl] = _rms(y[:, sl], gn_ref[:, sl]).astype(y_ref.dtype)


def _ssd(xbc, z, dt, cw, cb, dtb, alog, dskip, gn):
    B, S, _ = xbc.shape
    q = SSD_CHUNK
    tok = lambda c: pl.BlockSpec((1, q, c), lambda b, j: (b, j, 0))
    row = lambda r, c: pl.BlockSpec((r, c), lambda b, j: (0, 0))
    return pl.pallas_call(
        _ssd_kernel,
        grid=(B, S // q),
        in_specs=[tok(CONV_DIM), tok(SSM_INNER), tok(LANES), row(CONV_WIDTH, CONV_DIM),
                  row(1, CONV_DIM), row(1, LANES), row(1, LANES), row(1, SSM_INNER),
                  row(1, SSM_INNER)],
        out_specs=tok(SSM_INNER),
        out_shape=jax.ShapeDtypeStruct((B, S, SSM_INNER), BF16),
        scratch_shapes=[pltpu.VMEM((q + 8, CONV_DIM), F32), pltpu.VMEM((SSM_STATE, SSM_INNER), F32)],
        compiler_params=_cparams("arbitrary", "arbitrary"),
        name="ssd_scan",
    )(xbc, z, dt, cw, cb, dtb, alog, dskip, gn)


def _out_mlp_kernel(x_ref, o1_ref, o2_ref, wo1_ref, wo2_ref, g_ref, w1_ref, w2_ref, y_ref,
                    xn_ref, acc_ref, *, tf):
    xnew = x_ref[...] + _dot(o1_ref[...], wo1_ref[...]) + _dot(o2_ref[...], wo2_ref[...])
    xn_ref[...] = _rms(xnew, g_ref[...]).astype(BF16)
    acc_ref[...] = xnew

    def body(c, carry):
        cs = pl.multiple_of(c * tf, tf)
        h = _dot(xn_ref[...], w1_ref[:, pl.ds(cs, tf)])
        a = jnp.square(jnp.maximum(h, 0.0)).astype(BF16)
        acc_ref[...] += _dot(a, w2_ref[pl.ds(cs, tf), :])
        return carry

    lax.fori_loop(0, w1_ref.shape[1] // tf, body, 0)
    y_ref[...] = acc_ref[...]


def _out_mlp(x, o1, o2, wo1, wo2, g, w1, w2):
    T, D = x.shape
    tm = TM_MLP
    tok = lambda c: pl.BlockSpec((tm, c), lambda i: (i, 0))
    return pl.pallas_call(
        functools.partial(_out_mlp_kernel, tf=TF_MLP),
        grid=(T // tm,),
        in_specs=[tok(D), tok(o1.shape[1]), tok(o2.shape[1]), _resident(wo1.shape),
                  _resident(wo2.shape), pl.BlockSpec((1, D), lambda i: (0, 0)),
                  _resident(w1.shape), _resident(w2.shape)],
        out_specs=tok(D),
        out_shape=jax.ShapeDtypeStruct((T, D), F32),
        scratch_shapes=[pltpu.VMEM((tm, D), BF16), pltpu.VMEM((tm, D), F32)],
        compiler_params=_cparams("arbitrary"),
        name="out_mlp",
    )(x, o1, o2, wo1, wo2, g, w1, w2)


def _pack_even_w(w):
    qa, ka, va, qi, ki, wi, qb, kb, vb, fb = jnp.split(
        w, [512, 640, 768, 1280, 1344, 1352, 1864, 2376, 2888], axis=1)
    pad = jnp.zeros((w.shape[0], LANES - IDX_DIM - N_IDX_HEADS - N_HEADS), w.dtype)
    return jnp.concatenate([qa, ka, va, qi, qb, kb, vb, ki, wi, fb, pad], axis=1).astype(BF16)


def _pack_odd_w(w):
    cq, ckv, kr, z, xbc, dt = jnp.split(w, [384, 640, 704, 1728, 3776], axis=1)
    zpad = lambda n: jnp.zeros((w.shape[0], n), w.dtype)
    return jnp.concatenate([cq, ckv, kr, zpad(LANES - QK_ROPE), dt, zpad(LANES - SSM_HEADS), z, xbc],
                           axis=1).astype(BF16)


def _pack_wuq(w):
    n_qk = QK_NOPE + QK_ROPE
    w = w.reshape(w.shape[0], N_HEADS, n_qk)
    w = jnp.pad(w, ((0, 0), (0, 0), (0, AUG - n_qk)))
    return w.reshape(w.shape[0], N_HEADS * AUG).astype(BF16)


def _row(v, width=None):
    v = v.astype(F32)[None, :]
    if width is not None and width > v.shape[1]:
        v = jnp.pad(v, ((0, 0), (0, width - v.shape[1])))
    return v


def kernel(x, positions, ev_norm, ev_w_in, ev_b_f, ev_qn_a, ev_kn_a, ev_qn_b, ev_kn_b, ev_w_out,
           od_norm, od_w_in, od_cq_norm, od_ckv_norm, od_w_uq, od_w_ukv, od_qn_c, od_kn_c,
           od_conv_w, od_conv_b, od_dt_bias, od_a_log, od_d_skip, od_gate_norm, od_w_out,
           mlp_norm, mlp_w1, mlp_w2):
    B, S, D = x.shape
    depth = mlp_w1.shape[0]
    cos = sin = None
    for layer in range(depth):
        i = layer // 2
        if layer % 2 == 0:
            bf = jnp.zeros((1, LANES), F32).at[0, MISC_F:MISC_F + N_HEADS].set(ev_b_f[i])
            qa, ka, va, qi, kd, wi, qb, kb, vb = _even_proj(
                x, _row(ev_norm[i]), _pack_even_w(ev_w_in[i]), bf, _row(ev_qn_a[i]),
                _row(ev_kn_a[i]), _row(ev_qn_b[i]), _row(ev_kn_b[i]))
            o1 = _dsa(qa, ka, va, qi, kd, wi)
            o2 = _flash(qb, kb, vb)
            wo = ev_w_out[i].astype(BF16)
        else:
            if cos is None:
                cos, sin = _rope_tables(positions)
            q, k, v, z, xbc, dt = _odd_proj(
                x, _row(od_norm[i]), _pack_odd_w(od_w_in[i]), _row(od_cq_norm[i]),
                _row(od_ckv_norm[i]), _pack_wuq(od_w_uq[i]), od_w_ukv[i].astype(BF16),
                _row(od_qn_c[i], AUG), _row(od_kn_c[i], AUG), cos, sin)
            o1 = _flash(q, k, v)
            o2 = _ssd(xbc, z, dt, od_conv_w[i].astype(F32), _row(od_conv_b[i]),
                      _row(od_dt_bias[i], LANES), _row(od_a_log[i], LANES),
                      _row(jnp.repeat(od_d_skip[i], SSM_HEAD_DIM)), _row(od_gate_norm[i]))
            wo = od_w_out[i].astype(BF16)
        k1 = o1.shape[-1]
        x = _out_mlp(x.reshape(B * S, D), o1.reshape(B * S, k1), o2.reshape(B * S, -1),
                     wo[:k1], wo[k1:], _row(mlp_norm[layer]), mlp_w1[layer].astype(BF16),
                     mlp_w2[layer].astype(BF16)).reshape(B, S, D)
    return x
```

```python
import functools

import jax
import jax.numpy as jnp
from jax import lax
from jax.experimental import pallas as pl
from jax.experimental.pallas import tpu as pltpu

F32 = jnp.float32
BF16 = jnp.bfloat16
EPS = 1e-6
NEG_INF = float("-inf")
F32_MAX = 3.4028234663852886e38
LOG2E = 1.4426950408889634

D_MODEL = 1024
HEAD_DIM = 128
N_HEADS = 4
N_IDX_HEADS = 8
IDX_DIM = 64
TOPK_MAX = 256
QK_NOPE = 128
QK_ROPE = 64
Q_LORA = 384
KV_LORA = 256
ROPE_THETA = 10000.0
SSM_INNER = 1024
SSM_HEAD_DIM = 64
SSM_HEADS = 16
SSM_GROUPS = 4
SSM_STATE = 128
CONV_WIDTH = 4
CONV_DIM = SSM_INNER + 2 * SSM_GROUPS * SSM_STATE
D_FF = 4 * D_MODEL

LANES = 128
AUG = 2 * LANES
VMEM_LIMIT = 56 * 1024 * 1024

TM_PROJ = 512
TM_MLP = 512
TF_MLP = 512
TQ_FLASH = 512
TK_FLASH = 512
TQ_DSA = 256
TK_DSA = 512
SSD_CHUNK = 128
BISECT_GROUP = 4
BISECT_ROUND = 28
BISECT_MAX_ROUNDS = 8

EV_QA, EV_KA, EV_VA, EV_QI, EV_QB, EV_KB, EV_VB, EV_MISC, EV_END = (
    0, 512, 640, 768, 1280, 1792, 2304, 2816, 2944)
MISC_W = 64
MISC_F = 72
OD_CQ, OD_CKV, OD_KR, OD_DT, OD_Z, OD_XBC, OD_END = (0, 384, 640, 768, 896, 1920, 3968)


def _cparams(*sem):
    return pltpu.CompilerParams(dimension_semantics=sem, vmem_limit_bytes=VMEM_LIMIT)


def _resident(shape):
    nd = len(shape)
    return pl.BlockSpec(shape, lambda *_: (0,) * nd, pipeline_mode=pl.Buffered(1))


def _rms(x, g, n=None):
    n = x.shape[-1] if n is None else n
    ss = jnp.sum(x * x, axis=-1, keepdims=True)
    return x * lax.rsqrt(ss * (1.0 / n) + EPS) * g


def _dot(a, b):
    return jnp.dot(a, b, preferred_element_type=F32)


def _dot_nt(a, b):
    return lax.dot_general(a, b, (((1,), (1,)), ((), ())), preferred_element_type=F32)


def _dot_hi(a, b):
    return jnp.dot(a, b, preferred_element_type=F32, precision=lax.Precision.HIGHEST)


def _sigmoid(x):
    return 1.0 / (1.0 + jnp.exp(-x))


def _softplus(x):
    return jnp.maximum(x, 0.0) + jnp.log(1.0 + jnp.exp(-jnp.abs(x)))


def _lane_iota(shape):
    return lax.broadcasted_iota(jnp.int32, shape, len(shape) - 1)


def _tri(n):
    r = lax.broadcasted_iota(jnp.int32, (n, n), 0)
    c = lax.broadcasted_iota(jnp.int32, (n, n), 1)
    return r >= c


def _split3(c):
    hi = c.astype(BF16).astype(F32)
    mid = (c - hi).astype(BF16).astype(F32)
    lo = (c - hi - mid).astype(BF16).astype(F32)
    return hi, mid, lo


def _even_proj_kernel(x_ref, g_ref, w_ref, bf_ref, qna_ref, kna_ref, qnb_ref, knb_ref,
                      qa_ref, ka_ref, va_ref, qi_ref, kd_ref, wi_ref, qb_ref, kb_ref, vb_ref,
                      carry_ref):
    tm = x_ref.shape[1]

    @pl.when(pl.program_id(1) == 0)
    def _():
        carry_ref[...] = jnp.zeros_like(carry_ref)

    xn = _rms(x_ref[0], g_ref[...]).astype(BF16)

    def proj(lo, hi):
        return _dot(xn, w_ref[:, lo:hi])

    scale = HEAD_DIM ** -0.5 * LOG2E
    lane = _lane_iota((tm, LANES))
    ones_col = jnp.where(lane == 0, 1.0, 0.0).astype(BF16)
    qa = proj(EV_QA, EV_KA)
    for h in range(N_HEADS):
        sl = slice(h * HEAD_DIM, (h + 1) * HEAD_DIM)
        qa_ref[0, :, sl] = (_rms(qa[:, sl], qna_ref[...]) * scale).astype(BF16)
    ka_ref[0] = _rms(proj(EV_KA, EV_VA), kna_ref[...]).astype(BF16)
    va_ref[0, :, :HEAD_DIM] = proj(EV_VA, EV_QI).astype(BF16)
    va_ref[0, :, HEAD_DIM:] = ones_col
    qi_ref[0] = proj(EV_QI, EV_QB).astype(BF16)
    vb = proj(EV_VB, EV_MISC)
    for h in range(N_HEADS):
        vb_ref[0, :, h * AUG:h * AUG + HEAD_DIM] = vb[:, h * HEAD_DIM:(h + 1) * HEAD_DIM].astype(BF16)
        vb_ref[0, :, h * AUG + HEAD_DIM:(h + 1) * AUG] = ones_col

    misc = proj(EV_MISC, EV_END)
    kd_ref[0] = jnp.where(lane < IDX_DIM, misc, pltpu.roll(misc, IDX_DIM, 1)).astype(BF16)
    wi_ref[0] = misc.T[MISC_W:MISC_W + N_IDX_HEADS, :]

    zf = misc + bf_ref[...]
    logf = jnp.minimum(zf, 0.0) - jnp.log(1.0 + jnp.exp(-jnp.abs(zf)))
    logf = jnp.where((lane >= MISC_F) & (lane < MISC_F + N_HEADS), logf, 0.0)
    csum = _dot_hi(_tri(tm).astype(F32), logf) + carry_ref[...]
    carry_ref[...] = csum[tm - 1:tm, :]

    qb = proj(EV_QB, EV_KB)
    kb = proj(EV_KB, EV_VB)
    one = jnp.ones((tm, LANES), F32)
    zero = jnp.zeros((tm, LANES), F32)
    for h in range(N_HEADS):
        sl = slice(h * HEAD_DIM, (h + 1) * HEAD_DIM)
        c = jnp.broadcast_to(csum[:, MISC_F + h:MISC_F + h + 1], (tm, LANES)) * LOG2E
        hi, mid, lo = _split3(c)
        aq = jnp.where(lane == 0, hi, jnp.where(lane == 1, mid, jnp.where(lane == 2, lo,
             jnp.where(lane < 6, one, zero))))
        ak = jnp.where(lane < 3, one, jnp.where(lane == 3, -hi, jnp.where(lane == 4, -mid,
             jnp.where(lane == 5, -lo, zero))))
        qb_ref[0, :, h * AUG:h * AUG + LANES] = (_rms(qb[:, sl], qnb_ref[...]) * scale).astype(BF16)
        qb_ref[0, :, h * AUG + LANES:(h + 1) * AUG] = aq.astype(BF16)
        kb_ref[0, :, h * AUG:h * AUG + LANES] = _rms(kb[:, sl], knb_ref[...]).astype(BF16)
        kb_ref[0, :, h * AUG + LANES:(h + 1) * AUG] = ak.astype(BF16)


def _even_proj(x, g, w, bf, qna, kna, qnb, knb):
    B, S, D = x.shape
    tm = TM_PROJ
    tok = lambda c: pl.BlockSpec((1, tm, c), lambda b, j: (b, j, 0))
    row = lambda c: pl.BlockSpec((1, c), lambda b, j: (0, 0))
    out_cols = (512, 128, AUG, 512, 128, None, N_HEADS * AUG, N_HEADS * AUG, N_HEADS * AUG)
    out_specs = [tok(c) if c else pl.BlockSpec((1, N_IDX_HEADS, tm), lambda b, j: (b, 0, j))
                 for c in out_cols]
    out_shape = [jax.ShapeDtypeStruct((B, S, c), BF16) if c else
                 jax.ShapeDtypeStruct((B, N_IDX_HEADS, S), F32) for c in out_cols]
    return pl.pallas_call(
        _even_proj_kernel,
        grid=(B, S // tm),
        in_specs=[tok(D), row(D), _resident(w.shape), row(LANES), row(LANES), row(LANES),
                  row(LANES), row(LANES)],
        out_specs=out_specs,
        out_shape=out_shape,
        scratch_shapes=[pltpu.VMEM((1, LANES), F32)],
        compiler_params=_cparams("arbitrary", "arbitrary"),
        name="even_proj",
    )(x, g, w, bf, qna, kna, qnb, knb)


def _fold_max(mx, s):
    for g in range(s.shape[1] // LANES):
        mx = jnp.maximum(mx, s[:, g * LANES:(g + 1) * LANES])
    return mx


def _flash_kernel(q_ref, k_ref, v_ref, o_ref, lg_ref, acc_ref, *, tq, tk):
    i = pl.program_id(2)
    q = q_ref[0]
    ratio = tq // tk
    nfull = i * ratio

    def logits(j):
        return _dot_nt(q, k_ref[0, pl.ds(pl.multiple_of(j * tk, tk), tk), :])

    def pass_a(j, mx):
        s = logits(j)
        lg_ref[j] = s
        return _fold_max(mx, s)

    mx = lax.fori_loop(0, nfull, pass_a, jnp.full((tq, LANES), NEG_INF, F32))
    for d in range(ratio):
        j = nfull + d
        row = i * tq + lax.broadcasted_iota(jnp.int32, (tq, tk), 0)
        col = j * tk + lax.broadcasted_iota(jnp.int32, (tq, tk), 1)
        s = jnp.where(col <= row, logits(j), NEG_INF)
        lg_ref[j] = s
        mx = _fold_max(mx, s)
    m = jnp.max(mx, axis=-1, keepdims=True)

    acc_ref[...] = jnp.zeros_like(acc_ref)

    def pass_b(j, c):
        p = jnp.exp2(lg_ref[j] - m).astype(BF16)
        acc_ref[...] += _dot(p, v_ref[0, pl.ds(pl.multiple_of(j * tk, tk), tk), :])
        return c

    lax.fori_loop(0, nfull + ratio, pass_b, 0)
    acc = acc_ref[...]
    o_ref[0] = (acc[:, :HEAD_DIM] / acc[:, HEAD_DIM:HEAD_DIM + 1]).astype(o_ref.dtype)


def _flash(q, k, v):
    B, S, _ = q.shape
    tq, tk = TQ_FLASH, TK_FLASH
    return pl.pallas_call(
        functools.partial(_flash_kernel, tq=tq, tk=tk),
        grid=(B, N_HEADS, S // tq),
        in_specs=[pl.BlockSpec((1, tq, AUG), lambda b, h, i: (b, i, h)),
                  pl.BlockSpec((1, S, AUG), lambda b, h, i: (b, 0, h)),
                  pl.BlockSpec((1, S, AUG), lambda b, h, i: (b, 0, h))],
        out_specs=pl.BlockSpec((1, tq, HEAD_DIM), lambda b, h, i: (b, i, h)),
        out_shape=jax.ShapeDtypeStruct((B, S, N_HEADS * HEAD_DIM), BF16),
        scratch_shapes=[pltpu.VMEM((S // tk, tq, tk), F32), pltpu.VMEM((tq, AUG), F32)],
        compiler_params=_cparams("arbitrary", "arbitrary", "arbitrary"),
        name="flash_attn",
    )(q, k, v)


def _dsa_kernel(qa_ref, ka_ref, va_ref, qi_ref, kd_ref, wi_ref, o_ref,
                sc_ref, qstk_ref, lg_ref, acc_ref, *, tkc, topk):
    tq = qa_ref.shape[1]
    i = pl.program_id(1)
    nkc = (i * tq + tq + tkc - 1) // tkc
    kf = float(topk)
    lane = _lane_iota((tq, LANES))
    keyi = lax.broadcasted_iota(jnp.int32, (tkc, tq), 0)
    qpos = i * tq + lax.broadcasted_iota(jnp.int32, (tkc, tq), 1)

    q8 = qi_ref[0]
    for j in range(N_IDX_HEADS):
        grp = q8[:, LANES * (j // 2):LANES * (j // 2 + 1)]
        keep = (lane < IDX_DIM) if j % 2 == 0 else (lane >= IDX_DIM)
        qstk_ref[tq * j:tq * (j + 1), :] = jnp.where(keep, grp, jnp.zeros_like(grp))
    wrows = [wi_ref[0, j:j + 1, :] for j in range(N_IDX_HEADS)]

    def chunk(ref, kc):
        return ref[0, pl.ds(pl.multiple_of(kc * tkc, tkc), tkc), :]

    def fold8(x, op):
        rows = x.shape[0]
        while rows > 8:
            rows //= 2
            x = op(x[:rows], x[rows:])
        return x

    def p1(kc, carry):
        rmax, rmin = carry
        s8 = _dot_nt(chunk(kd_ref, kc), qstk_ref[...])
        score = wrows[0] * jnp.maximum(s8[:, 0:tq], 0.0)
        for j in range(1, N_IDX_HEADS):
            score = score + wrows[j] * jnp.maximum(s8[:, tq * j:tq * (j + 1)], 0.0)
        adm = (kc * tkc + keyi) <= qpos
        sc_ref[kc] = jnp.where(adm, score, NEG_INF)
        rmax = jnp.maximum(rmax, fold8(jnp.where(adm, score, NEG_INF), jnp.maximum))
        rmin = jnp.minimum(rmin, fold8(jnp.where(adm, score, -NEG_INF), jnp.minimum))
        return rmax, rmin

    rmax, rmin = lax.fori_loop(0, nkc, p1, (jnp.full((8, tq), NEG_INF, F32),
                                            jnp.full((8, tq), -NEG_INF, F32)))
    rmax = jnp.max(rmax, axis=0, keepdims=True)
    rmin = jnp.min(rmin, axis=0, keepdims=True)

    def row_count(pred):
        def cb(kc, c):
            return c + fold8(jnp.where(pred(sc_ref[kc], kc), 1.0, 0.0), jnp.add)
        c = lax.fori_loop(0, nkc, cb, jnp.zeros((8, tq), F32))
        return jnp.sum(c, axis=0, keepdims=True)

    def any_row(flag):
        return jnp.max(flag.astype(jnp.int32))

    def select_threshold():
        hi0 = rmax + (jnp.abs(rmax) * 1.2e-7 + 1e-37)

        def bis_cond(st):
            it, _, _, _, unres = st
            return (unres > 0) & (it < BISECT_ROUND)

        def bis_body(st):
            it, lo, hi, cnt, _ = st
            for _ in range(BISECT_GROUP):
                mid = lo + (hi - lo) * 0.5
                c = row_count(lambda blk, kc, mid=mid: blk >= mid)
                up = c >= kf
                lo = jnp.where(up, mid, lo)
                hi = jnp.where(up, hi, mid)
                cnt = jnp.where(up, c, cnt)
            return it + BISECT_GROUP, lo, hi, cnt, any_row(cnt != kf)

        def resolve_ties(lo, cnt):
            def vmin(kc, v):
                blk = sc_ref[kc]
                return jnp.minimum(v, fold8(jnp.where(blk >= lo, blk, -NEG_INF), jnp.minimum))
            v = lax.fori_loop(0, nkc, vmin, jnp.full((8, tq), -NEG_INF, F32))
            v = jnp.min(v, axis=0, keepdims=True)
            cgt = row_count(lambda blk, kc: blk > v)
            tie = (cnt != kf) & (cgt < kf)
            need = kf - cgt
            nbits = max(1, int(sc_ref.shape[0] * tkc - 1).bit_length())

            def jb(_, st):
                jlo, jhi = st
                jm = lax.shift_right_arithmetic(jlo + jhi, jnp.ones_like(jlo))
                c = row_count(lambda blk, kc: (blk == v) & ((kc * tkc + keyi) <= jm))
                ok = c >= need
                return jnp.where(ok, jlo, jm), jnp.where(ok, jm, jhi)

            _, jcut = lax.fori_loop(0, nbits, jb, (jnp.full((1, tq), -1, jnp.int32),
                                                   jnp.full((1, tq), sc_ref.shape[0] * tkc - 1,
                                                            jnp.int32)))

            def drop(kc, c):
                blk = sc_ref[kc]
                cut = tie & (blk == v) & ((kc * tkc + keyi) > jcut)
                sc_ref[kc] = jnp.where(cut, NEG_INF, blk)
                return c
            lax.fori_loop(0, nkc, drop, 0)
            return jnp.where(tie, v, lo), jnp.where(tie, kf, cnt)

        def round_cond(st):
            r, _, _, _, unres = st
            return (unres > 0) & (r < BISECT_MAX_ROUNDS)

        def round_body(st):
            r, lo, hi, cnt, unres = st
            _, lo, hi, cnt, unres = lax.while_loop(bis_cond, bis_body, (0, lo, hi, cnt, unres))
            lo, cnt = lax.cond(unres > 0, resolve_ties, lambda a, b: (a, b), lo, cnt)
            return r + 1, lo, hi, cnt, any_row(cnt != kf)

        cnt0 = (qpos[0:1, :] + 1).astype(F32)
        st = lax.while_loop(round_cond, round_body, (0, rmin, hi0, cnt0, any_row(cnt0 != kf)))
        return st[1]

    thr = lax.cond((i + 1) * tq > topk, select_threshold,
                   lambda: jnp.full((1, tq), -F32_MAX, F32))

    qa = qa_ref[0]
    qs = jnp.concatenate([qa[:, h * HEAD_DIM:(h + 1) * HEAD_DIM] for h in range(N_HEADS)], axis=0)

    def p3a(kc, mx):
        s = _dot_nt(qs, chunk(ka_ref, kc))
        sel = jnp.where(sc_ref[kc] >= thr, 0.0, NEG_INF).T
        parts = []
        for h in range(N_HEADS):
            rs = slice(h * tq, (h + 1) * tq)
            sh = s[rs] + sel
            lg_ref[kc, rs, :] = sh
            parts.append(_fold_max(mx[rs], sh))
        return jnp.concatenate(parts, axis=0)

    mx = lax.fori_loop(0, nkc, p3a, jnp.full((N_HEADS * tq, LANES), NEG_INF, F32))
    m = jnp.max(mx, axis=-1, keepdims=True)

    acc_ref[...] = jnp.zeros_like(acc_ref)

    def p3b(kc, c):
        p = jnp.exp2(lg_ref[kc] - m).astype(BF16)
        acc_ref[...] += _dot(p, chunk(va_ref, kc))
        return c

    lax.fori_loop(0, nkc, p3b, 0)
    for h in range(N_HEADS):
        acc = acc_ref[h * tq:(h + 1) * tq, :]
        o_ref[0, :, h * HEAD_DIM:(h + 1) * HEAD_DIM] = (
            acc[:, :HEAD_DIM] / acc[:, HEAD_DIM:HEAD_DIM + 1]).astype(o_ref.dtype)


def _dsa(qa, ka, va, qi, kd, wi):
    B, S, _ = qa.shape
    tq, tkc = TQ_DSA, min(TK_DSA, S)
    topk = min(TOPK_MAX, S // 4)
    qblk = lambda c: pl.BlockSpec((1, tq, c), lambda b, i: (b, i, 0))
    full = lambda c: pl.BlockSpec((1, S, c), lambda b, i: (b, 0, 0))
    return pl.pallas_call(
        functools.partial(_dsa_kernel, tkc=tkc, topk=topk),
        grid=(B, S // tq),
        in_specs=[qblk(512), full(128), full(AUG), qblk(512), full(128),
                  pl.BlockSpec((1, N_IDX_HEADS, tq), lambda b, i: (b, 0, i))],
        out_specs=qblk(512),
        out_shape=jax.ShapeDtypeStruct((B, S, N_HEADS * HEAD_DIM), BF16),
        scratch_shapes=[pltpu.VMEM((S // tkc, tkc, tq), F32),
                        pltpu.VMEM((N_IDX_HEADS * tq, LANES), BF16),
                        pltpu.VMEM((S // tkc, N_HEADS * tq, tkc), F32),
                        pltpu.VMEM((N_HEADS * tq, AUG), F32)],
        compiler_params=_cparams("arbitrary", "arbitrary"),
        name="dsa_attn",
    )(qa, ka, va, qi, kd, wi)


def _rope_table_kernel(pos_ref, inv_ref, cos_ref, sin_ref):
    tm = pos_ref.shape[1]
    lane = _lane_iota((tm, LANES))
    ang = pos_ref[0].astype(F32) * inv_ref[...]
    half = QK_ROPE // 2
    cos_ref[0] = jnp.where(lane < QK_ROPE, jnp.cos(ang), 0.0)
    sn = jnp.sin(ang)
    sin_ref[0] = jnp.where(lane < half, -sn, jnp.where(lane < QK_ROPE, sn, 0.0))


def _rope_tables(positions):
    B, S = positions.shape
    tm = TM_PROJ
    half = QK_ROPE // 2
    inv = ROPE_THETA ** (-jnp.arange(half, dtype=F32) / half)
    inv = jnp.concatenate([inv, inv, jnp.zeros((LANES - QK_ROPE,), F32)])[None, :]
    return pl.pallas_call(
        _rope_table_kernel,
        grid=(B, S // tm),
        in_specs=[pl.BlockSpec((1, tm, 1), lambda b, j: (b, j, 0)),
                  pl.BlockSpec((1, LANES), lambda b, j: (0, 0))],
        out_specs=[pl.BlockSpec((1, tm, LANES), lambda b, j: (b, j, 0))] * 2,
        out_shape=[jax.ShapeDtypeStruct((B, S, LANES), F32)] * 2,
        compiler_params=_cparams("arbitrary", "arbitrary"),
        name="rope_tables",
    )(positions[:, :, None], inv)


def _rope(t, cos, sin):
    lane = _lane_iota(t.shape)
    half = QK_ROPE // 2
    swap = jnp.where(lane < half, pltpu.roll(t, LANES - half, 1), pltpu.roll(t, half, 1))
    return t * cos + swap * sin


def _odd_proj_kernel(x_ref, g_ref, w_ref, cqn_ref, ckvn_ref, wuq_ref, wukv_ref, qnc_ref, knc_ref,
                     cos_ref, sin_ref, q_ref, k_ref, v_ref, z_ref, xbc_ref, dt_ref):
    xn = _rms(x_ref[0], g_ref[...]).astype(BF16)

    def proj(lo, hi):
        return _dot(xn, w_ref[:, lo:hi])

    cos = cos_ref[0]
    sin = sin_ref[0]
    n_qk = QK_NOPE + QK_ROPE
    scale = n_qk ** -0.5 * LOG2E
    ones_col = jnp.where(_lane_iota(cos.shape) == 0, 1.0, 0.0).astype(BF16)

    cq = _rms(proj(OD_CQ, OD_CKV), cqn_ref[...]).astype(BF16)
    q = _dot(cq, wuq_ref[...])
    ckv = _rms(proj(OD_CKV, OD_KR), ckvn_ref[...]).astype(BF16)
    kv = _dot(ckv, wukv_ref[...])
    kr = proj(OD_KR, OD_DT)
    kr_ss = jnp.sum(kr * kr, axis=-1, keepdims=True)
    kr_rot = _rope(kr * knc_ref[:, LANES:], cos, sin)
    for h in range(N_HEADS):
        qh = q[:, h * AUG:(h + 1) * AUG]
        qh = _rms(qh, qnc_ref[...], n_qk) * scale
        q_ref[0, :, h * AUG:h * AUG + LANES] = qh[:, :LANES].astype(BF16)
        q_ref[0, :, h * AUG + LANES:(h + 1) * AUG] = _rope(qh[:, LANES:], cos, sin).astype(BF16)
        kn = kv[:, h * AUG:h * AUG + LANES]
        r = lax.rsqrt((jnp.sum(kn * kn, axis=-1, keepdims=True) + kr_ss) * (1.0 / n_qk) + EPS)
        k_ref[0, :, h * AUG:h * AUG + LANES] = (kn * r * knc_ref[:, :LANES]).astype(BF16)
        k_ref[0, :, h * AUG + LANES:(h + 1) * AUG] = (kr_rot * r).astype(BF16)
        v_ref[0, :, h * AUG:h * AUG + HEAD_DIM] = kv[:, h * AUG + LANES:(h + 1) * AUG].astype(BF16)
        v_ref[0, :, h * AUG + HEAD_DIM:(h + 1) * AUG] = ones_col

    dt_ref[0] = proj(OD_DT, OD_Z)
    z_ref[0] = proj(OD_Z, OD_XBC)
    xbc_ref[0] = proj(OD_XBC, OD_END)


def _odd_proj(x, g, w, cqn, ckvn, wuq, wukv, qnc, knc, cos, sin):
    B, S, D = x.shape
    tm = TM_PROJ
    tok = lambda c: pl.BlockSpec((1, tm, c), lambda b, j: (b, j, 0))
    row = lambda c: pl.BlockSpec((1, c), lambda b, j: (0, 0))
    out_cols = (N_HEADS * AUG, N_HEADS * AUG, N_HEADS * AUG, SSM_INNER, CONV_DIM, LANES)
    out_dt = (BF16, BF16, BF16, F32, F32, F32)
    return pl.pallas_call(
        _odd_proj_kernel,
        grid=(B, S // tm),
        in_specs=[tok(D), row(D), _resident(w.shape), row(Q_LORA), row(KV_LORA),
                  _resident(wuq.shape), _resident(wukv.shape), row(AUG), row(AUG),
                  tok(LANES), tok(LANES)],
        out_specs=[tok(c) for c in out_cols],
        out_shape=[jax.ShapeDtypeStruct((B, S, c), dt) for c, dt in zip(out_cols, out_dt)],
        compiler_params=_cparams("arbitrary", "arbitrary"),
        name="odd_proj",
    )(x, g, w, cqn, ckvn, wuq, wukv, qnc, knc, cos, sin)


def _ssd_kernel(xbc_ref, z_ref, dt_ref, cw_ref, cb_ref, dtb_ref, alog_ref, dskip_ref, gn_ref,
                y_ref, xe_ref, st_ref):
    q = xbc_ref.shape[1]
    halo = 8

    @pl.when(pl.program_id(1) == 0)
    def _():
        xe_ref[0:halo, :] = jnp.zeros((halo, CONV_DIM), F32)
        st_ref[...] = jnp.zeros_like(st_ref)

    xe_ref[halo:halo + q, :] = xbc_ref[0]
    conv = cb_ref[...]
    for w in range(CONV_WIDTH):
        off = halo - (CONV_WIDTH - 1) + w
        conv = conv + xe_ref[off:off + q, :] * cw_ref[w:w + 1, :]
    xe_ref[0:halo, :] = xe_ref[q:q + halo, :]
    act = conv * _sigmoid(conv)
    xs = act[:, :SSM_INNER]
    gs = SSM_STATE
    bm = [act[:, SSM_INNER + g * gs:SSM_INNER + (g + 1) * gs] for g in range(SSM_GROUPS)]
    cm = [act[:, SSM_INNER + (SSM_GROUPS + g) * gs:SSM_INNER + (SSM_GROUPS + g + 1) * gs]
          for g in range(SSM_GROUPS)]

    lane = _lane_iota((q, LANES))
    dt = _softplus(dt_ref[0] + dtb_ref[...])
    a = jnp.where(lane < SSM_HEADS, dt * (-jnp.exp(alog_ref[...])), 0.0)
    tri = _tri(q)
    trif = tri.astype(F32)
    acum = _dot_hi(trif, a)
    acum_t = lax.dot_general(a.T, trif, (((1,), (1,)), ((), ())), preferred_element_type=F32,
                             precision=lax.Precision.HIGHEST)
    last = acum[q - 1:q, :]
    dte = jnp.exp(last - acum)
    eac = jnp.exp(acum)
    cdec = jnp.exp(last)

    def expand(mat):
        rows = mat.shape[0]
        ln = _lane_iota((rows, LANES))
        parts = []
        for p in range(SSM_HEADS // 2):
            lo = jnp.broadcast_to(mat[:, 2 * p:2 * p + 1], (rows, LANES))
            hi = jnp.broadcast_to(mat[:, 2 * p + 1:2 * p + 2], (rows, LANES))
            parts.append(jnp.where(ln < SSM_HEAD_DIM, lo, hi))
        return jnp.concatenate(parts, axis=1)

    xd = xs * expand(dt)
    eac_x = expand(eac)
    xdd = (xd * expand(dte)).astype(BF16)
    xd16 = xd.astype(BF16)
    zero16 = jnp.zeros((q, LANES), BF16)

    rpg = SSM_HEADS // SSM_GROUPS
    gw = rpg * SSM_HEAD_DIM
    y_parts = []
    new_states = []
    for g in range(SSM_GROUPS):
        b16 = bm[g].astype(BF16)
        c16 = cm[g].astype(BF16)
        cb = _dot_nt(c16, b16)
        st_old = st_ref[:, g * gw:(g + 1) * gw]
        y_off = _dot(c16, st_old.astype(BF16)) * eac_x[:, g * gw:(g + 1) * gw]
        y_diag = []
        for pp in range(rpg // 2):
            p = g * (rpg // 2) + pp
            xp = xd16[:, p * LANES:(p + 1) * LANES]
            acc = None
            for e in range(2):
                h = 2 * p + e
                diff = acum[:, h:h + 1] - acum_t[h:h + 1, :]
                lm = jnp.exp(jnp.where(tri, diff, NEG_INF))
                gmat = (cb * lm).astype(BF16)
                keep = (lane < SSM_HEAD_DIM) if e == 0 else (lane >= SSM_HEAD_DIM)
                part = _dot(gmat, jnp.where(keep, xp, zero16))
                acc = part if acc is None else acc + part
            y_diag.append(acc)
        y_parts.append(jnp.concatenate(y_diag, axis=1) + y_off)
        st_new = _dot(bm[g].T.astype(BF16), xdd[:, g * gw:(g + 1) * gw])
        new_states.append(st_new)
    cdec_x = expand(cdec)
    for g in range(SSM_GROUPS):
        sl = slice(g * gw, (g + 1) * gw)
        st_ref[:, sl] = st_ref[:, sl] * cdec_x[:, sl] + new_states[g]

    y = jnp.concatenate(y_parts, axis=1) + dskip_ref[...] * xs
    zz = z_ref[0]
    y = y * (zz * _sigmoid(zz))
    for g in range(SSM_GROUPS):
        sl = slice(g * gw, (g + 1) * gw)
        y_ref[0, :, sl] = _rms(y[:, sl], gn_ref[:, sl]).astype(y_ref.dtype)


def _ssd(xbc, z, dt, cw, cb, dtb, alog, dskip, gn):
    B, S, _ = xbc.shape
    q = SSD_CHUNK
    tok = lambda c: pl.BlockSpec((1, q, c), lambda b, j: (b, j, 0))
    row = lambda r, c: pl.BlockSpec((r, c), lambda b, j: (0, 0))
    return pl.pallas_call(
        _ssd_kernel,
        grid=(B, S // q),
        in_specs=[tok(CONV_DIM), tok(SSM_INNER), tok(LANES), row(CONV_WIDTH, CONV_DIM),
                  row(1, CONV_DIM), row(1, LANES), row(1, LANES), row(1, SSM_INNER),
                  row(1, SSM_INNER)],
        out_specs=tok(SSM_INNER),
        out_shape=jax.ShapeDtypeStruct((B, S, SSM_INNER), BF16),
        scratch_shapes=[pltpu.VMEM((q + 8, CONV_DIM), F32), pltpu.VMEM((SSM_STATE, SSM_INNER), F32)],
        compiler_params=_cparams("arbitrary", "arbitrary"),
        name="ssd_scan",
    )(xbc, z, dt, cw, cb, dtb, alog, dskip, gn)


def _out_mlp_kernel(x_ref, o1_ref, o2_ref, wo1_ref, wo2_ref, g_ref, w1_ref, w2_ref, y_ref,
                    xn_ref, acc_ref, *, tf):
    xnew = x_ref[...] + _dot(o1_ref[...], wo1_ref[...]) + _dot(o2_ref[...], wo2_ref[...])
    xn_ref[...] = _rms(xnew, g_ref[...]).astype(BF16)
    acc_ref[...] = xnew

    def body(c, carry):
        cs = pl.multiple_of(c * tf, tf)
        h = _dot(xn_ref[...], w1_ref[:, pl.ds(cs, tf)])
        a = jnp.square(jnp.maximum(h, 0.0)).astype(BF16)
        acc_ref[...] += _dot(a, w2_ref[pl.ds(cs, tf), :])
        return carry

    lax.fori_loop(0, w1_ref.shape[1] // tf, body, 0)
    y_ref[...] = acc_ref[...]


def _out_mlp(x, o1, o2, wo1, wo2, g, w1, w2):
    T, D = x.shape
    tm = TM_MLP
    tok = lambda c: pl.BlockSpec((tm, c), lambda i: (i, 0))
    return pl.pallas_call(
        functools.partial(_out_mlp_kernel, tf=TF_MLP),
        grid=(T // tm,),
        in_specs=[tok(D), tok(o1.shape[1]), tok(o2.shape[1]), _resident(wo1.shape),
                  _resident(wo2.shape), pl.BlockSpec((1, D), lambda i: (0, 0)),
                  _resident(w1.shape), _resident(w2.shape)],
        out_specs=tok(D),
        out_shape=jax.ShapeDtypeStruct((T, D), F32),
        scratch_shapes=[pltpu.VMEM((tm, D), BF16), pltpu.VMEM((tm, D), F32)],
        compiler_params=_cparams("arbitrary"),
        name="out_mlp",
    )(x, o1, o2, wo1, wo2, g, w1, w2)


def _pack_even_w(w):
    qa, ka, va, qi, ki, wi, qb, kb, vb, fb = jnp.split(
        w, [512, 640, 768, 1280, 1344, 1352, 1864, 2376, 2888], axis=1)
    pad = jnp.zeros((w.shape[0], LANES - IDX_DIM - N_IDX_HEADS - N_HEADS), w.dtype)
    return jnp.concatenate([qa, ka, va, qi, qb, kb, vb, ki, wi, fb, pad], axis=1).astype(BF16)


def _pack_odd_w(w):
    cq, ckv, kr, z, xbc, dt = jnp.split(w, [384, 640, 704, 1728, 3776], axis=1)
    zpad = lambda n: jnp.zeros((w.shape[0], n), w.dtype)
    return jnp.concatenate([cq, ckv, kr, zpad(LANES - QK_ROPE), dt, zpad(LANES - SSM_HEADS), z, xbc],
                           axis=1).astype(BF16)


def _pack_wuq(w):
    n_qk = QK_NOPE + QK_ROPE
    w = w.reshape(w.shape[0], N_HEADS, n_qk)
    w = jnp.pad(w, ((0, 0), (0, 0), (0, AUG - n_qk)))
    return w.reshape(w.shape[0], N_HEADS * AUG).astype(BF16)


def _row(v, width=None):
    v = v.astype(F32)[None, :]
    if width is not None and width > v.shape[1]:
        v = jnp.pad(v, ((0, 0), (0, width - v.shape[1])))
    return v


def kernel(x, positions, ev_norm, ev_w_in, ev_b_f, ev_qn_a, ev_kn_a, ev_qn_b, ev_kn_b, ev_w_out,
           od_norm, od_w_in, od_cq_norm, od_ckv_norm, od_w_uq, od_w_ukv, od_qn_c, od_kn_c,
           od_conv_w, od_conv_b, od_dt_bias, od_a_log, od_d_skip, od_gate_norm, od_w_out,
           mlp_norm, mlp_w1, mlp_w2):
    B, S, D = x.shape
    depth = mlp_w1.shape[0]
    cos = sin = None
    for layer in range(depth):
        i = layer // 2
        if layer % 2 == 0:
            bf = jnp.zeros((1, LANES), F32).at[0, MISC_F:MISC_F + N_HEADS].set(ev_b_f[i])
            qa, ka, va, qi, kd, wi, qb, kb, vb = _even_proj(
                x, _row(ev_norm[i]), _pack_even_w(ev_w_in[i]), bf, _row(ev_qn_a[i]),
                _row(ev_kn_a[i]), _row(ev_qn_b[i]), _row(ev_kn_b[i]))
            o1 = _dsa(qa, ka, va, qi, kd, wi)
            o2 = _flash(qb, kb, vb)
            wo = ev_w_out[i].astype(BF16)
        else:
            if cos is None:
                cos, sin = _rope_tables(positions)
            q, k, v, z, xbc, dt = _odd_proj(
                x, _row(od_norm[i]), _pack_odd_w(od_w_in[i]), _row(od_cq_norm[i]),
                _row(od_ckv_norm[i]), _pack_wuq(od_w_uq[i]), od_w_ukv[i].astype(BF16),
                _row(od_qn_c[i], AUG), _row(od_kn_c[i], AUG), cos, sin)
            o1 = _flash(q, k, v)
            o2 = _ssd(xbc, z, dt, od_conv_w[i].astype(F32), _row(od_conv_b[i]),
                      _row(od_dt_bias[i], LANES), _row(od_a_log[i], LANES),
                      _row(jnp.repeat(od_d_skip[i], SSM_HEAD_DIM)), _row(od_gate_norm[i]))
            wo = od_w_out[i].astype(BF16)
        k1 = o1.shape[-1]
        x = _out_mlp(x.reshape(B * S, D), o1.reshape(B * S, k1), o2.reshape(B * S, -1),
                     wo[:k1], wo[k1:], _row(mlp_norm[layer]), mlp_w1[layer].astype(BF16),
                     mlp_w2[layer].astype(BF16)).reshape(B, S, D)
    return x
```

```python
import functools

import jax
import jax.numpy as jnp
from jax import lax
from jax.experimental import pallas as pl
from jax.experimental.pallas import tpu as pltpu

F32 = jnp.float32
BF16 = jnp.bfloat16
EPS = 1e-6
NEG_INF = float("-inf")
F32_MAX = 3.4028234663852886e38
LOG2E = 1.4426950408889634

D_MODEL = 1024
HEAD_DIM = 128
N_HEADS = 4
N_IDX_HEADS = 8
IDX_DIM = 64
TOPK_MAX = 256
QK_NOPE = 128
QK_ROPE = 64
Q_LORA = 384
KV_LORA = 256
ROPE_THETA = 10000.0
SSM_INNER = 1024
SSM_HEAD_DIM = 64
SSM_HEADS = 16
SSM_GROUPS = 4
SSM_STATE = 128
CONV_WIDTH = 4
CONV_DIM = SSM_INNER + 2 * SSM_GROUPS * SSM_STATE
D_FF = 4 * D_MODEL

LANES = 128
AUG = 2 * LANES
VMEM_LIMIT = 56 * 1024 * 1024

TM_PROJ = 512
TM_MLP = 512
TF_MLP = 512
TQ_FLASH = 1024
TK_FLASH = 1024
TQ_DSA = 256
TK_DSA = 512
SSD_CHUNK = 128
BISECT_GROUP = 4
BISECT_ROUND = 28
BISECT_MAX_ROUNDS = 8

EV_QA, EV_KA, EV_VA, EV_QI, EV_QB, EV_KB, EV_VB, EV_MISC, EV_END = (
    0, 512, 640, 768, 1280, 1792, 2304, 2816, 2944)
MISC_W = 64
MISC_F = 72
OD_CQ, OD_CKV, OD_KR, OD_DT, OD_Z, OD_XBC, OD_END = (0, 384, 640, 768, 896, 1920, 3968)


def _cparams(*sem):
    return pltpu.CompilerParams(dimension_semantics=sem, vmem_limit_bytes=VMEM_LIMIT)


def _resident(shape):
    nd = len(shape)
    return pl.BlockSpec(shape, lambda *_: (0,) * nd, pipeline_mode=pl.Buffered(1))


def _rms(x, g, n=None):
    n = x.shape[-1] if n is None else n
    ss = jnp.sum(x * x, axis=-1, keepdims=True)
    return x * lax.rsqrt(ss * (1.0 / n) + EPS) * g


def _dot(a, b):
    return jnp.dot(a, b, preferred_element_type=F32)


def _dot_nt(a, b):
    return lax.dot_general(a, b, (((1,), (1,)), ((), ())), preferred_element_type=F32)


def _dot_hi(a, b):
    return jnp.dot(a, b, preferred_element_type=F32, precision=lax.Precision.HIGHEST)


def _sigmoid(x):
    return 1.0 / (1.0 + jnp.exp(-x))


def _softplus(x):
    return jnp.maximum(x, 0.0) + jnp.log(1.0 + jnp.exp(-jnp.abs(x)))


def _lane_iota(shape):
    return lax.broadcasted_iota(jnp.int32, shape, len(shape) - 1)


def _tri(n):
    r = lax.broadcasted_iota(jnp.int32, (n, n), 0)
    c = lax.broadcasted_iota(jnp.int32, (n, n), 1)
    return r >= c


def _split3(c):
    hi = c.astype(BF16).astype(F32)
    mid = (c - hi).astype(BF16).astype(F32)
    lo = (c - hi - mid).astype(BF16).astype(F32)
    return hi, mid, lo


def _even_proj_kernel(x_ref, g_ref, w_ref, bf_ref, qna_ref, kna_ref, qnb_ref, knb_ref,
                      qa_ref, ka_ref, va_ref, qi_ref, kd_ref, wi_ref, qb_ref, kb_ref, vb_ref,
                      carry_ref):
    tm = x_ref.shape[1]

    @pl.when(pl.program_id(1) == 0)
    def _():
        carry_ref[...] = jnp.zeros_like(carry_ref)

    xn = _rms(x_ref[0], g_ref[...]).astype(BF16)

    def proj(lo, hi):
        return _dot(xn, w_ref[:, lo:hi])

    scale = HEAD_DIM ** -0.5 * LOG2E
    lane = _lane_iota((tm, LANES))
    ones_col = jnp.where(lane == 0, 1.0, 0.0).astype(BF16)
    qa = proj(EV_QA, EV_KA)
    for h in range(N_HEADS):
        sl = slice(h * HEAD_DIM, (h + 1) * HEAD_DIM)
        qa_ref[0, :, sl] = (_rms(qa[:, sl], qna_ref[...]) * scale).astype(BF16)
    ka_ref[0] = _rms(proj(EV_KA, EV_VA), kna_ref[...]).astype(BF16)
    va_ref[0, :, :HEAD_DIM] = proj(EV_VA, EV_QI).astype(BF16)
    va_ref[0, :, HEAD_DIM:] = ones_col
    qi_ref[0] = proj(EV_QI, EV_QB).astype(BF16)
    vb = proj(EV_VB, EV_MISC)
    for h in range(N_HEADS):
        vb_ref[0, :, h * AUG:h * AUG + HEAD_DIM] = vb[:, h * HEAD_DIM:(h + 1) * HEAD_DIM].astype(BF16)
        vb_ref[0, :, h * AUG + HEAD_DIM:(h + 1) * AUG] = ones_col

    misc = proj(EV_MISC, EV_END)
    kd_ref[0] = jnp.where(lane < IDX_DIM, misc, pltpu.roll(misc, IDX_DIM, 1)).astype(BF16)
    wi_ref[0] = misc.T[MISC_W:MISC_W + N_IDX_HEADS, :]

    zf = misc + bf_ref[...]
    logf = jnp.minimum(zf, 0.0) - jnp.log(1.0 + jnp.exp(-jnp.abs(zf)))
    logf = jnp.where((lane >= MISC_F) & (lane < MISC_F + N_HEADS), logf, 0.0)
    tri = _tri(tm).astype(BF16)
    csum = sum(_dot(tri, part.astype(BF16)) for part in _split3(logf)) + carry_ref[...]
    carry_ref[...] = csum[tm - 1:tm, :]

    qb = proj(EV_QB, EV_KB)
    kb = proj(EV_KB, EV_VB)
    one = jnp.ones((tm, LANES), F32)
    zero = jnp.zeros((tm, LANES), F32)
    for h in range(N_HEADS):
        sl = slice(h * HEAD_DIM, (h + 1) * HEAD_DIM)
        c = jnp.broadcast_to(csum[:, MISC_F + h:MISC_F + h + 1], (tm, LANES)) * LOG2E
        hi, mid, lo = _split3(c)
        aq = jnp.where(lane == 0, hi, jnp.where(lane == 1, mid, jnp.where(lane == 2, lo,
             jnp.where(lane < 6, one, zero))))
        ak = jnp.where(lane < 3, one, jnp.where(lane == 3, -hi, jnp.where(lane == 4, -mid,
             jnp.where(lane == 5, -lo, zero))))
        qb_ref[0, :, h * AUG:h * AUG + LANES] = (_rms(qb[:, sl], qnb_ref[...]) * scale).astype(BF16)
        qb_ref[0, :, h * AUG + LANES:(h + 1) * AUG] = aq.astype(BF16)
        kb_ref[0, :, h * AUG:h * AUG + LANES] = _rms(kb[:, sl], knb_ref[...]).astype(BF16)
        kb_ref[0, :, h * AUG + LANES:(h + 1) * AUG] = ak.astype(BF16)


def _even_proj(x, g, w, bf, qna, kna, qnb, knb):
    B, S, D = x.shape
    tm = TM_PROJ
    tok = lambda c: pl.BlockSpec((1, tm, c), lambda b, j: (b, j, 0))
    row = lambda c: pl.BlockSpec((1, c), lambda b, j: (0, 0))
    out_cols = (512, 128, AUG, 512, 128, None, N_HEADS * AUG, N_HEADS * AUG, N_HEADS * AUG)
    out_specs = [tok(c) if c else pl.BlockSpec((1, N_IDX_HEADS, tm), lambda b, j: (b, 0, j))
                 for c in out_cols]
    out_shape = [jax.ShapeDtypeStruct((B, S, c), BF16) if c else
                 jax.ShapeDtypeStruct((B, N_IDX_HEADS, S), F32) for c in out_cols]
    return pl.pallas_call(
        _even_proj_kernel,
        grid=(B, S // tm),
        in_specs=[tok(D), row(D), _resident(w.shape), row(LANES), row(LANES), row(LANES),
                  row(LANES), row(LANES)],
        out_specs=out_specs,
        out_shape=out_shape,
        scratch_shapes=[pltpu.VMEM((1, LANES), F32)],
        compiler_params=_cparams("arbitrary", "arbitrary"),
        name="even_proj",
    )(x, g, w, bf, qna, kna, qnb, knb)


def _fold_max(mx, s):
    for g in range(s.shape[1] // LANES):
        mx = jnp.maximum(mx, s[:, g * LANES:(g + 1) * LANES])
    return mx


def _flash_kernel(q_ref, k_ref, v_ref, o_ref, lg_ref, acc_ref, *, tq, tk):
    i = pl.program_id(2)
    q = q_ref[0]
    ratio = tq // tk
    nfull = i * ratio

    def kchunk(ref, j):
        return ref[0, pl.ds(pl.multiple_of(j * tk, tk), tk), :]

    def pass_a(j, mx):
        s = _dot_nt(q, kchunk(k_ref, j))
        lg_ref[j] = s
        return _fold_max(mx, s)

    mx = lax.fori_loop(0, nfull, pass_a, jnp.full((tq, LANES), NEG_INF, F32))
    for d in range(ratio):
        j = nfull + d
        row = i * tq + lax.broadcasted_iota(jnp.int32, (tq, tk), 0)
        col = j * tk + lax.broadcasted_iota(jnp.int32, (tq, tk), 1)
        s = jnp.where(col <= row, _dot_nt(q, kchunk(k_ref, j)), NEG_INF)
        lg_ref[j] = s
        mx = _fold_max(mx, s)
    m = jnp.max(mx, axis=-1, keepdims=True)

    acc_ref[...] = jnp.zeros_like(acc_ref)

    def pass_b(j, c):
        p = jnp.exp2(lg_ref[j] - m).astype(BF16)
        acc_ref[...] += _dot(p, kchunk(v_ref, j))
        return c

    lax.fori_loop(0, nfull + ratio, pass_b, 0)
    acc = acc_ref[...]
    o_ref[0] = (acc[:, :HEAD_DIM] / acc[:, HEAD_DIM:HEAD_DIM + 1]).astype(o_ref.dtype)


def _flash(q, k, v):
    B, S, _ = q.shape
    tq, tk = min(TQ_FLASH, S), min(TK_FLASH, S)
    return pl.pallas_call(
        functools.partial(_flash_kernel, tq=tq, tk=tk),
        grid=(B, N_HEADS, S // tq),
        in_specs=[pl.BlockSpec((1, tq, AUG), lambda b, h, i: (b, i, h)),
                  pl.BlockSpec((1, S, AUG), lambda b, h, i: (b, 0, h)),
                  pl.BlockSpec((1, S, AUG), lambda b, h, i: (b, 0, h))],
        out_specs=pl.BlockSpec((1, tq, HEAD_DIM), lambda b, h, i: (b, i, h)),
        out_shape=jax.ShapeDtypeStruct((B, S, N_HEADS * HEAD_DIM), BF16),
        scratch_shapes=[pltpu.VMEM((S // tk, tq, tk), F32), pltpu.VMEM((tq, AUG), F32)],
        compiler_params=_cparams("arbitrary", "arbitrary", "arbitrary"),
        name="flash_attn",
    )(q, k, v)


def _dsa_kernel(qa_ref, ka_ref, va_ref, qi_ref, kd_ref, wi_ref, o_ref,
                sc_ref, qstk_ref, lg_ref, acc_ref, *, tkc, topk):
    tq = qa_ref.shape[1]
    i = pl.program_id(1)
    nkc = (i * tq + tq + tkc - 1) // tkc
    kf = float(topk)
    lane = _lane_iota((tq, LANES))
    keyi = lax.broadcasted_iota(jnp.int32, (tkc, tq), 0)
    qpos = i * tq + lax.broadcasted_iota(jnp.int32, (tkc, tq), 1)

    q8 = qi_ref[0]
    for j in range(N_IDX_HEADS):
        grp = q8[:, LANES * (j // 2):LANES * (j // 2 + 1)]
        keep = (lane < IDX_DIM) if j % 2 == 0 else (lane >= IDX_DIM)
        qstk_ref[tq * j:tq * (j + 1), :] = jnp.where(keep, grp, jnp.zeros_like(grp))
    wrows = [wi_ref[0, j:j + 1, :] for j in range(N_IDX_HEADS)]

    def chunk(ref, kc):
        return ref[0, pl.ds(pl.multiple_of(kc * tkc, tkc), tkc), :]

    def fold8(x, op):
        rows = x.shape[0]
        while rows > 8:
            rows //= 2
            x = op(x[:rows], x[rows:])
        return x

    def p1(kc, carry):
        rmax, rmin = carry
        s8 = _dot_nt(chunk(kd_ref, kc), qstk_ref[...])
        score = wrows[0] * jnp.maximum(s8[:, 0:tq], 0.0)
        for j in range(1, N_IDX_HEADS):
            score = score + wrows[j] * jnp.maximum(s8[:, tq * j:tq * (j + 1)], 0.0)
        adm = (kc * tkc + keyi) <= qpos
        sc_ref[kc] = jnp.where(adm, score, NEG_INF)
        rmax = jnp.maximum(rmax, fold8(jnp.where(adm, score, NEG_INF), jnp.maximum))
        rmin = jnp.minimum(rmin, fold8(jnp.where(adm, score, -NEG_INF), jnp.minimum))
        return rmax, rmin

    rmax, rmin = lax.fori_loop(0, nkc, p1, (jnp.full((8, tq), NEG_INF, F32),
                                            jnp.full((8, tq), -NEG_INF, F32)))
    rmax = jnp.max(rmax, axis=0, keepdims=True)
    rmin = jnp.min(rmin, axis=0, keepdims=True)

    def row_count(pred):
        def cb(kc, c):
            return c + fold8(jnp.where(pred(sc_ref[kc], kc), 1.0, 0.0), jnp.add)
        c = lax.fori_loop(0, nkc, cb, jnp.zeros((8, tq), F32))
        return jnp.sum(c, axis=0, keepdims=True)

    def any_row(flag):
        return jnp.max(flag.astype(jnp.int32))

    def select_threshold():
        hi0 = rmax + (jnp.abs(rmax) * 1.2e-7 + 1e-37)

        def bis_cond(st):
            return (st[-1] > 0) & (st[0] < BISECT_ROUND)

        def bis_body(st):
            it, lo, hi, cnt, _ = st
            for _ in range(BISECT_GROUP):
                mid = lo + (hi - lo) * 0.5
                c = row_count(lambda blk, kc, mid=mid: blk >= mid)
                up = c >= kf
                lo = jnp.where(up, mid, lo)
                hi = jnp.where(up, hi, mid)
                cnt = jnp.where(up, c, cnt)
            return it + BISECT_GROUP, lo, hi, cnt, any_row(cnt != kf)

        def resolve_ties(lo, cnt):
            def vmin(kc, v):
                blk = sc_ref[kc]
                return jnp.minimum(v, fold8(jnp.where(blk >= lo, blk, -NEG_INF), jnp.minimum))
            v = lax.fori_loop(0, nkc, vmin, jnp.full((8, tq), -NEG_INF, F32))
            v = jnp.min(v, axis=0, keepdims=True)
            cgt = row_count(lambda blk, kc: blk > v)
            tie = (cnt != kf) & (cgt < kf)
            need = kf - cgt
            nbits = max(1, int(sc_ref.shape[0] * tkc - 1).bit_length())

            def jb(_, st):
                jlo, jhi = st
                jm = lax.shift_right_arithmetic(jlo + jhi, jnp.ones_like(jlo))
                c = row_count(lambda blk, kc: (blk == v) & ((kc * tkc + keyi) <= jm))
                ok = c >= need
                return jnp.where(ok, jlo, jm), jnp.where(ok, jm, jhi)

            _, jcut = lax.fori_loop(0, nbits, jb, (jnp.full((1, tq), -1, jnp.int32),
                                                   jnp.full((1, tq), sc_ref.shape[0] * tkc - 1,
                                                            jnp.int32)))

            def drop(kc, c):
                blk = sc_ref[kc]
                cut = tie & (blk == v) & ((kc * tkc + keyi) > jcut)
                sc_ref[kc] = jnp.where(cut, NEG_INF, blk)
                return c
            lax.fori_loop(0, nkc, drop, 0)
            return jnp.where(tie, v, lo), jnp.where(tie, kf, cnt)

        def round_cond(st):
            return (st[-1] > 0) & (st[0] < BISECT_MAX_ROUNDS)

        def round_body(st):
            r, lo, hi, cnt, unres = st
            _, lo, hi, cnt, unres = lax.while_loop(bis_cond, bis_body, (0, lo, hi, cnt, unres))
            lo, cnt = lax.cond(unres > 0, resolve_ties, lambda a, b: (a, b), lo, cnt)
            return r + 1, lo, hi, cnt, any_row(cnt != kf)

        cnt0 = (qpos[0:1, :] + 1).astype(F32)
        st = lax.while_loop(round_cond, round_body, (0, rmin, hi0, cnt0, any_row(cnt0 != kf)))
        return st[1]

    thr = lax.cond((i + 1) * tq > topk, select_threshold,
                   lambda: jnp.full((1, tq), -F32_MAX, F32))

    qa = qa_ref[0]
    qs = jnp.concatenate([qa[:, h * HEAD_DIM:(h + 1) * HEAD_DIM] for h in range(N_HEADS)], axis=0)

    def p3a(kc, mx):
        s = _dot_nt(qs, chunk(ka_ref, kc))
        sel = jnp.where(sc_ref[kc] >= thr, 0.0, NEG_INF).T
        parts = []
        for h in range(N_HEADS):
            rs = slice(h * tq, (h + 1) * tq)
            sh = s[rs] + sel
            lg_ref[kc, rs, :] = sh
            parts.append(_fold_max(mx[rs], sh))
        return jnp.concatenate(parts, axis=0)

    mx = lax.fori_loop(0, nkc, p3a, jnp.full((N_HEADS * tq, LANES), NEG_INF, F32))
    m = jnp.max(mx, axis=-1, keepdims=True)

    acc_ref[...] = jnp.zeros_like(acc_ref)

    def p3b(kc, c):
        p = jnp.exp2(lg_ref[kc] - m).astype(BF16)
        acc_ref[...] += _dot(p, chunk(va_ref, kc))
        return c

    lax.fori_loop(0, nkc, p3b, 0)
    for h in range(N_HEADS):
        acc = acc_ref[h * tq:(h + 1) * tq, :]
        o_ref[0, :, h * HEAD_DIM:(h + 1) * HEAD_DIM] = (
            acc[:, :HEAD_DIM] / acc[:, HEAD_DIM:HEAD_DIM + 1]).astype(o_ref.dtype)


def _dsa(qa, ka, va, qi, kd, wi):
    B, S, _ = qa.shape
    tq, tkc = TQ_DSA, min(TK_DSA, S)
    topk = min(TOPK_MAX, S // 4)
    qblk = lambda c: pl.BlockSpec((1, tq, c), lambda b, i: (b, i, 0))
    full = lambda c: pl.BlockSpec((1, S, c), lambda b, i: (b, 0, 0))
    return pl.pallas_call(
        functools.partial(_dsa_kernel, tkc=tkc, topk=topk),
        grid=(B, S // tq),
        in_specs=[qblk(512), full(128), full(AUG), qblk(512), full(128),
                  pl.BlockSpec((1, N_IDX_HEADS, tq), lambda b, i: (b, 0, i))],
        out_specs=qblk(512),
        out_shape=jax.ShapeDtypeStruct((B, S, N_HEADS * HEAD_DIM), BF16),
        scratch_shapes=[pltpu.VMEM((S // tkc, tkc, tq), F32),
                        pltpu.VMEM((N_IDX_HEADS * tq, LANES), BF16),
                        pltpu.VMEM((S // tkc, N_HEADS * tq, tkc), F32),
                        pltpu.VMEM((N_HEADS * tq, AUG), F32)],
        compiler_params=_cparams("arbitrary", "arbitrary"),
        name="dsa_attn",
    )(qa, ka, va, qi, kd, wi)


def _rope_table_kernel(pos_ref, inv_ref, cos_ref, sin_ref):
    tm = pos_ref.shape[1]
    lane = _lane_iota((tm, LANES))
    ang = pos_ref[0].astype(F32) * inv_ref[...]
    half = QK_ROPE // 2
    cos_ref[0] = jnp.where(lane < QK_ROPE, jnp.cos(ang), 0.0)
    sn = jnp.sin(ang)
    sin_ref[0] = jnp.where(lane < half, -sn, jnp.where(lane < QK_ROPE, sn, 0.0))


def _rope_tables(positions):
    B, S = positions.shape
    tm = TM_PROJ
    half = QK_ROPE // 2
    inv = ROPE_THETA ** (-jnp.arange(half, dtype=F32) / half)
    inv = jnp.concatenate([inv, inv, jnp.zeros((LANES - QK_ROPE,), F32)])[None, :]
    return pl.pallas_call(
        _rope_table_kernel,
        grid=(B, S // tm),
        in_specs=[pl.BlockSpec((1, tm, 1), lambda b, j: (b, j, 0)),
                  pl.BlockSpec((1, LANES), lambda b, j: (0, 0))],
        out_specs=[pl.BlockSpec((1, tm, LANES), lambda b, j: (b, j, 0))] * 2,
        out_shape=[jax.ShapeDtypeStruct((B, S, LANES), F32)] * 2,
        compiler_params=_cparams("arbitrary", "arbitrary"),
        name="rope_tables",
    )(positions[:, :, None], inv)


def _rope(t, cos, sin):
    lane = _lane_iota(t.shape)
    half = QK_ROPE // 2
    swap = jnp.where(lane < half, pltpu.roll(t, LANES - half, 1), pltpu.roll(t, half, 1))
    return t * cos + swap * sin


def _odd_proj_kernel(x_ref, g_ref, w_ref, cqn_ref, ckvn_ref, wuq_ref, wukv_ref, qnc_ref, knc_ref,
                     cos_ref, sin_ref, q_ref, k_ref, v_ref, z_ref, xbc_ref, dt_ref):
    xn = _rms(x_ref[0], g_ref[...]).astype(BF16)

    def proj(lo, hi):
        return _dot(xn, w_ref[:, lo:hi])

    cos = cos_ref[0]
    sin = sin_ref[0]
    n_qk = QK_NOPE + QK_ROPE
    scale = n_qk ** -0.5 * LOG2E
    ones_col = jnp.where(_lane_iota(cos.shape) == 0, 1.0, 0.0).astype(BF16)

    cq = _rms(proj(OD_CQ, OD_CKV), cqn_ref[...]).astype(BF16)
    q = _dot(cq, wuq_ref[...])
    ckv = _rms(proj(OD_CKV, OD_KR), ckvn_ref[...]).astype(BF16)
    kv = _dot(ckv, wukv_ref[...])
    kr = proj(OD_KR, OD_DT)
    kr_ss = jnp.sum(kr * kr, axis=-1, keepdims=True)
    kr_rot = _rope(kr * knc_ref[:, LANES:], cos, sin)
    for h in range(N_HEADS):
        qh = q[:, h * AUG:(h + 1) * AUG]
        qh = _rms(qh, qnc_ref[...], n_qk) * scale
        q_ref[0, :, h * AUG:h * AUG + LANES] = qh[:, :LANES].astype(BF16)
        q_ref[0, :, h * AUG + LANES:(h + 1) * AUG] = _rope(qh[:, LANES:], cos, sin).astype(BF16)
        kn = kv[:, h * AUG:h * AUG + LANES]
        r = lax.rsqrt((jnp.sum(kn * kn, axis=-1, keepdims=True) + kr_ss) * (1.0 / n_qk) + EPS)
        k_ref[0, :, h * AUG:h * AUG + LANES] = (kn * r * knc_ref[:, :LANES]).astype(BF16)
        k_ref[0, :, h * AUG + LANES:(h + 1) * AUG] = (kr_rot * r).astype(BF16)
        v_ref[0, :, h * AUG:h * AUG + HEAD_DIM] = kv[:, h * AUG + LANES:(h + 1) * AUG].astype(BF16)
        v_ref[0, :, h * AUG + HEAD_DIM:(h + 1) * AUG] = ones_col

    dt_ref[0] = proj(OD_DT, OD_Z)
    z_ref[0] = proj(OD_Z, OD_XBC)
    xbc_ref[0] = proj(OD_XBC, OD_END)


def _odd_proj(x, g, w, cqn, ckvn, wuq, wukv, qnc, knc, cos, sin):
    B, S, D = x.shape
    tm = TM_PROJ
    tok = lambda c: pl.BlockSpec((1, tm, c), lambda b, j: (b, j, 0))
    row = lambda c: pl.BlockSpec((1, c), lambda b, j: (0, 0))
    out_cols = (N_HEADS * AUG, N_HEADS * AUG, N_HEADS * AUG, SSM_INNER, CONV_DIM, LANES)
    out_dt = (BF16, BF16, BF16, F32, F32, F32)
    return pl.pallas_call(
        _odd_proj_kernel,
        grid=(B, S // tm),
        in_specs=[tok(D), row(D), _resident(w.shape), row(Q_LORA), row(KV_LORA),
                  _resident(wuq.shape), _resident(wukv.shape), row(AUG), row(AUG),
                  tok(LANES), tok(LANES)],
        out_specs=[tok(c) for c in out_cols],
        out_shape=[jax.ShapeDtypeStruct((B, S, c), dt) for c, dt in zip(out_cols, out_dt)],
        compiler_params=_cparams("arbitrary", "arbitrary"),
        name="odd_proj",
    )(x, g, w, cqn, ckvn, wuq, wukv, qnc, knc, cos, sin)


def _ssd_kernel(xbc_ref, z_ref, dt_ref, cw_ref, cb_ref, dtb_ref, alog_ref, dskip_ref, gn_ref,
                y_ref, xe_ref, st_ref):
    q = xbc_ref.shape[1]
    halo = 8

    @pl.when(pl.program_id(1) == 0)
    def _():
        xe_ref[0:halo, :] = jnp.zeros((halo, CONV_DIM), F32)
        st_ref[...] = jnp.zeros_like(st_ref)

    xe_ref[halo:halo + q, :] = xbc_ref[0]
    conv = cb_ref[...]
    for w in range(CONV_WIDTH):
        off = halo - (CONV_WIDTH - 1) + w
        conv = conv + xe_ref[off:off + q, :] * cw_ref[w:w + 1, :]
    xe_ref[0:halo, :] = xe_ref[q:q + halo, :]
    act = conv * _sigmoid(conv)
    xs = act[:, :SSM_INNER]
    gs = SSM_STATE
    bm = [act[:, SSM_INNER + g * gs:SSM_INNER + (g + 1) * gs] for g in range(SSM_GROUPS)]
    cm = [act[:, SSM_INNER + (SSM_GROUPS + g) * gs:SSM_INNER + (SSM_GROUPS + g + 1) * gs]
          for g in range(SSM_GROUPS)]

    lane = _lane_iota((q, LANES))
    dt = _softplus(dt_ref[0] + dtb_ref[...])
    a = jnp.where(lane < SSM_HEADS, dt * (-jnp.exp(alog_ref[...])), 0.0)
    tri = _tri(q)
    trif = tri.astype(F32)
    acum = _dot_hi(trif, a)
    acum_t = lax.dot_general(a.T, trif, (((1,), (1,)), ((), ())), preferred_element_type=F32,
                             precision=lax.Precision.HIGHEST)
    last = acum[q - 1:q, :]
    dte = jnp.exp(last - acum)
    eac = jnp.exp(acum)
    cdec = jnp.exp(last)

    def expand(mat):
        rows = mat.shape[0]
        ln = _lane_iota((rows, LANES))
        parts = []
        for p in range(SSM_HEADS // 2):
            lo = jnp.broadcast_to(mat[:, 2 * p:2 * p + 1], (rows, LANES))
            hi = jnp.broadcast_to(mat[:, 2 * p + 1:2 * p + 2], (rows, LANES))
            parts.append(jnp.where(ln < SSM_HEAD_DIM, lo, hi))
        return jnp.concatenate(parts, axis=1)

    xd = xs * expand(dt)
    eac_x = expand(eac)
    xdd = (xd * expand(dte)).astype(BF16)
    xd16 = xd.astype(BF16)
    zero16 = jnp.zeros((q, LANES), BF16)

    rpg = SSM_HEADS // SSM_GROUPS
    gw = rpg * SSM_HEAD_DIM
    y_parts = []
    new_states = []
    for g in range(SSM_GROUPS):
        b16 = bm[g].astype(BF16)
        c16 = cm[g].astype(BF16)
        cb = _dot_nt(c16, b16)
        st_old = st_ref[:, g * gw:(g + 1) * gw]
        y_off = _dot(c16, st_old.astype(BF16)) * eac_x[:, g * gw:(g + 1) * gw]
        y_diag = []
        for pp in range(rpg // 2):
            p = g * (rpg // 2) + pp
            xp = xd16[:, p * LANES:(p + 1) * LANES]
            acc = None
            for e in range(2):
                h = 2 * p + e
                diff = acum[:, h:h + 1] - acum_t[h:h + 1, :]
                lm = jnp.exp(jnp.where(tri, diff, NEG_INF))
                gmat = (cb * lm).astype(BF16)
                keep = (lane < SSM_HEAD_DIM) if e == 0 else (lane >= SSM_HEAD_DIM)
                part = _dot(gmat, jnp.where(keep, xp, zero16))
                acc = part if acc is None else acc + part
            y_diag.append(acc)
        y_parts.append(jnp.concatenate(y_diag, axis=1) + y_off)
        st_new = _dot(bm[g].T.astype(BF16), xdd[:, g * gw:(g + 1) * gw])
        new_states.append(st_new)
    cdec_x = expand(cdec)
    for g in range(SSM_GROUPS):
        sl = slice(g * gw, (g + 1) * gw)
        st_ref[:, sl] = st_ref[:, sl] * cdec_x[:, sl] + new_states[g]

    y = jnp.concatenate(y_parts, axis=1) + dskip_ref[...] * xs
    zz = z_ref[0]
    y = y * (zz * _sigmoid(zz))
    for g in range(SSM_GROUPS):
        sl = slice(g * gw, (g + 1) * gw)
        y_ref[0, :, sl] = _rms(y[:, sl], gn_ref[:, sl]).astype(y_ref.dtype)


def _ssd(xbc, z, dt, cw, cb, dtb, alog, dskip, gn):
    B, S, _ = xbc.shape
    q = SSD_CHUNK
    tok = lambda c: pl.BlockSpec((1, q, c), lambda b, j: (b, j, 0))
    row = lambda r, c: pl.BlockSpec((r, c), lambda b, j: (0, 0))
    return pl.pallas_call(
        _ssd_kernel,
        grid=(B, S // q),
        in_specs=[tok(CONV_DIM), tok(SSM_INNER), tok(LANES), row(CONV_WIDTH, CONV_DIM),
                  row(1, CONV_DIM), row(1, LANES), row(1, LANES), row(1, SSM_INNER),
                  row(1, SSM_INNER)],
        out_specs=tok(SSM_INNER),
        out_shape=jax.ShapeDtypeStruct((B, S, SSM_INNER), BF16),
        scratch_shapes=[pltpu.VMEM((q + 8, CONV_DIM), F32), pltpu.VMEM((SSM_STATE, SSM_INNER), F32)],
        compiler_params=_cparams("arbitrary", "arbitrary"),
        name="ssd_scan",
    )(xbc, z, dt, cw, cb, dtb, alog, dskip, gn)


def _out_mlp_kernel(x_ref, o1_ref, o2_ref, wo1_ref, wo2_ref, g_ref, w1_ref, w2_ref, y_ref,
                    xn_ref, acc_ref, *, tf):
    xnew = x_ref[...] + _dot(o1_ref[...], wo1_ref[...]) + _dot(o2_ref[...], wo2_ref[...])
    xn_ref[...] = _rms(xnew, g_ref[...]).astype(BF16)
    acc_ref[...] = xnew

    def body(c, carry):
        cs = pl.multiple_of(c * tf, tf)
        h = _dot(xn_ref[...], w1_ref[:, pl.ds(cs, tf)])
        a = jnp.square(jnp.maximum(h, 0.0)).astype(BF16)
        acc_ref[...] += _dot(a, w2_ref[pl.ds(cs, tf), :])
        return carry

    lax.fori_loop(0, w1_ref.shape[1] // tf, body, 0)
    y_ref[...] = acc_ref[...]


def _out_mlp(x, o1, o2, wo1, wo2, g, w1, w2):
    T, D = x.shape
    tm = TM_MLP
    tok = lambda c: pl.BlockSpec((tm, c), lambda i: (i, 0))
    return pl.pallas_call(
        functools.partial(_out_mlp_kernel, tf=TF_MLP),
        grid=(T // tm,),
        in_specs=[tok(D), tok(o1.shape[1]), tok(o2.shape[1]), _resident(wo1.shape),
                  _resident(wo2.shape), pl.BlockSpec((1, D), lambda i: (0, 0)),
                  _resident(w1.shape), _resident(w2.shape)],
        out_specs=tok(D),
        out_shape=jax.ShapeDtypeStruct((T, D), F32),
        scratch_shapes=[pltpu.VMEM((tm, D), BF16), pltpu.VMEM((tm, D), F32)],
        compiler_params=_cparams("arbitrary"),
        name="out_mlp",
    )(x, o1, o2, wo1, wo2, g, w1, w2)


def _pack_even_w(w):
    w = w.astype(BF16)
    qa, ka, va, qi, ki, wi, qb, kb, vb, fb = jnp.split(
        w, [512, 640, 768, 1280, 1344, 1352, 1864, 2376, 2888], axis=-1)
    pad = jnp.zeros(w.shape[:-1] + (LANES - IDX_DIM - N_IDX_HEADS - N_HEADS,), w.dtype)
    return jnp.concatenate([qa, ka, va, qi, qb, kb, vb, ki, wi, fb, pad], axis=-1)


def _pack_odd_w(w):
    w = w.astype(BF16)
    cq, ckv, kr, z, xbc, dt = jnp.split(w, [384, 640, 704, 1728, 3776], axis=-1)
    zpad = lambda n: jnp.zeros(w.shape[:-1] + (n,), w.dtype)
    return jnp.concatenate([cq, ckv, kr, zpad(LANES - QK_ROPE), dt, zpad(LANES - SSM_HEADS), z, xbc],
                           axis=-1)


def _pack_wuq(w):
    n_qk = QK_NOPE + QK_ROPE
    w = w.reshape(w.shape[0], N_HEADS, n_qk)
    w = jnp.pad(w, ((0, 0), (0, 0), (0, AUG - n_qk)))
    return w.reshape(w.shape[0], N_HEADS * AUG).astype(BF16)


def _row(v, width=None):
    v = v.astype(F32)[None, :]
    if width is not None and width > v.shape[1]:
        v = jnp.pad(v, ((0, 0), (0, width - v.shape[1])))
    return v


def kernel(x, positions, ev_norm, ev_w_in, ev_b_f, ev_qn_a, ev_kn_a, ev_qn_b, ev_kn_b, ev_w_out,
           od_norm, od_w_in, od_cq_norm, od_ckv_norm, od_w_uq, od_w_ukv, od_qn_c, od_kn_c,
           od_conv_w, od_conv_b, od_dt_bias, od_a_log, od_d_skip, od_gate_norm, od_w_out,
           mlp_norm, mlp_w1, mlp_w2):
    B, S, D = x.shape
    depth = mlp_w1.shape[0]
    ev_w, od_w = _pack_even_w(ev_w_in), _pack_odd_w(od_w_in)
    ev_wo, od_wo = ev_w_out.astype(BF16), od_w_out.astype(BF16)
    w1, w2 = mlp_w1.astype(BF16), mlp_w2.astype(BF16)
    cos = sin = None
    for layer in range(depth):
        i = layer // 2
        if layer % 2 == 0:
            bf = jnp.zeros((1, LANES), F32).at[0, MISC_F:MISC_F + N_HEADS].set(ev_b_f[i])
            qa, ka, va, qi, kd, wi, qb, kb, vb = _even_proj(
                x, _row(ev_norm[i]), ev_w[i], bf, _row(ev_qn_a[i]),
                _row(ev_kn_a[i]), _row(ev_qn_b[i]), _row(ev_kn_b[i]))
            o1 = _dsa(qa, ka, va, qi, kd, wi)
            o2 = _flash(qb, kb, vb)
            wo = ev_wo[i]
        else:
            if cos is None:
                cos, sin = _rope_tables(positions)
            q, k, v, z, xbc, dt = _odd_proj(
                x, _row(od_norm[i]), od_w[i], _row(od_cq_norm[i]),
                _row(od_ckv_norm[i]), _pack_wuq(od_w_uq[i]), od_w_ukv[i].astype(BF16),
                _row(od_qn_c[i], AUG), _row(od_kn_c[i], AUG), cos, sin)
            o1 = _flash(q, k, v)
            o2 = _ssd(xbc, z, dt, od_conv_w[i].astype(F32), _row(od_conv_b[i]),
                      _row(od_dt_bias[i], LANES), _row(od_a_log[i], LANES),
                      _row(jnp.repeat(od_d_skip[i], SSM_HEAD_DIM)), _row(od_gate_norm[i]))
            wo = od_wo[i]
        k1 = o1.shape[-1]
        x = _out_mlp(x.reshape(B * S, D), o1.reshape(B * S, k1), o2.reshape(B * S, -1),
                     wo[:k1], wo[k1:], _row(mlp_norm[layer]), w1[layer], w2[layer]).reshape(B, S, D)
    return x
```

```python
import functools

import jax
import jax.numpy as jnp
from jax import lax
from jax.experimental import pallas as pl
from jax.experimental.pallas import tpu as pltpu

F32 = jnp.float32
BF16 = jnp.bfloat16
EPS = 1e-6
NEG_INF = float("-inf")
F32_MAX = 3.4028234663852886e38
LOG2E = 1.4426950408889634

D_MODEL = 1024
HEAD_DIM = 128
N_HEADS = 4
N_IDX_HEADS = 8
IDX_DIM = 64
TOPK_MAX = 256
QK_NOPE = 128
QK_ROPE = 64
Q_LORA = 384
KV_LORA = 256
ROPE_THETA = 10000.0
SSM_INNER = 1024
SSM_HEAD_DIM = 64
SSM_HEADS = 16
SSM_GROUPS = 4
SSM_STATE = 128
CONV_WIDTH = 4
CONV_DIM = SSM_INNER + 2 * SSM_GROUPS * SSM_STATE
D_FF = 4 * D_MODEL

LANES = 128
AUG = 2 * LANES
VMEM_LIMIT = 56 * 1024 * 1024

TM_PROJ = 512
TM_MLP = 512
TF_MLP = 512
TQ_FLASH = 1024
TK_FLASH = 1024
TQ_DSA = 256
TK_DSA = 512
SSD_CHUNK = 128
BISECT_GROUP = 4
BISECT_ROUND = 24
BISECT_MAX_ROUNDS = 8

EV_QA, EV_KA, EV_VA, EV_QI, EV_QB, EV_KB, EV_VB, EV_MISC, EV_END = (
    0, 512, 640, 768, 1280, 1792, 2304, 2816, 2944)
MISC_W = 64
MISC_F = 72
OD_CQ, OD_CKV, OD_KR, OD_DT, OD_Z, OD_XBC, OD_END = (0, 384, 640, 768, 896, 1920, 3968)


def _cparams(*sem):
    return pltpu.CompilerParams(dimension_semantics=sem, vmem_limit_bytes=VMEM_LIMIT)


def _resident(shape):
    nd = len(shape)
    return pl.BlockSpec(shape, lambda *_: (0,) * nd, pipeline_mode=pl.Buffered(1))


def _rms(x, g, n=None):
    n = x.shape[-1] if n is None else n
    ss = jnp.sum(x * x, axis=-1, keepdims=True)
    return x * lax.rsqrt(ss * (1.0 / n) + EPS) * g


def _dot(a, b):
    return jnp.dot(a, b, preferred_element_type=F32)


def _dot_nt(a, b):
    return lax.dot_general(a, b, (((1,), (1,)), ((), ())), preferred_element_type=F32)


def _dot_hi(a, b):
    return jnp.dot(a, b, preferred_element_type=F32, precision=lax.Precision.HIGHEST)


def _sigmoid(x):
    return 1.0 / (1.0 + jnp.exp(-x))


def _softplus(x):
    return jnp.maximum(x, 0.0) + jnp.log(1.0 + jnp.exp(-jnp.abs(x)))


def _lane_iota(shape):
    return lax.broadcasted_iota(jnp.int32, shape, len(shape) - 1)


def _tri(n):
    r = lax.broadcasted_iota(jnp.int32, (n, n), 0)
    c = lax.broadcasted_iota(jnp.int32, (n, n), 1)
    return r >= c


def _split3(c):
    hi = c.astype(BF16).astype(F32)
    mid = (c - hi).astype(BF16).astype(F32)
    lo = (c - hi - mid).astype(BF16).astype(F32)
    return hi, mid, lo


def _even_proj_kernel(x_ref, g_ref, w_ref, bf_ref, qna_ref, kna_ref, qnb_ref, knb_ref,
                      qa_ref, ka_ref, va_ref, qi_ref, kd_ref, wi_ref, qb_ref, kb_ref, vb_ref,
                      carry_ref):
    tm = x_ref.shape[1]

    @pl.when(pl.program_id(1) == 0)
    def _():
        carry_ref[...] = jnp.zeros_like(carry_ref)

    xn = _rms(x_ref[0], g_ref[...]).astype(BF16)

    def proj(lo, hi):
        return _dot(xn, w_ref[:, lo:hi])

    scale = HEAD_DIM ** -0.5 * LOG2E
    lane = _lane_iota((tm, LANES))
    ones_col = jnp.where(lane == 0, 1.0, 0.0).astype(BF16)
    qa = proj(EV_QA, EV_KA)
    for h in range(N_HEADS):
        sl = slice(h * HEAD_DIM, (h + 1) * HEAD_DIM)
        qa_ref[0, :, sl] = (_rms(qa[:, sl], qna_ref[...]) * scale).astype(BF16)
    ka_ref[0] = _rms(proj(EV_KA, EV_VA), kna_ref[...]).astype(BF16)
    va_ref[0, :, :HEAD_DIM] = proj(EV_VA, EV_QI).astype(BF16)
    va_ref[0, :, HEAD_DIM:] = ones_col
    qi_ref[0] = proj(EV_QI, EV_QB).astype(BF16)
    vb = proj(EV_VB, EV_MISC)
    for h in range(N_HEADS):
        vb_ref[0, :, h * AUG:h * AUG + HEAD_DIM] = vb[:, h * HEAD_DIM:(h + 1) * HEAD_DIM].astype(BF16)
        vb_ref[0, :, h * AUG + HEAD_DIM:(h + 1) * AUG] = ones_col

    misc = proj(EV_MISC, EV_END)
    kd_ref[0] = jnp.where(lane < IDX_DIM, misc, pltpu.roll(misc, IDX_DIM, 1)).astype(BF16)
    wi_ref[0] = misc.T[MISC_W:MISC_W + N_IDX_HEADS, :]

    zf = misc + bf_ref[...]
    logf = jnp.minimum(zf, 0.0) - jnp.log(1.0 + jnp.exp(-jnp.abs(zf)))
    logf = jnp.where((lane >= MISC_F) & (lane < MISC_F + N_HEADS), logf, 0.0)
    tri = _tri(tm).astype(BF16)
    csum = sum(_dot(tri, part.astype(BF16)) for part in _split3(logf)) + carry_ref[...]
    carry_ref[...] = csum[tm - 1:tm, :]

    qb = proj(EV_QB, EV_KB)
    kb = proj(EV_KB, EV_VB)
    one = jnp.ones((tm, LANES), F32)
    zero = jnp.zeros((tm, LANES), F32)
    for h in range(N_HEADS):
        sl = slice(h * HEAD_DIM, (h + 1) * HEAD_DIM)
        c = jnp.broadcast_to(csum[:, MISC_F + h:MISC_F + h + 1], (tm, LANES)) * LOG2E
        hi, mid, lo = _split3(c)
        aq = jnp.where(lane == 0, hi, jnp.where(lane == 1, mid, jnp.where(lane == 2, lo,
             jnp.where(lane < 6, one, zero))))
        ak = jnp.where(lane < 3, one, jnp.where(lane == 3, -hi, jnp.where(lane == 4, -mid,
             jnp.where(lane == 5, -lo, zero))))
        qb_ref[0, :, h * AUG:h * AUG + LANES] = (_rms(qb[:, sl], qnb_ref[...]) * scale).astype(BF16)
        qb_ref[0, :, h * AUG + LANES:(h + 1) * AUG] = aq.astype(BF16)
        kb_ref[0, :, h * AUG:h * AUG + LANES] = _rms(kb[:, sl], knb_ref[...]).astype(BF16)
        kb_ref[0, :, h * AUG + LANES:(h + 1) * AUG] = ak.astype(BF16)


def _even_proj(x, g, w, bf, qna, kna, qnb, knb):
    B, S, D = x.shape
    tm = TM_PROJ
    tok = lambda c: pl.BlockSpec((1, tm, c), lambda b, j: (b, j, 0))
    row = lambda c: pl.BlockSpec((1, c), lambda b, j: (0, 0))
    out_cols = (512, 128, AUG, 512, 128, None, N_HEADS * AUG, N_HEADS * AUG, N_HEADS * AUG)
    out_specs = [tok(c) if c else pl.BlockSpec((1, N_IDX_HEADS, tm), lambda b, j: (b, 0, j))
                 for c in out_cols]
    out_shape = [jax.ShapeDtypeStruct((B, S, c), BF16) if c else
                 jax.ShapeDtypeStruct((B, N_IDX_HEADS, S), F32) for c in out_cols]
    return pl.pallas_call(
        _even_proj_kernel,
        grid=(B, S // tm),
        in_specs=[tok(D), row(D), _resident(w.shape), row(LANES), row(LANES), row(LANES),
                  row(LANES), row(LANES)],
        out_specs=out_specs,
        out_shape=out_shape,
        scratch_shapes=[pltpu.VMEM((1, LANES), F32)],
        compiler_params=_cparams("arbitrary", "arbitrary"),
        name="even_proj",
    )(x, g, w, bf, qna, kna, qnb, knb)


def _fold_max(mx, s):
    for g in range(s.shape[1] // LANES):
        mx = jnp.maximum(mx, s[:, g * LANES:(g + 1) * LANES])
    return mx


def _flash_kernel(q_ref, k_ref, v_ref, o_ref, lg_ref, acc_ref, *, tq, tk):
    i = pl.program_id(2)
    q = q_ref[0]
    ratio = tq // tk
    nfull = i * ratio

    def kchunk(ref, j):
        return ref[0, pl.ds(pl.multiple_of(j * tk, tk), tk), :]

    def pass_a(j, mx):
        s = _dot_nt(q, kchunk(k_ref, j))
        lg_ref[j] = s
        return _fold_max(mx, s)

    mx = lax.fori_loop(0, nfull, pass_a, jnp.full((tq, LANES), NEG_INF, F32))
    for d in range(ratio):
        j = nfull + d
        row = i * tq + lax.broadcasted_iota(jnp.int32, (tq, tk), 0)
        col = j * tk + lax.broadcasted_iota(jnp.int32, (tq, tk), 1)
        s = jnp.where(col <= row, _dot_nt(q, kchunk(k_ref, j)), NEG_INF)
        lg_ref[j] = s
        mx = _fold_max(mx, s)
    m = jnp.max(mx, axis=-1, keepdims=True)

    acc_ref[...] = jnp.zeros_like(acc_ref)

    def pass_b(j, c):
        p = jnp.exp2(lg_ref[j] - m).astype(BF16)
        acc_ref[...] += _dot(p, kchunk(v_ref, j))
        return c

    lax.fori_loop(0, nfull + ratio, pass_b, 0)
    acc = acc_ref[...]
    o_ref[0] = (acc[:, :HEAD_DIM] / acc[:, HEAD_DIM:HEAD_DIM + 1]).astype(o_ref.dtype)


def _flash(q, k, v):
    B, S, _ = q.shape
    tq, tk = min(TQ_FLASH, S), min(TK_FLASH, S)
    return pl.pallas_call(
        functools.partial(_flash_kernel, tq=tq, tk=tk),
        grid=(B, N_HEADS, S // tq),
        in_specs=[pl.BlockSpec((1, tq, AUG), lambda b, h, i: (b, i, h)),
                  pl.BlockSpec((1, S, AUG), lambda b, h, i: (b, 0, h)),
                  pl.BlockSpec((1, S, AUG), lambda b, h, i: (b, 0, h))],
        out_specs=pl.BlockSpec((1, tq, HEAD_DIM), lambda b, h, i: (b, i, h)),
        out_shape=jax.ShapeDtypeStruct((B, S, N_HEADS * HEAD_DIM), BF16),
        scratch_shapes=[pltpu.VMEM((S // tk, tq, tk), F32), pltpu.VMEM((tq, AUG), F32)],
        compiler_params=_cparams("arbitrary", "arbitrary", "arbitrary"),
        name="flash_attn",
    )(q, k, v)


def _dsa_kernel(qa_ref, ka_ref, va_ref, qi_ref, kd_ref, wi_ref, o_ref,
                sc_ref, qstk_ref, lg_ref, acc_ref, *, tkc, topk):
    tq = qa_ref.shape[1]
    i = pl.program_id(1)
    nkc = (i * tq + tq + tkc - 1) // tkc
    kf = float(topk)
    lane = _lane_iota((tq, LANES))
    keyi = lax.broadcasted_iota(jnp.int32, (tkc, tq), 0)
    qpos = i * tq + lax.broadcasted_iota(jnp.int32, (tkc, tq), 1)

    q8 = qi_ref[0]
    for j in range(N_IDX_HEADS):
        grp = q8[:, LANES * (j // 2):LANES * (j // 2 + 1)]
        keep = (lane < IDX_DIM) if j % 2 == 0 else (lane >= IDX_DIM)
        qstk_ref[tq * j:tq * (j + 1), :] = jnp.where(keep, grp, jnp.zeros_like(grp))
    wrows = [wi_ref[0, j:j + 1, :] for j in range(N_IDX_HEADS)]

    def chunk(ref, kc):
        return ref[0, pl.ds(pl.multiple_of(kc * tkc, tkc), tkc), :]

    def fold8(x, op):
        rows = x.shape[0]
        while rows > 8:
            rows //= 2
            x = op(x[:rows], x[rows:])
        return x

    def p1(kc, carry):
        rmax, rmin = carry
        s8 = _dot_nt(chunk(kd_ref, kc), qstk_ref[...])
        score = wrows[0] * jnp.maximum(s8[:, 0:tq], 0.0)
        for j in range(1, N_IDX_HEADS):
            score = score + wrows[j] * jnp.maximum(s8[:, tq * j:tq * (j + 1)], 0.0)
        adm = (kc * tkc + keyi) <= qpos
        sc_ref[kc] = jnp.where(adm, score, NEG_INF)
        rmax = jnp.maximum(rmax, fold8(jnp.where(adm, score, NEG_INF), jnp.maximum))
        rmin = jnp.minimum(rmin, fold8(jnp.where(adm, score, -NEG_INF), jnp.minimum))
        return rmax, rmin

    rmax, rmin = lax.fori_loop(0, nkc, p1, (jnp.full((8, tq), NEG_INF, F32),
                                            jnp.full((8, tq), -NEG_INF, F32)))
    rmax = jnp.max(rmax, axis=0, keepdims=True)
    rmin = jnp.min(rmin, axis=0, keepdims=True)

    def row_counts(*preds):
        def cb(kc, cs):
            blk = sc_ref[kc]
            return tuple(c + fold8(jnp.where(p(blk, kc), 1.0, 0.0), jnp.add)
                         for c, p in zip(cs, preds))
        cs = lax.fori_loop(0, nkc, cb, tuple(jnp.zeros((8, tq), F32) for _ in preds))
        return tuple(jnp.sum(c, axis=0, keepdims=True) for c in cs)

    def row_count(pred):
        return row_counts(pred)[0]

    def any_row(flag):
        return jnp.max(flag.astype(jnp.int32))

    def select_threshold():
        hi0 = rmax + (jnp.abs(rmax) * 1.2e-7 + 1e-37)
        n_adm = (qpos[0:1, :] + 1).astype(F32)
        c_gt0, c_ge0 = row_counts(lambda blk, kc: blk > 0.0, lambda blk, kc: blk >= 0.0)
        above = c_ge0 >= kf
        pend0 = (above & (c_gt0 < kf)).astype(jnp.int32)
        lo0 = jnp.where(above, 0.0, rmin)
        cnt0 = jnp.where(above, c_ge0, n_adm)
        hi0 = jnp.where(above, hi0, 0.0)

        def unresolved(cnt, pend):
            return any_row((cnt != kf) & (pend == 0))

        def bis_cond(st):
            return (st[-1] > 0) & (st[0] < BISECT_ROUND)

        def bis_body(st):
            it, lo, hi, cnt, pend, _ = st
            for _ in range(BISECT_GROUP):
                mid = lo + (hi - lo) * 0.5
                c = row_count(lambda blk, kc, mid=mid: blk >= mid)
                up = c >= kf
                lo = jnp.where(up, mid, lo)
                hi = jnp.where(up, hi, mid)
                cnt = jnp.where(up, c, cnt)
            return it + BISECT_GROUP, lo, hi, cnt, pend, unresolved(cnt, pend)

        def resolve_ties(lo, cnt):
            def vmin(kc, v):
                blk = sc_ref[kc]
                return jnp.minimum(v, fold8(jnp.where(blk >= lo, blk, -NEG_INF), jnp.minimum))
            v = lax.fori_loop(0, nkc, vmin, jnp.full((8, tq), -NEG_INF, F32))
            v = jnp.min(v, axis=0, keepdims=True)
            cgt = row_count(lambda blk, kc: blk > v)
            tie = (cnt != kf) & (cgt < kf)
            need = kf - cgt
            tri = _tri(tkc).astype(BF16)

            def drop(kc, seen):
                blk = sc_ref[kc]
                eq = tie & (blk == v)
                rank = _dot(tri, jnp.where(eq, 1.0, 0.0).astype(BF16)) + seen
                sc_ref[kc] = jnp.where(eq & (rank > need), NEG_INF, blk)
                return rank[tkc - 1:tkc, :]
            lax.fori_loop(0, nkc, drop, jnp.zeros((1, tq), F32))
            return jnp.where(tie, v, lo), jnp.where(tie, kf, cnt)

        def round_cond(st):
            return (st[-1] > 0) & (st[0] < BISECT_MAX_ROUNDS)

        def round_body(st):
            r, lo, hi, cnt, pend, _ = st
            _, lo, hi, cnt, pend, _ = lax.while_loop(
                bis_cond, bis_body, (0, lo, hi, cnt, pend, unresolved(cnt, pend)))
            lo, cnt = lax.cond(any_row(cnt != kf) > 0, resolve_ties, lambda a, b: (a, b), lo, cnt)
            return r + 1, lo, hi, cnt, jnp.zeros_like(pend), any_row(cnt != kf)

        st = lax.while_loop(round_cond, round_body,
                            (0, lo0, hi0, cnt0, pend0, any_row(cnt0 != kf)))
        return st[1]

    thr = lax.cond((i + 1) * tq > topk, select_threshold,
                   lambda: jnp.full((1, tq), -F32_MAX, F32))

    qa = qa_ref[0]
    qs = jnp.concatenate([qa[:, h * HEAD_DIM:(h + 1) * HEAD_DIM] for h in range(N_HEADS)], axis=0)

    def p3a(kc, mx):
        s = _dot_nt(qs, chunk(ka_ref, kc))
        sel = jnp.where(sc_ref[kc] >= thr, 0.0, NEG_INF).T
        parts = []
        for h in range(N_HEADS):
            rs = slice(h * tq, (h + 1) * tq)
            sh = s[rs] + sel
            lg_ref[kc, rs, :] = sh
            parts.append(_fold_max(mx[rs], sh))
        return jnp.concatenate(parts, axis=0)

    mx = lax.fori_loop(0, nkc, p3a, jnp.full((N_HEADS * tq, LANES), NEG_INF, F32))
    m = jnp.max(mx, axis=-1, keepdims=True)

    acc_ref[...] = jnp.zeros_like(acc_ref)

    def p3b(kc, c):
        p = jnp.exp2(lg_ref[kc] - m).astype(BF16)
        acc_ref[...] += _dot(p, chunk(va_ref, kc))
        return c

    lax.fori_loop(0, nkc, p3b, 0)
    for h in range(N_HEADS):
        acc = acc_ref[h * tq:(h + 1) * tq, :]
        o_ref[0, :, h * HEAD_DIM:(h + 1) * HEAD_DIM] = (
            acc[:, :HEAD_DIM] / acc[:, HEAD_DIM:HEAD_DIM + 1]).astype(o_ref.dtype)


def _dsa(qa, ka, va, qi, kd, wi):
    B, S, _ = qa.shape
    tq, tkc = TQ_DSA, min(TK_DSA, S)
    topk = min(TOPK_MAX, S // 4)
    qblk = lambda c: pl.BlockSpec((1, tq, c), lambda b, i: (b, i, 0))
    full = lambda c: pl.BlockSpec((1, S, c), lambda b, i: (b, 0, 0))
    return pl.pallas_call(
        functools.partial(_dsa_kernel, tkc=tkc, topk=topk),
        grid=(B, S // tq),
        in_specs=[qblk(512), full(128), full(AUG), qblk(512), full(128),
                  pl.BlockSpec((1, N_IDX_HEADS, tq), lambda b, i: (b, 0, i))],
        out_specs=qblk(512),
        out_shape=jax.ShapeDtypeStruct((B, S, N_HEADS * HEAD_DIM), BF16),
        scratch_shapes=[pltpu.VMEM((S // tkc, tkc, tq), F32),
                        pltpu.VMEM((N_IDX_HEADS * tq, LANES), BF16),
                        pltpu.VMEM((S // tkc, N_HEADS * tq, tkc), F32),
                        pltpu.VMEM((N_HEADS * tq, AUG), F32)],
        compiler_params=_cparams("arbitrary", "arbitrary"),
        name="dsa_attn",
    )(qa, ka, va, qi, kd, wi)


def _rope_table_kernel(pos_ref, inv_ref, cos_ref, sin_ref):
    tm = pos_ref.shape[1]
    lane = _lane_iota((tm, LANES))
    ang = pos_ref[0].astype(F32) * inv_ref[...]
    half = QK_ROPE // 2
    cos_ref[0] = jnp.where(lane < QK_ROPE, jnp.cos(ang), 0.0)
    sn = jnp.sin(ang)
    sin_ref[0] = jnp.where(lane < half, -sn, jnp.where(lane < QK_ROPE, sn, 0.0))


def _rope_tables(positions):
    B, S = positions.shape
    tm = TM_PROJ
    half = QK_ROPE // 2
    inv = ROPE_THETA ** (-jnp.arange(half, dtype=F32) / half)
    inv = jnp.concatenate([inv, inv, jnp.zeros((LANES - QK_ROPE,), F32)])[None, :]
    return pl.pallas_call(
        _rope_table_kernel,
        grid=(B, S // tm),
        in_specs=[pl.BlockSpec((1, tm, 1), lambda b, j: (b, j, 0)),
                  pl.BlockSpec((1, LANES), lambda b, j: (0, 0))],
        out_specs=[pl.BlockSpec((1, tm, LANES), lambda b, j: (b, j, 0))] * 2,
        out_shape=[jax.ShapeDtypeStruct((B, S, LANES), F32)] * 2,
        compiler_params=_cparams("arbitrary", "arbitrary"),
        name="rope_tables",
    )(positions[:, :, None], inv)


def _rope(t, cos, sin):
    lane = _lane_iota(t.shape)
    half = QK_ROPE // 2
    swap = jnp.where(lane < half, pltpu.roll(t, LANES - half, 1), pltpu.roll(t, half, 1))
    return t * cos + swap * sin


def _odd_proj_kernel(x_ref, g_ref, w_ref, cqn_ref, ckvn_ref, wuq_ref, wukv_ref, qnc_ref, knc_ref,
                     cos_ref, sin_ref, q_ref, k_ref, v_ref, z_ref, xbc_ref, dt_ref):
    xn = _rms(x_ref[0], g_ref[...]).astype(BF16)

    def proj(lo, hi):
        return _dot(xn, w_ref[:, lo:hi])

    cos = cos_ref[0]
    sin = sin_ref[0]
    n_qk = QK_NOPE + QK_ROPE
    scale = n_qk ** -0.5 * LOG2E
    ones_col = jnp.where(_lane_iota(cos.shape) == 0, 1.0, 0.0).astype(BF16)

    cq = _rms(proj(OD_CQ, OD_CKV), cqn_ref[...]).astype(BF16)
    q = _dot(cq, wuq_ref[...])
    ckv = _rms(proj(OD_CKV, OD_KR), ckvn_ref[...]).astype(BF16)
    kv = _dot(ckv, wukv_ref[...])
    kr = proj(OD_KR, OD_DT)
    kr_ss = jnp.sum(kr * kr, axis=-1, keepdims=True)
    kr_rot = _rope(kr * knc_ref[:, LANES:], cos, sin)
    for h in range(N_HEADS):
        qh = q[:, h * AUG:(h + 1) * AUG]
        qh = _rms(qh, qnc_ref[...], n_qk) * scale
        q_ref[0, :, h * AUG:h * AUG + LANES] = qh[:, :LANES].astype(BF16)
        q_ref[0, :, h * AUG + LANES:(h + 1) * AUG] = _rope(qh[:, LANES:], cos, sin).astype(BF16)
        kn = kv[:, h * AUG:h * AUG + LANES]
        r = lax.rsqrt((jnp.sum(kn * kn, axis=-1, keepdims=True) + kr_ss) * (1.0 / n_qk) + EPS)
        k_ref[0, :, h * AUG:h * AUG + LANES] = (kn * r * knc_ref[:, :LANES]).astype(BF16)
        k_ref[0, :, h * AUG + LANES:(h + 1) * AUG] = (kr_rot * r).astype(BF16)
        v_ref[0, :, h * AUG:h * AUG + HEAD_DIM] = kv[:, h * AUG + LANES:(h + 1) * AUG].astype(BF16)
        v_ref[0, :, h * AUG + HEAD_DIM:(h + 1) * AUG] = ones_col

    dt_ref[0] = proj(OD_DT, OD_Z)
    z_ref[0] = proj(OD_Z, OD_XBC)
    xbc_ref[0] = proj(OD_XBC, OD_END)


def _odd_proj(x, g, w, cqn, ckvn, wuq, wukv, qnc, knc, cos, sin):
    B, S, D = x.shape
    tm = TM_PROJ
    tok = lambda c: pl.BlockSpec((1, tm, c), lambda b, j: (b, j, 0))
    row = lambda c: pl.BlockSpec((1, c), lambda b, j: (0, 0))
    out_cols = (N_HEADS * AUG, N_HEADS * AUG, N_HEADS * AUG, SSM_INNER, CONV_DIM, LANES)
    out_dt = (BF16, BF16, BF16, F32, F32, F32)
    return pl.pallas_call(
        _odd_proj_kernel,
        grid=(B, S // tm),
        in_specs=[tok(D), row(D), _resident(w.shape), row(Q_LORA), row(KV_LORA),
                  _resident(wuq.shape), _resident(wukv.shape), row(AUG), row(AUG),
                  tok(LANES), tok(LANES)],
        out_specs=[tok(c) for c in out_cols],
        out_shape=[jax.ShapeDtypeStruct((B, S, c), dt) for c, dt in zip(out_cols, out_dt)],
        compiler_params=_cparams("arbitrary", "arbitrary"),
        name="odd_proj",
    )(x, g, w, cqn, ckvn, wuq, wukv, qnc, knc, cos, sin)


def _ssd_kernel(xbc_ref, z_ref, dt_ref, cw_ref, cb_ref, dtb_ref, alog_ref, dskip_ref, gn_ref,
                y_ref, xe_ref, st_ref):
    q = xbc_ref.shape[1]
    halo = 8

    @pl.when(pl.program_id(1) == 0)
    def _():
        xe_ref[0:halo, :] = jnp.zeros((halo, CONV_DIM), F32)
        st_ref[...] = jnp.zeros_like(st_ref)

    xe_ref[halo:halo + q, :] = xbc_ref[0]
    conv = cb_ref[...]
    for w in range(CONV_WIDTH):
        off = halo - (CONV_WIDTH - 1) + w
        conv = conv + xe_ref[off:off + q, :] * cw_ref[w:w + 1, :]
    xe_ref[0:halo, :] = xe_ref[q:q + halo, :]
    act = conv * _sigmoid(conv)
    xs = act[:, :SSM_INNER]
    gs = SSM_STATE
    bm = [act[:, SSM_INNER + g * gs:SSM_INNER + (g + 1) * gs] for g in range(SSM_GROUPS)]
    cm = [act[:, SSM_INNER + (SSM_GROUPS + g) * gs:SSM_INNER + (SSM_GROUPS + g + 1) * gs]
          for g in range(SSM_GROUPS)]

    lane = _lane_iota((q, LANES))
    dt = _softplus(dt_ref[0] + dtb_ref[...])
    a = jnp.where(lane < SSM_HEADS, dt * (-jnp.exp(alog_ref[...])), 0.0)
    tri = _tri(q)
    trif = tri.astype(F32)
    acum = _dot_hi(trif, a)
    acum_t = lax.dot_general(a.T, trif, (((1,), (1,)), ((), ())), preferred_element_type=F32,
                             precision=lax.Precision.HIGHEST)
    last = acum[q - 1:q, :]
    dte = jnp.exp(last - acum)
    eac = jnp.exp(acum)
    cdec = jnp.exp(last)

    def expand(mat):
        rows = mat.shape[0]
        ln = _lane_iota((rows, LANES))
        parts = []
        for p in range(SSM_HEADS // 2):
            lo = jnp.broadcast_to(mat[:, 2 * p:2 * p + 1], (rows, LANES))
            hi = jnp.broadcast_to(mat[:, 2 * p + 1:2 * p + 2], (rows, LANES))
            parts.append(jnp.where(ln < SSM_HEAD_DIM, lo, hi))
        return jnp.concatenate(parts, axis=1)

    xd = xs * expand(dt)
    eac_x = expand(eac)
    xdd = (xd * expand(dte)).astype(BF16)
    xd16 = xd.astype(BF16)
    zero16 = jnp.zeros((q, LANES), BF16)

    rpg = SSM_HEADS // SSM_GROUPS
    gw = rpg * SSM_HEAD_DIM
    y_parts = []
    new_states = []
    for g in range(SSM_GROUPS):
        b16 = bm[g].astype(BF16)
        c16 = cm[g].astype(BF16)
        cb = _dot_nt(c16, b16)
        st_old = st_ref[:, g * gw:(g + 1) * gw]
        y_off = _dot(c16, st_old.astype(BF16)) * eac_x[:, g * gw:(g + 1) * gw]
        y_diag = []
        for pp in range(rpg // 2):
            p = g * (rpg // 2) + pp
            xp = xd16[:, p * LANES:(p + 1) * LANES]
            acc = None
            for e in range(2):
                h = 2 * p + e
                diff = acum[:, h:h + 1] - acum_t[h:h + 1, :]
                lm = jnp.exp(jnp.where(tri, diff, NEG_INF))
                gmat = (cb * lm).astype(BF16)
                keep = (lane < SSM_HEAD_DIM) if e == 0 else (lane >= SSM_HEAD_DIM)
                part = _dot(gmat, jnp.where(keep, xp, zero16))
                acc = part if acc is None else acc + part
            y_diag.append(acc)
        y_parts.append(jnp.concatenate(y_diag, axis=1) + y_off)
        st_new = _dot(bm[g].T.astype(BF16), xdd[:, g * gw:(g + 1) * gw])
        new_states.append(st_new)
    cdec_x = expand(cdec)
    for g in range(SSM_GROUPS):
        sl = slice(g * gw, (g + 1) * gw)
        st_ref[:, sl] = st_ref[:, sl] * cdec_x[:, sl] + new_states[g]

    y = jnp.concatenate(y_parts, axis=1) + dskip_ref[...] * xs
    zz = z_ref[0]
    y = y * (zz * _sigmoid(zz))
    for g in range(SSM_GROUPS):
        sl = slice(g * gw, (g + 1) * gw)
        y_ref[0, :, sl] = _rms(y[:, sl], gn_ref[:, sl]).astype(y_ref.dtype)


def _ssd(xbc, z, dt, cw, cb, dtb, alog, dskip, gn):
    B, S, _ = xbc.shape
    q = SSD_CHUNK
    tok = lambda c: pl.BlockSpec((1, q, c), lambda b, j: (b, j, 0))
    row = lambda r, c: pl.BlockSpec((r, c), lambda b, j: (0, 0))
    return pl.pallas_call(
        _ssd_kernel,
        grid=(B, S // q),
        in_specs=[tok(CONV_DIM), tok(SSM_INNER), tok(LANES), row(CONV_WIDTH, CONV_DIM),
                  row(1, CONV_DIM), row(1, LANES), row(1, LANES), row(1, SSM_INNER),
                  row(1, SSM_INNER)],
        out_specs=tok(SSM_INNER),
        out_shape=jax.ShapeDtypeStruct((B, S, SSM_INNER), BF16),
        scratch_shapes=[pltpu.VMEM((q + 8, CONV_DIM), F32), pltpu.VMEM((SSM_STATE, SSM_INNER), F32)],
        compiler_params=_cparams("arbitrary", "arbitrary"),
        name="ssd_scan",
    )(xbc, z, dt, cw, cb, dtb, alog, dskip, gn)


def _out_mlp_kernel(x_ref, o1_ref, o2_ref, wo1_ref, wo2_ref, g_ref, w1_ref, w2_ref, y_ref,
                    xn_ref, acc_ref, *, tf):
    xnew = x_ref[...] + _dot(o1_ref[...], wo1_ref[...]) + _dot(o2_ref[...], wo2_ref[...])
    xn_ref[...] = _rms(xnew, g_ref[...]).astype(BF16)
    acc_ref[...] = xnew

    def body(c, carry):
        cs = pl.multiple_of(c * tf, tf)
        h = _dot(xn_ref[...], w1_ref[:, pl.ds(cs, tf)])
        a = jnp.square(jnp.maximum(h, 0.0)).astype(BF16)
        acc_ref[...] += _dot(a, w2_ref[pl.ds(cs, tf), :])
        return carry

    lax.fori_loop(0, w1_ref.shape[1] // tf, body, 0)
    y_ref[...] = acc_ref[...]


def _out_mlp(x, o1, o2, wo1, wo2, g, w1, w2):
    T, D = x.shape
    tm = TM_MLP
    tok = lambda c: pl.BlockSpec((tm, c), lambda i: (i, 0))
    return pl.pallas_call(
        functools.partial(_out_mlp_kernel, tf=TF_MLP),
        grid=(T // tm,),
        in_specs=[tok(D), tok(o1.shape[1]), tok(o2.shape[1]), _resident(wo1.shape),
                  _resident(wo2.shape), pl.BlockSpec((1, D), lambda i: (0, 0)),
                  _resident(w1.shape), _resident(w2.shape)],
        out_specs=tok(D),
        out_shape=jax.ShapeDtypeStruct((T, D), F32),
        scratch_shapes=[pltpu.VMEM((tm, D), BF16), pltpu.VMEM((tm, D), F32)],
        compiler_params=_cparams("arbitrary"),
        name="out_mlp",
    )(x, o1, o2, wo1, wo2, g, w1, w2)


def _pack_even_w(w):
    w = w.astype(BF16)
    qa, ka, va, qi, ki, wi, qb, kb, vb, fb = jnp.split(
        w, [512, 640, 768, 1280, 1344, 1352, 1864, 2376, 2888], axis=-1)
    pad = jnp.zeros(w.shape[:-1] + (LANES - IDX_DIM - N_IDX_HEADS - N_HEADS,), w.dtype)
    return jnp.concatenate([qa, ka, va, qi, qb, kb, vb, ki, wi, fb, pad], axis=-1)


def _pack_odd_w(w):
    w = w.astype(BF16)
    cq, ckv, kr, z, xbc, dt = jnp.split(w, [384, 640, 704, 1728, 3776], axis=-1)
    zpad = lambda n: jnp.zeros(w.shape[:-1] + (n,), w.dtype)
    return jnp.concatenate([cq, ckv, kr, zpad(LANES - QK_ROPE), dt, zpad(LANES - SSM_HEADS), z, xbc],
                           axis=-1)


def _pack_wuq(w):
    n_qk = QK_NOPE + QK_ROPE
    w = w.reshape(w.shape[0], N_HEADS, n_qk)
    w = jnp.pad(w, ((0, 0), (0, 0), (0, AUG - n_qk)))
    return w.reshape(w.shape[0], N_HEADS * AUG).astype(BF16)


def _row(v, width=None):
    v = v.astype(F32)[None, :]
    if width is not None and width > v.shape[1]:
        v = jnp.pad(v, ((0, 0), (0, width - v.shape[1])))
    return v


def kernel(x, positions, ev_norm, ev_w_in, ev_b_f, ev_qn_a, ev_kn_a, ev_qn_b, ev_kn_b, ev_w_out,
           od_norm, od_w_in, od_cq_norm, od_ckv_norm, od_w_uq, od_w_ukv, od_qn_c, od_kn_c,
           od_conv_w, od_conv_b, od_dt_bias, od_a_log, od_d_skip, od_gate_norm, od_w_out,
           mlp_norm, mlp_w1, mlp_w2):
    B, S, D = x.shape
    depth = mlp_w1.shape[0]
    ev_w, od_w = _pack_even_w(ev_w_in), _pack_odd_w(od_w_in)
    ev_wo, od_wo = ev_w_out.astype(BF16), od_w_out.astype(BF16)
    w1, w2 = mlp_w1.astype(BF16), mlp_w2.astype(BF16)
    cos = sin = None
    for layer in range(depth):
        i = layer // 2
        if layer % 2 == 0:
            bf = jnp.zeros((1, LANES), F32).at[0, MISC_F:MISC_F + N_HEADS].set(ev_b_f[i])
            qa, ka, va, qi, kd, wi, qb, kb, vb = _even_proj(
                x, _row(ev_norm[i]), ev_w[i], bf, _row(ev_qn_a[i]),
                _row(ev_kn_a[i]), _row(ev_qn_b[i]), _row(ev_kn_b[i]))
            o1 = _dsa(qa, ka, va, qi, kd, wi)
            o2 = _flash(qb, kb, vb)
            wo = ev_wo[i]
        else:
            if cos is None:
                cos, sin = _rope_tables(positions)
            q, k, v, z, xbc, dt = _odd_proj(
                x, _row(od_norm[i]), od_w[i], _row(od_cq_norm[i]),
                _row(od_ckv_norm[i]), _pack_wuq(od_w_uq[i]), od_w_ukv[i].astype(BF16),
                _row(od_qn_c[i], AUG), _row(od_kn_c[i], AUG), cos, sin)
            o1 = _flash(q, k, v)
            o2 = _ssd(xbc, z, dt, od_conv_w[i].astype(F32), _row(od_conv_b[i]),
                      _row(od_dt_bias[i], LANES), _row(od_a_log[i], LANES),
                      _row(jnp.repeat(od_d_skip[i], SSM_HEAD_DIM)), _row(od_gate_norm[i]))
            wo = od_wo[i]
        k1 = o1.shape[-1]
        x = _out_mlp(x.reshape(B * S, D), o1.reshape(B * S, k1), o2.reshape(B * S, -1),
                     wo[:k1], wo[k1:], _row(mlp_norm[layer]), w1[layer], w2[layer]).reshape(B, S, D)
    return x
```

```python
import functools

import jax
import jax.numpy as jnp
from jax import lax
from jax.experimental import pallas as pl
from jax.experimental.pallas import tpu as pltpu

F32 = jnp.float32
BF16 = jnp.bfloat16
EPS = 1e-6
NEG_INF = float("-inf")
F32_MAX = 3.4028234663852886e38
LOG2E = 1.4426950408889634

D_MODEL = 1024
HEAD_DIM = 128
N_HEADS = 4
N_IDX_HEADS = 8
IDX_DIM = 64
TOPK_MAX = 256
QK_NOPE = 128
QK_ROPE = 64
Q_LORA = 384
KV_LORA = 256
ROPE_THETA = 10000.0
SSM_INNER = 1024
SSM_HEAD_DIM = 64
SSM_HEADS = 16
SSM_GROUPS = 4
SSM_STATE = 128
CONV_WIDTH = 4
CONV_DIM = SSM_INNER + 2 * SSM_GROUPS * SSM_STATE
D_FF = 4 * D_MODEL

LANES = 128
AUG = 2 * LANES
VMEM_LIMIT = 56 * 1024 * 1024

TM_PROJ = 512
TM_MLP = 1024
TF_MLP = 512
TQ_FLASH = 1024
TK_FLASH = 1024
TQ_DSA = 512
TK_DSA = 512
DSA_HEAD_GROUP = 2
SSD_CHUNK = 128
BISECT_GROUP = 4
BISECT_ROUND = 24
BISECT_MAX_ROUNDS = 8

EV_QA, EV_KA, EV_VA, EV_QI, EV_QB, EV_KB, EV_VB, EV_MISC, EV_END = (
    0, 512, 640, 768, 1280, 1792, 2304, 2816, 2944)
MISC_W = 64
MISC_F = 72
OD_CQ, OD_CKV, OD_KR, OD_DT, OD_Z, OD_XBC, OD_END = (0, 384, 640, 768, 896, 1920, 3968)


def _cparams(*sem):
    return pltpu.CompilerParams(dimension_semantics=sem, vmem_limit_bytes=VMEM_LIMIT)


def _resident(shape):
    nd = len(shape)
    return pl.BlockSpec(shape, lambda *_: (0,) * nd, pipeline_mode=pl.Buffered(1))


def _rms(x, g, n=None):
    n = x.shape[-1] if n is None else n
    ss = jnp.sum(x * x, axis=-1, keepdims=True)
    return x * lax.rsqrt(ss * (1.0 / n) + EPS) * g


def _dot(a, b):
    return jnp.dot(a, b, preferred_element_type=F32)


def _dot_nt(a, b):
    return lax.dot_general(a, b, (((1,), (1,)), ((), ())), preferred_element_type=F32)


def _dot_hi(a, b):
    return jnp.dot(a, b, preferred_element_type=F32, precision=lax.Precision.HIGHEST)


def _sigmoid(x):
    return 1.0 / (1.0 + jnp.exp(-x))


def _softplus(x):
    return jnp.maximum(x, 0.0) + jnp.log(1.0 + jnp.exp(-jnp.abs(x)))


def _lane_iota(shape):
    return lax.broadcasted_iota(jnp.int32, shape, len(shape) - 1)


def _tri(n):
    r = lax.broadcasted_iota(jnp.int32, (n, n), 0)
    c = lax.broadcasted_iota(jnp.int32, (n, n), 1)
    return r >= c


def _split3(c):
    hi = c.astype(BF16).astype(F32)
    mid = (c - hi).astype(BF16).astype(F32)
    lo = (c - hi - mid).astype(BF16).astype(F32)
    return hi, mid, lo


def _even_proj_kernel(x_ref, g_ref, w_ref, bf_ref, qna_ref, kna_ref, qnb_ref, knb_ref,
                      qa_ref, ka_ref, va_ref, qi_ref, kd_ref, wi_ref, qb_ref, kb_ref, vb_ref,
                      carry_ref):
    tm = x_ref.shape[1]

    @pl.when(pl.program_id(1) == 0)
    def _():
        carry_ref[...] = jnp.zeros_like(carry_ref)

    xn = _rms(x_ref[0], g_ref[...]).astype(BF16)

    def proj(lo, hi):
        return _dot(xn, w_ref[:, lo:hi])

    scale = HEAD_DIM ** -0.5 * LOG2E
    lane = _lane_iota((tm, LANES))
    ones_col = jnp.where(lane == 0, 1.0, 0.0).astype(BF16)
    qa = proj(EV_QA, EV_KA)
    for h in range(N_HEADS):
        sl = slice(h * HEAD_DIM, (h + 1) * HEAD_DIM)
        qa_ref[0, :, sl] = (_rms(qa[:, sl], qna_ref[...]) * scale).astype(BF16)
    ka_ref[0] = _rms(proj(EV_KA, EV_VA), kna_ref[...]).astype(BF16)
    va_ref[0, :, :HEAD_DIM] = proj(EV_VA, EV_QI).astype(BF16)
    va_ref[0, :, HEAD_DIM:] = ones_col
    qi_ref[0] = proj(EV_QI, EV_QB).astype(BF16)
    vb = proj(EV_VB, EV_MISC)
    for h in range(N_HEADS):
        vb_ref[0, :, h * AUG:h * AUG + HEAD_DIM] = vb[:, h * HEAD_DIM:(h + 1) * HEAD_DIM].astype(BF16)
        vb_ref[0, :, h * AUG + HEAD_DIM:(h + 1) * AUG] = ones_col

    misc = proj(EV_MISC, EV_END)
    kd_ref[0] = jnp.where(lane < IDX_DIM, misc, pltpu.roll(misc, IDX_DIM, 1)).astype(BF16)
    wi_ref[0] = misc.T[MISC_W:MISC_W + N_IDX_HEADS, :]

    zf = misc + bf_ref[...]
    logf = jnp.minimum(zf, 0.0) - jnp.log(1.0 + jnp.exp(-jnp.abs(zf)))
    logf = jnp.where((lane >= MISC_F) & (lane < MISC_F + N_HEADS), logf, 0.0)
    tri = _tri(tm).astype(BF16)
    csum = sum(_dot(tri, part.astype(BF16)) for part in _split3(logf)) + carry_ref[...]
    carry_ref[...] = csum[tm - 1:tm, :]

    qb = proj(EV_QB, EV_KB)
    kb = proj(EV_KB, EV_VB)
    one = jnp.ones((tm, LANES), F32)
    zero = jnp.zeros((tm, LANES), F32)
    for h in range(N_HEADS):
        sl = slice(h * HEAD_DIM, (h + 1) * HEAD_DIM)
        c = jnp.broadcast_to(csum[:, MISC_F + h:MISC_F + h + 1], (tm, LANES)) * LOG2E
        hi, mid, lo = _split3(c)
        aq = jnp.where(lane == 0, hi, jnp.where(lane == 1, mid, jnp.where(lane == 2, lo,
             jnp.where(lane < 6, one, zero))))
        ak = jnp.where(lane < 3, one, jnp.where(lane == 3, -hi, jnp.where(lane == 4, -mid,
             jnp.where(lane == 5, -lo, zero))))
        qb_ref[0, :, h * AUG:h * AUG + LANES] = (_rms(qb[:, sl], qnb_ref[...]) * scale).astype(BF16)
        qb_ref[0, :, h * AUG + LANES:(h + 1) * AUG] = aq.astype(BF16)
        kb_ref[0, :, h * AUG:h * AUG + LANES] = _rms(kb[:, sl], knb_ref[...]).astype(BF16)
        kb_ref[0, :, h * AUG + LANES:(h + 1) * AUG] = ak.astype(BF16)


def _even_proj(x, g, w, bf, qna, kna, qnb, knb):
    B, S, D = x.shape
    tm = TM_PROJ
    tok = lambda c: pl.BlockSpec((1, tm, c), lambda b, j: (b, j, 0))
    row = lambda c: pl.BlockSpec((1, c), lambda b, j: (0, 0))
    out_cols = (512, 128, AUG, 512, 128, None, N_HEADS * AUG, N_HEADS * AUG, N_HEADS * AUG)
    out_specs = [tok(c) if c else pl.BlockSpec((1, N_IDX_HEADS, tm), lambda b, j: (b, 0, j))
                 for c in out_cols]
    out_shape = [jax.ShapeDtypeStruct((B, S, c), BF16) if c else
                 jax.ShapeDtypeStruct((B, N_IDX_HEADS, S), F32) for c in out_cols]
    return pl.pallas_call(
        _even_proj_kernel,
        grid=(B, S // tm),
        in_specs=[tok(D), row(D), _resident(w.shape), row(LANES), row(LANES), row(LANES),
                  row(LANES), row(LANES)],
        out_specs=out_specs,
        out_shape=out_shape,
        scratch_shapes=[pltpu.VMEM((1, LANES), F32)],
        compiler_params=_cparams("arbitrary", "arbitrary"),
        name="even_proj",
    )(x, g, w, bf, qna, kna, qnb, knb)


def _fold_max(mx, s):
    for g in range(s.shape[1] // LANES):
        mx = jnp.maximum(mx, s[:, g * LANES:(g + 1) * LANES])
    return mx


def _flash_kernel(q_ref, k_ref, v_ref, o_ref, lg_ref, acc_ref, *, tq, tk):
    i = pl.program_id(2)
    q = q_ref[0]
    ratio = tq // tk
    nfull = i * ratio

    def kchunk(ref, j):
        return ref[0, pl.ds(pl.multiple_of(j * tk, tk), tk), :]

    def pass_a(j, mx):
        s = _dot_nt(q, kchunk(k_ref, j))
        lg_ref[j] = s
        return _fold_max(mx, s)

    mx = lax.fori_loop(0, nfull, pass_a, jnp.full((tq, LANES), NEG_INF, F32))
    for d in range(ratio):
        j = nfull + d
        row = i * tq + lax.broadcasted_iota(jnp.int32, (tq, tk), 0)
        col = j * tk + lax.broadcasted_iota(jnp.int32, (tq, tk), 1)
        s = jnp.where(col <= row, _dot_nt(q, kchunk(k_ref, j)), NEG_INF)
        lg_ref[j] = s
        mx = _fold_max(mx, s)
    m = jnp.max(mx, axis=-1, keepdims=True)

    acc_ref[...] = jnp.zeros_like(acc_ref)

    def pass_b(j, c):
        p = jnp.exp2(lg_ref[j] - m).astype(BF16)
        acc_ref[...] += _dot(p, kchunk(v_ref, j))
        return c

    lax.fori_loop(0, nfull + ratio, pass_b, 0)
    acc = acc_ref[...]
    o_ref[0] = (acc[:, :HEAD_DIM] / acc[:, HEAD_DIM:HEAD_DIM + 1]).astype(o_ref.dtype)


def _flash(q, k, v):
    B, S, _ = q.shape
    tq, tk = min(TQ_FLASH, S), min(TK_FLASH, S)
    return pl.pallas_call(
        functools.partial(_flash_kernel, tq=tq, tk=tk),
        grid=(B, N_HEADS, S // tq),
        in_specs=[pl.BlockSpec((1, tq, AUG), lambda b, h, i: (b, i, h)),
                  pl.BlockSpec((1, S, AUG), lambda b, h, i: (b, 0, h)),
                  pl.BlockSpec((1, S, AUG), lambda b, h, i: (b, 0, h))],
        out_specs=pl.BlockSpec((1, tq, HEAD_DIM), lambda b, h, i: (b, i, h)),
        out_shape=jax.ShapeDtypeStruct((B, S, N_HEADS * HEAD_DIM), BF16),
        scratch_shapes=[pltpu.VMEM((S // tk, tq, tk), F32), pltpu.VMEM((tq, AUG), F32)],
        compiler_params=_cparams("arbitrary", "arbitrary", "arbitrary"),
        name="flash_attn",
    )(q, k, v)


def _dsa_kernel(qa_ref, ka_ref, va_ref, qi_ref, kd_ref, wi_ref, o_ref,
                sc_ref, qstk_ref, lg_ref, acc_ref, *, tkc, topk):
    tq = qa_ref.shape[1]
    i = pl.program_id(1)
    nkc = (i * tq + tq + tkc - 1) // tkc
    kf = float(topk)
    lane = _lane_iota((tq, LANES))
    keyi = lax.broadcasted_iota(jnp.int32, (tkc, tq), 0)
    qpos = i * tq + lax.broadcasted_iota(jnp.int32, (tkc, tq), 1)

    q8 = qi_ref[0]
    for j in range(N_IDX_HEADS):
        grp = q8[:, LANES * (j // 2):LANES * (j // 2 + 1)]
        keep = (lane < IDX_DIM) if j % 2 == 0 else (lane >= IDX_DIM)
        qstk_ref[tq * j:tq * (j + 1), :] = jnp.where(keep, grp, jnp.zeros_like(grp))
    wrows = [wi_ref[0, j:j + 1, :] for j in range(N_IDX_HEADS)]

    def chunk(ref, kc):
        return ref[0, pl.ds(pl.multiple_of(kc * tkc, tkc), tkc), :]

    def fold8(x, op):
        cols = []
        for g in range(x.shape[1] // LANES):
            xg = x[:, g * LANES:(g + 1) * LANES]
            rows = xg.shape[0]
            while rows > 8:
                rows //= 2
                xg = op(xg[:rows], xg[rows:])
            cols.append(xg)
        return jnp.concatenate(cols, axis=1)

    def p1(kc, carry):
        rmax, rmin = carry
        s8 = _dot_nt(chunk(kd_ref, kc), qstk_ref[...])
        score = wrows[0] * jnp.maximum(s8[:, 0:tq], 0.0)
        for j in range(1, N_IDX_HEADS):
            score = score + wrows[j] * jnp.maximum(s8[:, tq * j:tq * (j + 1)], 0.0)
        adm = (kc * tkc + keyi) <= qpos
        sc_ref[kc] = jnp.where(adm, score, NEG_INF)
        rmax = jnp.maximum(rmax, fold8(jnp.where(adm, score, NEG_INF), jnp.maximum))
        rmin = jnp.minimum(rmin, fold8(jnp.where(adm, score, -NEG_INF), jnp.minimum))
        return rmax, rmin

    rmax, rmin = lax.fori_loop(0, nkc, p1, (jnp.full((8, tq), NEG_INF, F32),
                                            jnp.full((8, tq), -NEG_INF, F32)))
    rmax = jnp.max(rmax, axis=0, keepdims=True)
    rmin = jnp.min(rmin, axis=0, keepdims=True)

    def row_counts(*preds):
        def cb(kc, cs):
            blk = sc_ref[kc]
            return tuple(c + fold8(jnp.where(p(blk, kc), 1.0, 0.0), jnp.add)
                         for c, p in zip(cs, preds))
        cs = lax.fori_loop(0, nkc, cb, tuple(jnp.zeros((8, tq), F32) for _ in preds))
        return tuple(jnp.sum(c, axis=0, keepdims=True) for c in cs)

    def row_count(pred):
        return row_counts(pred)[0]

    def any_row(flag):
        return jnp.max(flag.astype(jnp.int32))

    def select_threshold():
        hi0 = rmax + (jnp.abs(rmax) * 1.2e-7 + 1e-37)
        n_adm = (qpos[0:1, :] + 1).astype(F32)
        c_gt0, c_ge0 = row_counts(lambda blk, kc: blk > 0.0, lambda blk, kc: blk >= 0.0)
        above = c_ge0 >= kf
        pend0 = (above & (c_gt0 < kf)).astype(jnp.int32)
        lo0 = jnp.where(above, 0.0, rmin)
        cnt0 = jnp.where(above, c_ge0, n_adm)
        hi0 = jnp.where(above, hi0, 0.0)
        few = n_adm <= kf
        lo0 = jnp.where(few, -F32_MAX, lo0)
        cnt0 = jnp.where(few, kf, cnt0)
        pend0 = jnp.where(few, 0, pend0)

        def unresolved(cnt, pend):
            return any_row((cnt > kf + 1.0) & (pend == 0))

        def fold_two_min(x):
            c1, c2 = [], []
            for g in range(tq // LANES):
                xg = x[:, g * LANES:(g + 1) * LANES]
                rows = tkc // 2
                m1 = jnp.minimum(xg[:rows], xg[rows:])
                m2 = jnp.maximum(xg[:rows], xg[rows:])
                while rows > 8:
                    rows //= 2
                    a1, b1, a2, b2 = m1[:rows], m1[rows:], m2[:rows], m2[rows:]
                    m1, m2 = (jnp.minimum(a1, b1),
                              jnp.minimum(jnp.maximum(a1, b1), jnp.minimum(a2, b2)))
                c1.append(m1)
                c2.append(m2)
            return jnp.concatenate(c1, axis=1), jnp.concatenate(c2, axis=1)

        def snap(lo, cnt, pend):
            def two_min(kc, st):
                m1, m2 = st
                blk = sc_ref[kc]
                a1, a2 = fold_two_min(jnp.where(blk >= lo, blk, -NEG_INF))
                return (jnp.minimum(m1, a1),
                        jnp.minimum(jnp.maximum(m1, a1), jnp.minimum(m2, a2)))
            inf8 = jnp.full((8, tq), -NEG_INF, F32)
            m1, m2 = lax.fori_loop(0, nkc, two_min, (inf8, inf8))
            a1 = jnp.min(m1, axis=0, keepdims=True)
            at_min = m1 == a1
            others = jnp.minimum(jnp.min(jnp.where(at_min, -NEG_INF, m1), axis=0, keepdims=True),
                                 jnp.min(m2, axis=0, keepdims=True))
            n_min = jnp.sum(jnp.where(at_min, 1.0, 0.0), axis=0, keepdims=True)
            a2 = jnp.where(n_min > 1.0, a1, others)
            ok = (cnt == kf + 1.0) & (pend == 0) & (a2 > a1)
            return jnp.where(ok, a2, lo), jnp.where(ok, kf, cnt)

        def bis_cond(st):
            return (st[-1] > 0) & (st[0] < BISECT_ROUND)

        def bis_body(st):
            it, lo, hi, cnt, pend, _ = st
            for _ in range(BISECT_GROUP):
                mid = lo + (hi - lo) * 0.5
                c = row_count(lambda blk, kc, mid=mid: blk >= mid)
                up = c >= kf
                lo = jnp.where(up, mid, lo)
                hi = jnp.where(up, hi, mid)
                cnt = jnp.where(up, c, cnt)
            return it + BISECT_GROUP, lo, hi, cnt, pend, unresolved(cnt, pend)

        def resolve_ties(lo, cnt):
            def vmin(kc, v):
                blk = sc_ref[kc]
                return jnp.minimum(v, fold8(jnp.where(blk >= lo, blk, -NEG_INF), jnp.minimum))
            v = lax.fori_loop(0, nkc, vmin, jnp.full((8, tq), -NEG_INF, F32))
            v = jnp.min(v, axis=0, keepdims=True)
            cgt = row_count(lambda blk, kc: blk > v)
            tie = (cnt != kf) & (cgt < kf)
            need = kf - cgt
            tri = _tri(tkc).astype(BF16)

            def drop(kc, seen):
                blk = sc_ref[kc]
                eq = tie & (blk == v)
                rank = _dot(tri, jnp.where(eq, 1.0, 0.0).astype(BF16)) + seen
                sc_ref[kc] = jnp.where(eq & (rank > need), NEG_INF, blk)
                return rank[tkc - 1:tkc, :]
            lax.fori_loop(0, nkc, drop, jnp.zeros((1, tq), F32))
            return jnp.where(tie, v, lo), jnp.where(tie, kf, cnt)

        def round_cond(st):
            return (st[-1] > 0) & (st[0] < BISECT_MAX_ROUNDS)

        def round_body(st):
            r, lo, hi, cnt, pend, _ = st
            _, lo, hi, cnt, pend, _ = lax.while_loop(
                bis_cond, bis_body, (0, lo, hi, cnt, pend, unresolved(cnt, pend)))
            lo, cnt = lax.cond(any_row((cnt == kf + 1.0) & (pend == 0)) > 0, snap,
                               lambda a, b, c: (a, b), lo, cnt, pend)
            lo, cnt = lax.cond(any_row(cnt != kf) > 0, resolve_ties, lambda a, b: (a, b), lo, cnt)
            return r + 1, lo, hi, cnt, jnp.zeros_like(pend), any_row(cnt != kf)

        st = lax.while_loop(round_cond, round_body,
                            (0, lo0, hi0, cnt0, pend0, any_row(cnt0 != kf)))
        return st[1]

    thr = lax.cond((i + 1) * tq > topk, select_threshold,
                   lambda: jnp.full((1, tq), -F32_MAX, F32))

    qa = qa_ref[0]
    hg = DSA_HEAD_GROUP
    for g0 in range(0, N_HEADS, hg):
        qs = jnp.concatenate([qa[:, h * HEAD_DIM:(h + 1) * HEAD_DIM] for h in range(g0, g0 + hg)],
                             axis=0)

        def p3a(kc, mx, qs=qs, first=(g0 == 0)):
            s = _dot_nt(qs, chunk(ka_ref, kc))
            if first:
                sel = jnp.where(sc_ref[kc] >= thr, 0.0, NEG_INF).T
                sc_ref[kc] = sel
            else:
                sel = sc_ref[kc]
            parts = []
            for h in range(hg):
                rs = slice(h * tq, (h + 1) * tq)
                sh = s[rs] + sel
                lg_ref[kc, rs, :] = sh
                parts.append(_fold_max(mx[rs], sh))
            return jnp.concatenate(parts, axis=0)

        mx = lax.fori_loop(0, nkc, p3a, jnp.full((hg * tq, LANES), NEG_INF, F32))
        m = jnp.max(mx, axis=-1, keepdims=True)

        acc_ref[...] = jnp.zeros_like(acc_ref)

        def p3b(kc, c, m=m):
            p = jnp.exp2(lg_ref[kc] - m).astype(BF16)
            acc_ref[...] += _dot(p, chunk(va_ref, kc))
            return c

        lax.fori_loop(0, nkc, p3b, 0)
        for h in range(hg):
            acc = acc_ref[h * tq:(h + 1) * tq, :]
            o_ref[0, :, (g0 + h) * HEAD_DIM:(g0 + h + 1) * HEAD_DIM] = (
                acc[:, :HEAD_DIM] / acc[:, HEAD_DIM:HEAD_DIM + 1]).astype(o_ref.dtype)


def _dsa(qa, ka, va, qi, kd, wi):
    B, S, _ = qa.shape
    tq, tkc = TQ_DSA, min(TK_DSA, S)
    assert tq == tkc, "the transposed selection mask reuses the score scratch"
    topk = min(TOPK_MAX, S // 4)
    qblk = lambda c: pl.BlockSpec((1, tq, c), lambda b, i: (b, i, 0))
    full = lambda c: pl.BlockSpec((1, S, c), lambda b, i: (b, 0, 0))
    return pl.pallas_call(
        functools.partial(_dsa_kernel, tkc=tkc, topk=topk),
        grid=(B, S // tq),
        in_specs=[qblk(512), full(128), full(AUG), qblk(512), full(128),
                  pl.BlockSpec((1, N_IDX_HEADS, tq), lambda b, i: (b, 0, i))],
        out_specs=qblk(512),
        out_shape=jax.ShapeDtypeStruct((B, S, N_HEADS * HEAD_DIM), BF16),
        scratch_shapes=[pltpu.VMEM((S // tkc, tkc, tq), F32),
                        pltpu.VMEM((N_IDX_HEADS * tq, LANES), BF16),
                        pltpu.VMEM((S // tkc, DSA_HEAD_GROUP * tq, tkc), F32),
                        pltpu.VMEM((DSA_HEAD_GROUP * tq, AUG), F32)],
        compiler_params=_cparams("arbitrary", "arbitrary"),
        name="dsa_attn",
    )(qa, ka, va, qi, kd, wi)


def _rope_table_kernel(pos_ref, inv_ref, cos_ref, sin_ref):
    tm = pos_ref.shape[1]
    lane = _lane_iota((tm, LANES))
    ang = pos_ref[0].astype(F32) * inv_ref[...]
    half = QK_ROPE // 2
    cos_ref[0] = jnp.where(lane < QK_ROPE, jnp.cos(ang), 0.0)
    sn = jnp.sin(ang)
    sin_ref[0] = jnp.where(lane < half, -sn, jnp.where(lane < QK_ROPE, sn, 0.0))


def _rope_tables(positions):
    B, S = positions.shape
    tm = TM_PROJ
    half = QK_ROPE // 2
    inv = ROPE_THETA ** (-jnp.arange(half, dtype=F32) / half)
    inv = jnp.concatenate([inv, inv, jnp.zeros((LANES - QK_ROPE,), F32)])[None, :]
    return pl.pallas_call(
        _rope_table_kernel,
        grid=(B, S // tm),
        in_specs=[pl.BlockSpec((1, tm, 1), lambda b, j: (b, j, 0)),
                  pl.BlockSpec((1, LANES), lambda b, j: (0, 0))],
        out_specs=[pl.BlockSpec((1, tm, LANES), lambda b, j: (b, j, 0))] * 2,
        out_shape=[jax.ShapeDtypeStruct((B, S, LANES), F32)] * 2,
        compiler_params=_cparams("arbitrary", "arbitrary"),
        name="rope_tables",
    )(positions[:, :, None], inv)


def _rope(t, cos, sin):
    lane = _lane_iota(t.shape)
    half = QK_ROPE // 2
    swap = jnp.where(lane < half, pltpu.roll(t, LANES - half, 1), pltpu.roll(t, half, 1))
    return t * cos + swap * sin


def _odd_proj_kernel(x_ref, g_ref, w_ref, cqn_ref, ckvn_ref, wuq_ref, wukv_ref, qnc_ref, knc_ref,
                     cos_ref, sin_ref, q_ref, k_ref, v_ref, z_ref, xbc_ref, dt_ref):
    xn = _rms(x_ref[0], g_ref[...]).astype(BF16)

    def proj(lo, hi):
        return _dot(xn, w_ref[:, lo:hi])

    cos = cos_ref[0]
    sin = sin_ref[0]
    n_qk = QK_NOPE + QK_ROPE
    scale = n_qk ** -0.5 * LOG2E
    ones_col = jnp.where(_lane_iota(cos.shape) == 0, 1.0, 0.0).astype(BF16)

    cq = _rms(proj(OD_CQ, OD_CKV), cqn_ref[...]).astype(BF16)
    q = _dot(cq, wuq_ref[...])
    ckv = _rms(proj(OD_CKV, OD_KR), ckvn_ref[...]).astype(BF16)
    kv = _dot(ckv, wukv_ref[...])
    kr = proj(OD_KR, OD_DT)
    kr_ss = jnp.sum(kr * kr, axis=-1, keepdims=True)
    kr_rot = _rope(kr * knc_ref[:, LANES:], cos, sin)
    for h in range(N_HEADS):
        qh = q[:, h * AUG:(h + 1) * AUG]
        qh = _rms(qh, qnc_ref[...], n_qk) * scale
        q_ref[0, :, h * AUG:h * AUG + LANES] = qh[:, :LANES].astype(BF16)
        q_ref[0, :, h * AUG + LANES:(h + 1) * AUG] = _rope(qh[:, LANES:], cos, sin).astype(BF16)
        kn = kv[:, h * AUG:h * AUG + LANES]
        r = lax.rsqrt((jnp.sum(kn * kn, axis=-1, keepdims=True) + kr_ss) * (1.0 / n_qk) + EPS)
        k_ref[0, :, h * AUG:h * AUG + LANES] = (kn * r * knc_ref[:, :LANES]).astype(BF16)
        k_ref[0, :, h * AUG + LANES:(h + 1) * AUG] = (kr_rot * r).astype(BF16)
        v_ref[0, :, h * AUG:h * AUG + HEAD_DIM] = kv[:, h * AUG + LANES:(h + 1) * AUG].astype(BF16)
        v_ref[0, :, h * AUG + HEAD_DIM:(h + 1) * AUG] = ones_col

    dt_ref[0] = proj(OD_DT, OD_Z)
    z_ref[0] = proj(OD_Z, OD_XBC)
    xbc_ref[0] = proj(OD_XBC, OD_END)


def _odd_proj(x, g, w, cqn, ckvn, wuq, wukv, qnc, knc, cos, sin):
    B, S, D = x.shape
    tm = TM_PROJ
    tok = lambda c: pl.BlockSpec((1, tm, c), lambda b, j: (b, j, 0))
    row = lambda c: pl.BlockSpec((1, c), lambda b, j: (0, 0))
    out_cols = (N_HEADS * AUG, N_HEADS * AUG, N_HEADS * AUG, SSM_INNER, CONV_DIM, LANES)
    out_dt = (BF16, BF16, BF16, F32, F32, F32)
    return pl.pallas_call(
        _odd_proj_kernel,
        grid=(B, S // tm),
        in_specs=[tok(D), row(D), _resident(w.shape), row(Q_LORA), row(KV_LORA),
                  _resident(wuq.shape), _resident(wukv.shape), row(AUG), row(AUG),
                  tok(LANES), tok(LANES)],
        out_specs=[tok(c) for c in out_cols],
        out_shape=[jax.ShapeDtypeStruct((B, S, c), dt) for c, dt in zip(out_cols, out_dt)],
        compiler_params=_cparams("arbitrary", "arbitrary"),
        name="odd_proj",
    )(x, g, w, cqn, ckvn, wuq, wukv, qnc, knc, cos, sin)


def _ssd_kernel(xbc_ref, z_ref, dt_ref, cw_ref, cb_ref, dtb_ref, alog_ref, dskip_ref, gn_ref,
                y_ref, xe_ref, st_ref):
    q = xbc_ref.shape[1]
    halo = 8

    @pl.when(pl.program_id(1) == 0)
    def _():
        xe_ref[0:halo, :] = jnp.zeros((halo, CONV_DIM), F32)
        st_ref[...] = jnp.zeros_like(st_ref)

    xe_ref[halo:halo + q, :] = xbc_ref[0]
    conv = cb_ref[...]
    for w in range(CONV_WIDTH):
        off = halo - (CONV_WIDTH - 1) + w
        conv = conv + xe_ref[off:off + q, :] * cw_ref[w:w + 1, :]
    xe_ref[0:halo, :] = xe_ref[q:q + halo, :]
    act = conv * _sigmoid(conv)
    xs = act[:, :SSM_INNER]
    gs = SSM_STATE
    bm = [act[:, SSM_INNER + g * gs:SSM_INNER + (g + 1) * gs] for g in range(SSM_GROUPS)]
    cm = [act[:, SSM_INNER + (SSM_GROUPS + g) * gs:SSM_INNER + (SSM_GROUPS + g + 1) * gs]
          for g in range(SSM_GROUPS)]

    lane = _lane_iota((q, LANES))
    dt = _softplus(dt_ref[0] + dtb_ref[...])
    a = jnp.where(lane < SSM_HEADS, dt * (-jnp.exp(alog_ref[...])), 0.0)
    tri = _tri(q)
    trif = tri.astype(F32)
    acum = _dot_hi(trif, a)
    acum_t = lax.dot_general(a.T, trif, (((1,), (1,)), ((), ())), preferred_element_type=F32,
                             precision=lax.Precision.HIGHEST)
    last = acum[q - 1:q, :]
    dte = jnp.exp(last - acum)
    eac = jnp.exp(acum)
    cdec = jnp.exp(last)

    def expand(mat):
        rows = mat.shape[0]
        ln = _lane_iota((rows, LANES))
        parts = []
        for p in range(SSM_HEADS // 2):
            lo = jnp.broadcast_to(mat[:, 2 * p:2 * p + 1], (rows, LANES))
            hi = jnp.broadcast_to(mat[:, 2 * p + 1:2 * p + 2], (rows, LANES))
            parts.append(jnp.where(ln < SSM_HEAD_DIM, lo, hi))
        return jnp.concatenate(parts, axis=1)

    xd = xs * expand(dt)
    eac_x = expand(eac)
    xdd = (xd * expand(dte)).astype(BF16)
    xd16 = xd.astype(BF16)
    zero16 = jnp.zeros((q, LANES), BF16)

    rpg = SSM_HEADS // SSM_GROUPS
    gw = rpg * SSM_HEAD_DIM
    y_parts = []
    new_states = []
    for g in range(SSM_GROUPS):
        b16 = bm[g].astype(BF16)
        c16 = cm[g].astype(BF16)
        cb = _dot_nt(c16, b16)
        st_old = st_ref[:, g * gw:(g + 1) * gw]
        y_off = _dot(c16, st_old.astype(BF16)) * eac_x[:, g * gw:(g + 1) * gw]
        y_diag = []
        for pp in range(rpg // 2):
            p = g * (rpg // 2) + pp
            xp = xd16[:, p * LANES:(p + 1) * LANES]
            acc = None
            for e in range(2):
                h = 2 * p + e
                diff = acum[:, h:h + 1] - acum_t[h:h + 1, :]
                lm = jnp.exp(jnp.where(tri, diff, NEG_INF))
                gmat = (cb * lm).astype(BF16)
                keep = (lane < SSM_HEAD_DIM) if e == 0 else (lane >= SSM_HEAD_DIM)
                part = _dot(gmat, jnp.where(keep, xp, zero16))
                acc = part if acc is None else acc + part
            y_diag.append(acc)
        y_parts.append(jnp.concatenate(y_diag, axis=1) + y_off)
        st_new = _dot(bm[g].T.astype(BF16), xdd[:, g * gw:(g + 1) * gw])
        new_states.append(st_new)
    cdec_x = expand(cdec)
    for g in range(SSM_GROUPS):
        sl = slice(g * gw, (g + 1) * gw)
        st_ref[:, sl] = st_ref[:, sl] * cdec_x[:, sl] + new_states[g]

    y = jnp.concatenate(y_parts, axis=1) + dskip_ref[...] * xs
    zz = z_ref[0]
    y = y * (zz * _sigmoid(zz))
    for g in range(SSM_GROUPS):
        sl = slice(g * gw, (g + 1) * gw)
        y_ref[0, :, sl] = _rms(y[:, sl], gn_ref[:, sl]).astype(y_ref.dtype)


def _ssd(xbc, z, dt, cw, cb, dtb, alog, dskip, gn):
    B, S, _ = xbc.shape
    q = SSD_CHUNK
    tok = lambda c: pl.BlockSpec((1, q, c), lambda b, j: (b, j, 0))
    row = lambda r, c: pl.BlockSpec((r, c), lambda b, j: (0, 0))
    return pl.pallas_call(
        _ssd_kernel,
        grid=(B, S // q),
        in_specs=[tok(CONV_DIM), tok(SSM_INNER), tok(LANES), row(CONV_WIDTH, CONV_DIM),
                  row(1, CONV_DIM), row(1, LANES), row(1, LANES), row(1, SSM_INNER),
                  row(1, SSM_INNER)],
        out_specs=tok(SSM_INNER),
        out_shape=jax.ShapeDtypeStruct((B, S, SSM_INNER), BF16),
        scratch_shapes=[pltpu.VMEM((q + 8, CONV_DIM), F32), pltpu.VMEM((SSM_STATE, SSM_INNER), F32)],
        compiler_params=_cparams("arbitrary", "arbitrary"),
        name="ssd_scan",
    )(xbc, z, dt, cw, cb, dtb, alog, dskip, gn)


def _out_mlp_kernel(x_ref, o1_ref, o2_ref, wo1_ref, wo2_ref, g_ref, w1_ref, w2_ref, y_ref,
                    xn_ref, acc_ref, *, tf):
    xnew = x_ref[...] + _dot(o1_ref[...], wo1_ref[...]) + _dot(o2_ref[...], wo2_ref[...])
    xn_ref[...] = _rms(xnew, g_ref[...]).astype(BF16)
    acc_ref[...] = xnew

    def body(c, carry):
        cs = pl.multiple_of(c * tf, tf)
        h = _dot(xn_ref[...], w1_ref[:, pl.ds(cs, tf)])
        a = jnp.square(jnp.maximum(h, 0.0)).astype(BF16)
        acc_ref[...] += _dot(a, w2_ref[pl.ds(cs, tf), :])
        return carry

    lax.fori_loop(0, w1_ref.shape[1] // tf, body, 0)
    y_ref[...] = acc_ref[...]


def _out_mlp(x, o1, o2, wo1, wo2, g, w1, w2):
    T, D = x.shape
    tm = TM_MLP
    tok = lambda c: pl.BlockSpec((tm, c), lambda i: (i, 0))
    return pl.pallas_call(
        functools.partial(_out_mlp_kernel, tf=TF_MLP),
        grid=(T // tm,),
        in_specs=[tok(D), tok(o1.shape[1]), tok(o2.shape[1]), _resident(wo1.shape),
                  _resident(wo2.shape), pl.BlockSpec((1, D), lambda i: (0, 0)),
                  _resident(w1.shape), _resident(w2.shape)],
        out_specs=tok(D),
        out_shape=jax.ShapeDtypeStruct((T, D), F32),
        scratch_shapes=[pltpu.VMEM((tm, D), BF16), pltpu.VMEM((tm, D), F32)],
        compiler_params=_cparams("arbitrary"),
        name="out_mlp",
    )(x, o1, o2, wo1, wo2, g, w1, w2)


def _pack_even_w(w):
    w = w.astype(BF16)
    qa, ka, va, qi, ki, wi, qb, kb, vb, fb = jnp.split(
        w, [512, 640, 768, 1280, 1344, 1352, 1864, 2376, 2888], axis=-1)
    pad = jnp.zeros(w.shape[:-1] + (LANES - IDX_DIM - N_IDX_HEADS - N_HEADS,), w.dtype)
    return jnp.concatenate([qa, ka, va, qi, qb, kb, vb, ki, wi, fb, pad], axis=-1)


def _pack_odd_w(w):
    w = w.astype(BF16)
    cq, ckv, kr, z, xbc, dt = jnp.split(w, [384, 640, 704, 1728, 3776], axis=-1)
    zpad = lambda n: jnp.zeros(w.shape[:-1] + (n,), w.dtype)
    return jnp.concatenate([cq, ckv, kr, zpad(LANES - QK_ROPE), dt, zpad(LANES - SSM_HEADS), z, xbc],
                           axis=-1)


def _pack_wuq(w):
    n_qk = QK_NOPE + QK_ROPE
    w = w.reshape(w.shape[0], N_HEADS, n_qk)
    w = jnp.pad(w, ((0, 0), (0, 0), (0, AUG - n_qk)))
    return w.reshape(w.shape[0], N_HEADS * AUG).astype(BF16)


def _row(v, width=None):
    v = v.astype(F32)[None, :]
    if width is not None and width > v.shape[1]:
        v = jnp.pad(v, ((0, 0), (0, width - v.shape[1])))
    return v


def kernel(x, positions, ev_norm, ev_w_in, ev_b_f, ev_qn_a, ev_kn_a, ev_qn_b, ev_kn_b, ev_w_out,
           od_norm, od_w_in, od_cq_norm, od_ckv_norm, od_w_uq, od_w_ukv, od_qn_c, od_kn_c,
           od_conv_w, od_conv_b, od_dt_bias, od_a_log, od_d_skip, od_gate_norm, od_w_out,
           mlp_norm, mlp_w1, mlp_w2):
    B, S, D = x.shape
    depth = mlp_w1.shape[0]
    ev_w, od_w = _pack_even_w(ev_w_in), _pack_odd_w(od_w_in)
    ev_wo, od_wo = ev_w_out.astype(BF16), od_w_out.astype(BF16)
    w1, w2 = mlp_w1.astype(BF16), mlp_w2.astype(BF16)
    cos = sin = None
    for layer in range(depth):
        i = layer // 2
        if layer % 2 == 0:
            bf = jnp.zeros((1, LANES), F32).at[0, MISC_F:MISC_F + N_HEADS].set(ev_b_f[i])
            qa, ka, va, qi, kd, wi, qb, kb, vb = _even_proj(
                x, _row(ev_norm[i]), ev_w[i], bf, _row(ev_qn_a[i]),
                _row(ev_kn_a[i]), _row(ev_qn_b[i]), _row(ev_kn_b[i]))
            o1 = _dsa(qa, ka, va, qi, kd, wi)
            o2 = _flash(qb, kb, vb)
            wo = ev_wo[i]
        else:
            if cos is None:
                cos, sin = _rope_tables(positions)
            q, k, v, z, xbc, dt = _odd_proj(
                x, _row(od_norm[i]), od_w[i], _row(od_cq_norm[i]),
                _row(od_ckv_norm[i]), _pack_wuq(od_w_uq[i]), od_w_ukv[i].astype(BF16),
                _row(od_qn_c[i], AUG), _row(od_kn_c[i], AUG), cos, sin)
            o1 = _flash(q, k, v)
            o2 = _ssd(xbc, z, dt, od_conv_w[i].astype(F32), _row(od_conv_b[i]),
                      _row(od_dt_bias[i], LANES), _row(od_a_log[i], LANES),
                      _row(jnp.repeat(od_d_skip[i], SSM_HEAD_DIM)), _row(od_gate_norm[i]))
            wo = od_wo[i]
        k1 = o1.shape[-1]
        x = _out_mlp(x.reshape(B * S, D), o1.reshape(B * S, k1), o2.reshape(B * S, -1),
                     wo[:k1], wo[k1:], _row(mlp_norm[layer]), w1[layer], w2[layer]).reshape(B, S, D)
    return x
```

```python
import functools

import jax
import jax.numpy as jnp
from jax import lax
from jax.experimental import pallas as pl
from jax.experimental.pallas import tpu as pltpu

F32 = jnp.float32
BF16 = jnp.bfloat16
EPS = 1e-6
NEG_INF = float("-inf")
F32_MAX = 3.4028234663852886e38
LOG2E = 1.4426950408889634

D_MODEL = 1024
HEAD_DIM = 128
N_HEADS = 4
N_IDX_HEADS = 8
IDX_DIM = 64
TOPK_MAX = 256
QK_NOPE = 128
QK_ROPE = 64
Q_LORA = 384
KV_LORA = 256
ROPE_THETA = 10000.0
SSM_INNER = 1024
SSM_HEAD_DIM = 64
SSM_HEADS = 16
SSM_GROUPS = 4
SSM_STATE = 128
CONV_WIDTH = 4
CONV_DIM = SSM_INNER + 2 * SSM_GROUPS * SSM_STATE
D_FF = 4 * D_MODEL

LANES = 128
AUG = 2 * LANES
VMEM_LIMIT = 56 * 1024 * 1024

TM_PROJ = 512
TM_EVEN = 1024
CUMSUM_BLOCK = 512
TM_MLP = 1024
TF_MLP = 512
TQ_FLASH = 1024
TK_FLASH = 1024
TQ_DSA = 512
TK_DSA = 512
DSA_HEAD_GROUP = 2
SSD_CHUNK = 128
BISECT_GROUP = 4
BISECT_ROUND = 24
BISECT_MAX_ROUNDS = 8

EV_QA, EV_KA, EV_VA, EV_QI, EV_QB, EV_KB, EV_VB, EV_MISC, EV_END = (
    0, 512, 640, 768, 1280, 1792, 2304, 2816, 2944)
MISC_W = 64
MISC_F = 72
OD_CQ, OD_CKV, OD_KR, OD_DT, OD_Z, OD_XBC, OD_END = (0, 384, 640, 768, 896, 1920, 3968)


def _cparams(*sem):
    return pltpu.CompilerParams(dimension_semantics=sem, vmem_limit_bytes=VMEM_LIMIT)


def _resident(shape):
    nd = len(shape)
    return pl.BlockSpec(shape, lambda *_: (0,) * nd, pipeline_mode=pl.Buffered(1))


def _rms(x, g, n=None):
    n = x.shape[-1] if n is None else n
    ss = jnp.sum(x * x, axis=-1, keepdims=True)
    return x * lax.rsqrt(ss * (1.0 / n) + EPS) * g


def _dot(a, b):
    return jnp.dot(a, b, preferred_element_type=F32)


def _dot_nt(a, b):
    return lax.dot_general(a, b, (((1,), (1,)), ((), ())), preferred_element_type=F32)


def _dot_hi(a, b):
    return jnp.dot(a, b, preferred_element_type=F32, precision=lax.Precision.HIGHEST)


def _sigmoid(x):
    return 1.0 / (1.0 + jnp.exp(-x))


def _softplus(x):
    return jnp.maximum(x, 0.0) + jnp.log(1.0 + jnp.exp(-jnp.abs(x)))


def _lane_iota(shape):
    return lax.broadcasted_iota(jnp.int32, shape, len(shape) - 1)


def _tri(n):
    r = lax.broadcasted_iota(jnp.int32, (n, n), 0)
    c = lax.broadcasted_iota(jnp.int32, (n, n), 1)
    return r >= c


def _split3(c):
    hi = c.astype(BF16).astype(F32)
    mid = (c - hi).astype(BF16).astype(F32)
    lo = (c - hi - mid).astype(BF16).astype(F32)
    return hi, mid, lo


def _even_proj_kernel(x_ref, g_ref, w_ref, bf_ref, qna_ref, kna_ref, qnb_ref, knb_ref,
                      qa_ref, ka_ref, va_ref, qi_ref, kd_ref, wi_ref, qb_ref, kb_ref, vb_ref,
                      carry_ref):
    tm = x_ref.shape[1]

    @pl.when(pl.program_id(1) == 0)
    def _():
        carry_ref[...] = jnp.zeros_like(carry_ref)

    xn = _rms(x_ref[0], g_ref[...]).astype(BF16)

    def proj(lo, hi):
        return _dot(xn, w_ref[:, lo:hi])

    scale = HEAD_DIM ** -0.5 * LOG2E
    lane = _lane_iota((tm, LANES))
    ones_col = jnp.where(lane == 0, 1.0, 0.0).astype(BF16)
    qa = proj(EV_QA, EV_KA)
    for h in range(N_HEADS):
        sl = slice(h * HEAD_DIM, (h + 1) * HEAD_DIM)
        qa_ref[0, :, sl] = (_rms(qa[:, sl], qna_ref[...]) * scale).astype(BF16)
    ka_ref[0] = _rms(proj(EV_KA, EV_VA), kna_ref[...]).astype(BF16)
    va_ref[0, :, :HEAD_DIM] = proj(EV_VA, EV_QI).astype(BF16)
    va_ref[0, :, HEAD_DIM:] = ones_col
    qi_ref[0] = proj(EV_QI, EV_QB).astype(BF16)
    vb = proj(EV_VB, EV_MISC)
    for h in range(N_HEADS):
        vb_ref[0, :, h * AUG:h * AUG + HEAD_DIM] = vb[:, h * HEAD_DIM:(h + 1) * HEAD_DIM].astype(BF16)
        vb_ref[0, :, h * AUG + HEAD_DIM:(h + 1) * AUG] = ones_col

    misc = proj(EV_MISC, EV_END)
    kd_ref[0] = jnp.where(lane < IDX_DIM, misc, pltpu.roll(misc, IDX_DIM, 1)).astype(BF16)
    wi_ref[0] = misc.T[MISC_W:MISC_W + N_IDX_HEADS, :]

    zf = misc + bf_ref[...]
    logf = jnp.minimum(zf, 0.0) - jnp.log(1.0 + jnp.exp(-jnp.abs(zf)))
    logf = jnp.where((lane >= MISC_F) & (lane < MISC_F + N_HEADS), logf, 0.0)
    cb = min(tm, CUMSUM_BLOCK)
    tri = _tri(cb).astype(BF16)
    carry = carry_ref[...]
    blocks = []
    for r in range(tm // cb):
        parts = _split3(logf[r * cb:(r + 1) * cb])
        blocks.append(sum(_dot(tri, part.astype(BF16)) for part in parts) + carry)
        carry = blocks[-1][cb - 1:cb, :]
    csum = jnp.concatenate(blocks, axis=0)
    carry_ref[...] = carry

    qb = proj(EV_QB, EV_KB)
    kb = proj(EV_KB, EV_VB)
    one = jnp.ones((tm, LANES), F32)
    zero = jnp.zeros((tm, LANES), F32)
    for h in range(N_HEADS):
        sl = slice(h * HEAD_DIM, (h + 1) * HEAD_DIM)
        c = jnp.broadcast_to(csum[:, MISC_F + h:MISC_F + h + 1], (tm, LANES)) * LOG2E
        hi, mid, lo = _split3(c)
        aq = jnp.where(lane == 0, hi, jnp.where(lane == 1, mid, jnp.where(lane == 2, lo,
             jnp.where(lane < 6, one, zero))))
        ak = jnp.where(lane < 3, one, jnp.where(lane == 3, -hi, jnp.where(lane == 4, -mid,
             jnp.where(lane == 5, -lo, zero))))
        qb_ref[0, :, h * AUG:h * AUG + LANES] = (_rms(qb[:, sl], qnb_ref[...]) * scale).astype(BF16)
        qb_ref[0, :, h * AUG + LANES:(h + 1) * AUG] = aq.astype(BF16)
        kb_ref[0, :, h * AUG:h * AUG + LANES] = _rms(kb[:, sl], knb_ref[...]).astype(BF16)
        kb_ref[0, :, h * AUG + LANES:(h + 1) * AUG] = ak.astype(BF16)


def _even_proj(x, g, w, bf, qna, kna, qnb, knb):
    B, S, D = x.shape
    tm = min(TM_EVEN, S)
    tok = lambda c: pl.BlockSpec((1, tm, c), lambda b, j: (b, j, 0))
    row = lambda c: pl.BlockSpec((1, c), lambda b, j: (0, 0))
    out_cols = (512, 128, AUG, 512, 128, None, N_HEADS * AUG, N_HEADS * AUG, N_HEADS * AUG)
    out_specs = [tok(c) if c else pl.BlockSpec((1, N_IDX_HEADS, tm), lambda b, j: (b, 0, j))
                 for c in out_cols]
    out_shape = [jax.ShapeDtypeStruct((B, S, c), BF16) if c else
                 jax.ShapeDtypeStruct((B, N_IDX_HEADS, S), F32) for c in out_cols]
    return pl.pallas_call(
        _even_proj_kernel,
        grid=(B, S // tm),
        in_specs=[tok(D), row(D), _resident(w.shape), row(LANES), row(LANES), row(LANES),
                  row(LANES), row(LANES)],
        out_specs=out_specs,
        out_shape=out_shape,
        scratch_shapes=[pltpu.VMEM((1, LANES), F32)],
        compiler_params=_cparams("arbitrary", "arbitrary"),
        name="even_proj",
    )(x, g, w, bf, qna, kna, qnb, knb)


def _fold_max(mx, s):
    for g in range(s.shape[1] // LANES):
        mx = jnp.maximum(mx, s[:, g * LANES:(g + 1) * LANES])
    return mx


def _flash_kernel(q_ref, k_ref, v_ref, o_ref, lg_ref, acc_ref, *, tq, tk):
    i = pl.program_id(2)
    q = q_ref[0]
    half = tq // 2

    def kchunk(ref, j):
        return ref[0, pl.ds(pl.multiple_of(j * tk, tk), tk), :]

    def pass_a(j, mx):
        s = _dot_nt(q, kchunk(k_ref, j))
        lg_ref[j] = s
        return _fold_max(mx, s)

    mx = lax.fori_loop(0, i, pass_a, jnp.full((tq, LANES), NEG_INF, F32))
    kd = kchunk(k_ref, i)
    tri = _tri(half)
    s_top = jnp.where(tri, _dot_nt(q[:half], kd[:half]), NEG_INF)
    s_bot = _dot_nt(q[half:], kd)
    s_bot = jnp.concatenate([s_bot[:, :half], jnp.where(tri, s_bot[:, half:], NEG_INF)], axis=1)
    lg_ref[i, :half, :half] = s_top
    lg_ref[i, half:, :] = s_bot
    mx = jnp.concatenate([_fold_max(mx[:half], s_top), _fold_max(mx[half:], s_bot)], axis=0)
    m = jnp.max(mx, axis=-1, keepdims=True)

    acc_ref[...] = jnp.zeros_like(acc_ref)

    def pass_b(j, c):
        p = jnp.exp2(lg_ref[j] - m).astype(BF16)
        acc_ref[...] += _dot(p, kchunk(v_ref, j))
        return c

    lax.fori_loop(0, i, pass_b, 0)
    vd = kchunk(v_ref, i)
    p_top = jnp.exp2(lg_ref[i, :half, :half] - m[:half]).astype(BF16)
    p_bot = jnp.exp2(lg_ref[i, half:, :] - m[half:]).astype(BF16)
    acc_top = acc_ref[:half, :] + _dot(p_top, vd[:half])
    acc_bot = acc_ref[half:, :] + _dot(p_bot, vd)
    o_ref[0, :half, :] = (acc_top[:, :HEAD_DIM] / acc_top[:, HEAD_DIM:HEAD_DIM + 1]).astype(o_ref.dtype)
    o_ref[0, half:, :] = (acc_bot[:, :HEAD_DIM] / acc_bot[:, HEAD_DIM:HEAD_DIM + 1]).astype(o_ref.dtype)


def _flash(q, k, v):
    B, S, _ = q.shape
    tq, tk = min(TQ_FLASH, S), min(TK_FLASH, S)
    assert tq == tk, "the diagonal chunk is split on the query block's own key chunk"
    return pl.pallas_call(
        functools.partial(_flash_kernel, tq=tq, tk=tk),
        grid=(B, N_HEADS, S // tq),
        in_specs=[pl.BlockSpec((1, tq, AUG), lambda b, h, i: (b, i, h)),
                  pl.BlockSpec((1, S, AUG), lambda b, h, i: (b, 0, h)),
                  pl.BlockSpec((1, S, AUG), lambda b, h, i: (b, 0, h))],
        out_specs=pl.BlockSpec((1, tq, HEAD_DIM), lambda b, h, i: (b, i, h)),
        out_shape=jax.ShapeDtypeStruct((B, S, N_HEADS * HEAD_DIM), BF16),
        scratch_shapes=[pltpu.VMEM((S // tk, tq, tk), F32), pltpu.VMEM((tq, AUG), F32)],
        compiler_params=_cparams("arbitrary", "arbitrary", "arbitrary"),
        name="flash_attn",
    )(q, k, v)


def _dsa_kernel(qa_ref, ka_ref, va_ref, qi_ref, kd_ref, wi_ref, o_ref,
                sc_ref, qstk_ref, lg_ref, acc_ref, *, tkc, topk):
    tq = qa_ref.shape[1]
    i = pl.program_id(1)
    nkc = (i * tq + tq + tkc - 1) // tkc
    kf = float(topk)
    lane = _lane_iota((tq, LANES))
    keyi = lax.broadcasted_iota(jnp.int32, (tkc, tq), 0)
    qpos = i * tq + lax.broadcasted_iota(jnp.int32, (tkc, tq), 1)

    q8 = qi_ref[0]
    for j in range(N_IDX_HEADS):
        grp = q8[:, LANES * (j // 2):LANES * (j // 2 + 1)]
        keep = (lane < IDX_DIM) if j % 2 == 0 else (lane >= IDX_DIM)
        qstk_ref[tq * j:tq * (j + 1), :] = jnp.where(keep, grp, jnp.zeros_like(grp))
    wrows = [wi_ref[0, j:j + 1, :] for j in range(N_IDX_HEADS)]

    def chunk(ref, kc):
        return ref[0, pl.ds(pl.multiple_of(kc * tkc, tkc), tkc), :]

    def fold8(x, op):
        cols = []
        for g in range(x.shape[1] // LANES):
            xg = x[:, g * LANES:(g + 1) * LANES]
            rows = xg.shape[0]
            while rows > 8:
                rows //= 2
                xg = op(xg[:rows], xg[rows:])
            cols.append(xg)
        return jnp.concatenate(cols, axis=1)

    def p1(kc, carry):
        rmax, rmin = carry
        s8 = _dot_nt(chunk(kd_ref, kc), qstk_ref[...])
        score = wrows[0] * jnp.maximum(s8[:, 0:tq], 0.0)
        for j in range(1, N_IDX_HEADS):
            score = score + wrows[j] * jnp.maximum(s8[:, tq * j:tq * (j + 1)], 0.0)
        adm = (kc * tkc + keyi) <= qpos
        sc_ref[kc] = jnp.where(adm, score, NEG_INF)
        rmax = jnp.maximum(rmax, fold8(jnp.where(adm, score, NEG_INF), jnp.maximum))
        rmin = jnp.minimum(rmin, fold8(jnp.where(adm, score, -NEG_INF), jnp.minimum))
        return rmax, rmin

    rmax, rmin = lax.fori_loop(0, nkc, p1, (jnp.full((8, tq), NEG_INF, F32),
                                            jnp.full((8, tq), -NEG_INF, F32)))
    rmax = jnp.max(rmax, axis=0, keepdims=True)
    rmin = jnp.min(rmin, axis=0, keepdims=True)

    def row_counts(*preds):
        def cb(kc, cs):
            blk = sc_ref[kc]
            return tuple(c + fold8(jnp.where(p(blk, kc), 1.0, 0.0), jnp.add)
                         for c, p in zip(cs, preds))
        cs = lax.fori_loop(0, nkc, cb, tuple(jnp.zeros((8, tq), F32) for _ in preds))
        return tuple(jnp.sum(c, axis=0, keepdims=True) for c in cs)

    def row_count(pred):
        return row_counts(pred)[0]

    def any_row(flag):
        return jnp.max(flag.astype(jnp.int32))

    def select_threshold():
        hi0 = rmax + (jnp.abs(rmax) * 1.2e-7 + 1e-37)
        n_adm = (qpos[0:1, :] + 1).astype(F32)
        c_gt0, c_ge0 = row_counts(lambda blk, kc: blk > 0.0, lambda blk, kc: blk >= 0.0)
        above = c_ge0 >= kf
        pend0 = (above & (c_gt0 < kf)).astype(jnp.int32)
        lo0 = jnp.where(above, 0.0, rmin)
        cnt0 = jnp.where(above, c_ge0, n_adm)
        hi0 = jnp.where(above, hi0, 0.0)
        few = n_adm <= kf
        lo0 = jnp.where(few, -F32_MAX, lo0)
        cnt0 = jnp.where(few, kf, cnt0)
        pend0 = jnp.where(few, 0, pend0)

        def unresolved(cnt, pend):
            return any_row((cnt > kf + 1.0) & (pend == 0))

        def fold_two_min(x):
            c1, c2 = [], []
            for g in range(tq // LANES):
                xg = x[:, g * LANES:(g + 1) * LANES]
                rows = tkc // 2
                m1 = jnp.minimum(xg[:rows], xg[rows:])
                m2 = jnp.maximum(xg[:rows], xg[rows:])
                while rows > 8:
                    rows //= 2
                    a1, b1, a2, b2 = m1[:rows], m1[rows:], m2[:rows], m2[rows:]
                    m1, m2 = (jnp.minimum(a1, b1),
                              jnp.minimum(jnp.maximum(a1, b1), jnp.minimum(a2, b2)))
                c1.append(m1)
                c2.append(m2)
            return jnp.concatenate(c1, axis=1), jnp.concatenate(c2, axis=1)

        def snap(lo, cnt, pend):
            def two_min(kc, st):
                m1, m2 = st
                blk = sc_ref[kc]
                a1, a2 = fold_two_min(jnp.where(blk >= lo, blk, -NEG_INF))
                return (jnp.minimum(m1, a1),
                        jnp.minimum(jnp.maximum(m1, a1), jnp.minimum(m2, a2)))
            inf8 = jnp.full((8, tq), -NEG_INF, F32)
            m1, m2 = lax.fori_loop(0, nkc, two_min, (inf8, inf8))
            a1 = jnp.min(m1, axis=0, keepdims=True)
            at_min = m1 == a1
            others = jnp.minimum(jnp.min(jnp.where(at_min, -NEG_INF, m1), axis=0, keepdims=True),
                                 jnp.min(m2, axis=0, keepdims=True))
            n_min = jnp.sum(jnp.where(at_min, 1.0, 0.0), axis=0, keepdims=True)
            a2 = jnp.where(n_min > 1.0, a1, others)
            ok = (cnt == kf + 1.0) & (pend == 0) & (a2 > a1)
            return jnp.where(ok, a2, lo), jnp.where(ok, kf, cnt)

        def bis_cond(st):
            return (st[-1] > 0) & (st[0] < BISECT_ROUND)

        def bis_body(st):
            it, lo, hi, cnt, pend, _ = st
            for _ in range(BISECT_GROUP):
                mid = lo + (hi - lo) * 0.5
                c = row_count(lambda blk, kc, mid=mid: blk >= mid)
                up = c >= kf
                lo = jnp.where(up, mid, lo)
                hi = jnp.where(up, hi, mid)
                cnt = jnp.where(up, c, cnt)
            return it + BISECT_GROUP, lo, hi, cnt, pend, unresolved(cnt, pend)

        def resolve_ties(lo, cnt):
            def vmin(kc, v):
                blk = sc_ref[kc]
                return jnp.minimum(v, fold8(jnp.where(blk >= lo, blk, -NEG_INF), jnp.minimum))
            v = lax.fori_loop(0, nkc, vmin, jnp.full((8, tq), -NEG_INF, F32))
            v = jnp.min(v, axis=0, keepdims=True)
            cgt = row_count(lambda blk, kc: blk > v)
            tie = (cnt != kf) & (cgt < kf)
            need = kf - cgt
            tri = _tri(tkc).astype(BF16)

            def drop(kc, seen):
                blk = sc_ref[kc]
                eq = tie & (blk == v)
                rank = _dot(tri, jnp.where(eq, 1.0, 0.0).astype(BF16)) + seen
                sc_ref[kc] = jnp.where(eq & (rank > need), NEG_INF, blk)
                return rank[tkc - 1:tkc, :]
            lax.fori_loop(0, nkc, drop, jnp.zeros((1, tq), F32))
            return jnp.where(tie, v, lo), jnp.where(tie, kf, cnt)

        def round_cond(st):
            return (st[-1] > 0) & (st[0] < BISECT_MAX_ROUNDS)

        def round_body(st):
            r, lo, hi, cnt, pend, _ = st
            _, lo, hi, cnt, pend, _ = lax.while_loop(
                bis_cond, bis_body, (0, lo, hi, cnt, pend, unresolved(cnt, pend)))
            lo, cnt = lax.cond(any_row((cnt == kf + 1.0) & (pend == 0)) > 0, snap,
                               lambda a, b, c: (a, b), lo, cnt, pend)
            lo, cnt = lax.cond(any_row(cnt != kf) > 0, resolve_ties, lambda a, b: (a, b), lo, cnt)
            return r + 1, lo, hi, cnt, jnp.zeros_like(pend), any_row(cnt != kf)

        st = lax.while_loop(round_cond, round_body,
                            (0, lo0, hi0, cnt0, pend0, any_row(cnt0 != kf)))
        return st[1]

    thr = lax.cond((i + 1) * tq > topk, select_threshold,
                   lambda: jnp.full((1, tq), -F32_MAX, F32))

    qa = qa_ref[0]
    hg = DSA_HEAD_GROUP
    for g0 in range(0, N_HEADS, hg):
        qs = jnp.concatenate([qa[:, h * HEAD_DIM:(h + 1) * HEAD_DIM] for h in range(g0, g0 + hg)],
                             axis=0)

        def p3a(kc, mx, qs=qs, first=(g0 == 0)):
            s = _dot_nt(qs, chunk(ka_ref, kc))
            if first:
                sel = jnp.where(sc_ref[kc] >= thr, 0.0, NEG_INF).T
                sc_ref[kc] = sel
            else:
                sel = sc_ref[kc]
            parts = []
            for h in range(hg):
                rs = slice(h * tq, (h + 1) * tq)
                sh = s[rs] + sel
                lg_ref[kc, rs, :] = sh
                parts.append(_fold_max(mx[rs], sh))
            return jnp.concatenate(parts, axis=0)

        mx = lax.fori_loop(0, nkc, p3a, jnp.full((hg * tq, LANES), NEG_INF, F32))
        m = jnp.max(mx, axis=-1, keepdims=True)

        acc_ref[...] = jnp.zeros_like(acc_ref)

        def p3b(kc, c, m=m):
            p = jnp.exp2(lg_ref[kc] - m).astype(BF16)
            acc_ref[...] += _dot(p, chunk(va_ref, kc))
            return c

        lax.fori_loop(0, nkc, p3b, 0)
        for h in range(hg):
            acc = acc_ref[h * tq:(h + 1) * tq, :]
            o_ref[0, :, (g0 + h) * HEAD_DIM:(g0 + h + 1) * HEAD_DIM] = (
                acc[:, :HEAD_DIM] / acc[:, HEAD_DIM:HEAD_DIM + 1]).astype(o_ref.dtype)


def _dsa(qa, ka, va, qi, kd, wi):
    B, S, _ = qa.shape
    tq, tkc = TQ_DSA, min(TK_DSA, S)
    assert tq == tkc, "the transposed selection mask reuses the score scratch"
    topk = min(TOPK_MAX, S // 4)
    qblk = lambda c: pl.BlockSpec((1, tq, c), lambda b, i: (b, i, 0))
    full = lambda c: pl.BlockSpec((1, S, c), lambda b, i: (b, 0, 0))
    return pl.pallas_call(
        functools.partial(_dsa_kernel, tkc=tkc, topk=topk),
        grid=(B, S // tq),
        in_specs=[qblk(512), full(128), full(AUG), qblk(512), full(128),
                  pl.BlockSpec((1, N_IDX_HEADS, tq), lambda b, i: (b, 0, i))],
        out_specs=qblk(512),
        out_shape=jax.ShapeDtypeStruct((B, S, N_HEADS * HEAD_DIM), BF16),
        scratch_shapes=[pltpu.VMEM((S // tkc, tkc, tq), F32),
                        pltpu.VMEM((N_IDX_HEADS * tq, LANES), BF16),
                        pltpu.VMEM((S // tkc, DSA_HEAD_GROUP * tq, tkc), F32),
                        pltpu.VMEM((DSA_HEAD_GROUP * tq, AUG), F32)],
        compiler_params=_cparams("arbitrary", "arbitrary"),
        name="dsa_attn",
    )(qa, ka, va, qi, kd, wi)


def _rope_table_kernel(pos_ref, inv_ref, cos_ref, sin_ref):
    tm = pos_ref.shape[1]
    lane = _lane_iota((tm, LANES))
    ang = pos_ref[0].astype(F32) * inv_ref[...]
    half = QK_ROPE // 2
    cos_ref[0] = jnp.where(lane < QK_ROPE, jnp.cos(ang), 0.0)
    sn = jnp.sin(ang)
    sin_ref[0] = jnp.where(lane < half, -sn, jnp.where(lane < QK_ROPE, sn, 0.0))


def _rope_tables(positions):
    B, S = positions.shape
    tm = TM_PROJ
    half = QK_ROPE // 2
    inv = ROPE_THETA ** (-jnp.arange(half, dtype=F32) / half)
    inv = jnp.concatenate([inv, inv, jnp.zeros((LANES - QK_ROPE,), F32)])[None, :]
    return pl.pallas_call(
        _rope_table_kernel,
        grid=(B, S // tm),
        in_specs=[pl.BlockSpec((1, tm, 1), lambda b, j: (b, j, 0)),
                  pl.BlockSpec((1, LANES), lambda b, j: (0, 0))],
        out_specs=[pl.BlockSpec((1, tm, LANES), lambda b, j: (b, j, 0))] * 2,
        out_shape=[jax.ShapeDtypeStruct((B, S, LANES), F32)] * 2,
        compiler_params=_cparams("arbitrary", "arbitrary"),
        name="rope_tables",
    )(positions[:, :, None], inv)


def _rope(t, cos, sin):
    lane = _lane_iota(t.shape)
    half = QK_ROPE // 2
    swap = jnp.where(lane < half, pltpu.roll(t, LANES - half, 1), pltpu.roll(t, half, 1))
    return t * cos + swap * sin


def _odd_proj_kernel(x_ref, g_ref, w_ref, cqn_ref, ckvn_ref, wuq_ref, wukv_ref, qnc_ref, knc_ref,
                     cos_ref, sin_ref, q_ref, k_ref, v_ref, z_ref, xbc_ref, dt_ref):
    xn = _rms(x_ref[0], g_ref[...]).astype(BF16)

    def proj(lo, hi):
        return _dot(xn, w_ref[:, lo:hi])

    cos = cos_ref[0]
    sin = sin_ref[0]
    n_qk = QK_NOPE + QK_ROPE
    scale = n_qk ** -0.5 * LOG2E
    ones_col = jnp.where(_lane_iota(cos.shape) == 0, 1.0, 0.0).astype(BF16)

    cq = _rms(proj(OD_CQ, OD_CKV), cqn_ref[...]).astype(BF16)
    q = _dot(cq, wuq_ref[...])
    ckv = _rms(proj(OD_CKV, OD_KR), ckvn_ref[...]).astype(BF16)
    kv = _dot(ckv, wukv_ref[...])
    kr = proj(OD_KR, OD_DT)
    kr_ss = jnp.sum(kr * kr, axis=-1, keepdims=True)
    kr_rot = _rope(kr * knc_ref[:, LANES:], cos, sin)
    for h in range(N_HEADS):
        qh = q[:, h * AUG:(h + 1) * AUG]
        qh = _rms(qh, qnc_ref[...], n_qk) * scale
        q_ref[0, :, h * AUG:h * AUG + LANES] = qh[:, :LANES].astype(BF16)
        q_ref[0, :, h * AUG + LANES:(h + 1) * AUG] = _rope(qh[:, LANES:], cos, sin).astype(BF16)
        kn = kv[:, h * AUG:h * AUG + LANES]
        r = lax.rsqrt((jnp.sum(kn * kn, axis=-1, keepdims=True) + kr_ss) * (1.0 / n_qk) + EPS)
        k_ref[0, :, h * AUG:h * AUG + LANES] = (kn * r * knc_ref[:, :LANES]).astype(BF16)
        k_ref[0, :, h * AUG + LANES:(h + 1) * AUG] = (kr_rot * r).astype(BF16)
        v_ref[0, :, h * AUG:h * AUG + HEAD_DIM] = kv[:, h * AUG + LANES:(h + 1) * AUG].astype(BF16)
        v_ref[0, :, h * AUG + HEAD_DIM:(h + 1) * AUG] = ones_col

    dt_ref[0] = proj(OD_DT, OD_Z)
    z_ref[0] = proj(OD_Z, OD_XBC)
    xbc_ref[0] = proj(OD_XBC, OD_END)


def _odd_proj(x, g, w, cqn, ckvn, wuq, wukv, qnc, knc, cos, sin):
    B, S, D = x.shape
    tm = TM_PROJ
    tok = lambda c: pl.BlockSpec((1, tm, c), lambda b, j: (b, j, 0))
    row = lambda c: pl.BlockSpec((1, c), lambda b, j: (0, 0))
    out_cols = (N_HEADS * AUG, N_HEADS * AUG, N_HEADS * AUG, SSM_INNER, CONV_DIM, LANES)
    out_dt = (BF16, BF16, BF16, F32, F32, F32)
    return pl.pallas_call(
        _odd_proj_kernel,
        grid=(B, S // tm),
        in_specs=[tok(D), row(D), _resident(w.shape), row(Q_LORA), row(KV_LORA),
                  _resident(wuq.shape), _resident(wukv.shape), row(AUG), row(AUG),
                  tok(LANES), tok(LANES)],
        out_specs=[tok(c) for c in out_cols],
        out_shape=[jax.ShapeDtypeStruct((B, S, c), dt) for c, dt in zip(out_cols, out_dt)],
        compiler_params=_cparams("arbitrary", "arbitrary"),
        name="odd_proj",
    )(x, g, w, cqn, ckvn, wuq, wukv, qnc, knc, cos, sin)


def _ssd_kernel(xbc_ref, z_ref, dt_ref, ex_ref, cw_ref, cb_ref, dtb_ref, alog_ref, dskip_ref,
                gn_ref, y_ref, xe_ref, st_ref):
    q = xbc_ref.shape[1]
    halo = 8

    @pl.when(pl.program_id(1) == 0)
    def _():
        xe_ref[0:halo, :] = jnp.zeros((halo, CONV_DIM), F32)
        st_ref[...] = jnp.zeros_like(st_ref)

    xe_ref[halo:halo + q, :] = xbc_ref[0]
    conv = cb_ref[...]
    for w in range(CONV_WIDTH):
        off = halo - (CONV_WIDTH - 1) + w
        conv = conv + xe_ref[off:off + q, :] * cw_ref[w:w + 1, :]
    xe_ref[0:halo, :] = xe_ref[q:q + halo, :]
    act = conv * _sigmoid(conv)
    xs = act[:, :SSM_INNER]
    gs = SSM_STATE
    bm = [act[:, SSM_INNER + g * gs:SSM_INNER + (g + 1) * gs] for g in range(SSM_GROUPS)]
    cm = [act[:, SSM_INNER + (SSM_GROUPS + g) * gs:SSM_INNER + (SSM_GROUPS + g + 1) * gs]
          for g in range(SSM_GROUPS)]

    lane = _lane_iota((q, LANES))
    dt = _softplus(dt_ref[0] + dtb_ref[...])
    a = jnp.where(lane < SSM_HEADS, dt * (-jnp.exp(alog_ref[...])), 0.0)
    tri = _tri(q)
    trif = tri.astype(F32)
    acum = _dot_hi(trif, a)
    acum_t = lax.dot_general(a.T, trif, (((1,), (1,)), ((), ())), preferred_element_type=F32,
                             precision=lax.Precision.HIGHEST)
    last = acum[q - 1:q, :]
    dte = jnp.exp(last - acum)
    eac = jnp.exp(acum)

    stack = jnp.concatenate([dt, dte, eac], axis=0)
    s_hi = stack.astype(BF16)
    s_lo = (stack - s_hi.astype(F32)).astype(BF16)
    wide = _dot(jnp.concatenate([s_hi, s_lo], axis=1), ex_ref[...])
    dt_x, dte_x, eac_x = wide[0:q], wide[q:2 * q], wide[2 * q:3 * q]
    cdec_x = eac_x[q - 1:q, :]

    xd = xs * dt_x
    xdd = (xd * dte_x).astype(BF16)
    xd16 = xd.astype(BF16)
    zero16 = jnp.zeros((q, LANES), BF16)

    rpg = SSM_HEADS // SSM_GROUPS
    gw = rpg * SSM_HEAD_DIM
    y_parts = []
    new_states = []
    for g in range(SSM_GROUPS):
        b16 = bm[g].astype(BF16)
        c16 = cm[g].astype(BF16)
        cb = _dot_nt(c16, b16)
        st_old = st_ref[:, g * gw:(g + 1) * gw]
        y_off = _dot(c16, st_old.astype(BF16)) * eac_x[:, g * gw:(g + 1) * gw]
        y_diag = []
        for pp in range(rpg // 2):
            p = g * (rpg // 2) + pp
            xp = xd16[:, p * LANES:(p + 1) * LANES]
            acc = None
            for e in range(2):
                h = 2 * p + e
                diff = acum[:, h:h + 1] - acum_t[h:h + 1, :]
                lm = jnp.exp(jnp.where(tri, diff, NEG_INF))
                gmat = (cb * lm).astype(BF16)
                keep = (lane < SSM_HEAD_DIM) if e == 0 else (lane >= SSM_HEAD_DIM)
                part = _dot(gmat, jnp.where(keep, xp, zero16))
                acc = part if acc is None else acc + part
            y_diag.append(acc)
        y_parts.append(jnp.concatenate(y_diag, axis=1) + y_off)
        st_new = _dot(bm[g].T.astype(BF16), xdd[:, g * gw:(g + 1) * gw])
        new_states.append(st_new)
    for g in range(SSM_GROUPS):
        sl = slice(g * gw, (g + 1) * gw)
        st_ref[:, sl] = st_ref[:, sl] * cdec_x[:, sl] + new_states[g]

    y = jnp.concatenate(y_parts, axis=1) + dskip_ref[...] * xs
    zz = z_ref[0]
    y = y * (zz * _sigmoid(zz))
    for g in range(SSM_GROUPS):
        sl = slice(g * gw, (g + 1) * gw)
        y_ref[0, :, sl] = _rms(y[:, sl], gn_ref[:, sl]).astype(y_ref.dtype)


def _ssd(xbc, z, dt, cw, cb, dtb, alog, dskip, gn):
    B, S, _ = xbc.shape
    q = SSD_CHUNK
    tok = lambda c: pl.BlockSpec((1, q, c), lambda b, j: (b, j, 0))
    row = lambda r, c: pl.BlockSpec((r, c), lambda b, j: (0, 0))
    head = jnp.arange(LANES)[:, None]
    chan = jnp.arange(SSM_INNER)[None, :] // SSM_HEAD_DIM
    ex = jnp.tile((head == chan).astype(BF16), (2, 1))
    return pl.pallas_call(
        _ssd_kernel,
        grid=(B, S // q),
        in_specs=[tok(CONV_DIM), tok(SSM_INNER), tok(LANES), row(*ex.shape),
                  row(CONV_WIDTH, CONV_DIM), row(1, CONV_DIM), row(1, LANES), row(1, LANES),
                  row(1, SSM_INNER), row(1, SSM_INNER)],
        out_specs=tok(SSM_INNER),
        out_shape=jax.ShapeDtypeStruct((B, S, SSM_INNER), BF16),
        scratch_shapes=[pltpu.VMEM((q + 8, CONV_DIM), F32), pltpu.VMEM((SSM_STATE, SSM_INNER), F32)],
        compiler_params=_cparams("arbitrary", "arbitrary"),
        name="ssd_scan",
    )(xbc, z, dt, ex, cw, cb, dtb, alog, dskip, gn)


def _out_mlp_kernel(x_ref, o1_ref, o2_ref, wo1_ref, wo2_ref, g_ref, w1_ref, w2_ref, y_ref,
                    xn_ref, acc_ref, *, tf):
    xnew = x_ref[...] + _dot(o1_ref[...], wo1_ref[...]) + _dot(o2_ref[...], wo2_ref[...])
    xn_ref[...] = _rms(xnew, g_ref[...]).astype(BF16)
    acc_ref[...] = xnew

    def body(c, carry):
        cs = pl.multiple_of(c * tf, tf)
        h = _dot(xn_ref[...], w1_ref[:, pl.ds(cs, tf)])
        a = jnp.square(jnp.maximum(h, 0.0)).astype(BF16)
        acc_ref[...] += _dot(a, w2_ref[pl.ds(cs, tf), :])
        return carry

    lax.fori_loop(0, w1_ref.shape[1] // tf, body, 0)
    y_ref[...] = acc_ref[...]


def _out_mlp(x, o1, o2, wo1, wo2, g, w1, w2):
    T, D = x.shape
    tm = TM_MLP
    tok = lambda c: pl.BlockSpec((tm, c), lambda i: (i, 0))
    return pl.pallas_call(
        functools.partial(_out_mlp_kernel, tf=TF_MLP),
        grid=(T // tm,),
        in_specs=[tok(D), tok(o1.shape[1]), tok(o2.shape[1]), _resident(wo1.shape),
                  _resident(wo2.shape), pl.BlockSpec((1, D), lambda i: (0, 0)),
                  _resident(w1.shape), _resident(w2.shape)],
        out_specs=tok(D),
        out_shape=jax.ShapeDtypeStruct((T, D), F32),
        scratch_shapes=[pltpu.VMEM((tm, D), BF16), pltpu.VMEM((tm, D), F32)],
        compiler_params=_cparams("arbitrary"),
        name="out_mlp",
    )(x, o1, o2, wo1, wo2, g, w1, w2)


def _pack_even_w(w):
    w = w.astype(BF16)
    qa, ka, va, qi, ki, wi, qb, kb, vb, fb = jnp.split(
        w, [512, 640, 768, 1280, 1344, 1352, 1864, 2376, 2888], axis=-1)
    pad = jnp.zeros(w.shape[:-1] + (LANES - IDX_DIM - N_IDX_HEADS - N_HEADS,), w.dtype)
    return jnp.concatenate([qa, ka, va, qi, qb, kb, vb, ki, wi, fb, pad], axis=-1)


def _pack_odd_w(w):
    w = w.astype(BF16)
    cq, ckv, kr, z, xbc, dt = jnp.split(w, [384, 640, 704, 1728, 3776], axis=-1)
    zpad = lambda n: jnp.zeros(w.shape[:-1] + (n,), w.dtype)
    return jnp.concatenate([cq, ckv, kr, zpad(LANES - QK_ROPE), dt, zpad(LANES - SSM_HEADS), z, xbc],
                           axis=-1)


def _pack_wuq(w):
    n_qk = QK_NOPE + QK_ROPE
    w = w.reshape(w.shape[0], N_HEADS, n_qk)
    w = jnp.pad(w, ((0, 0), (0, 0), (0, AUG - n_qk)))
    return w.reshape(w.shape[0], N_HEADS * AUG).astype(BF16)


def _row(v, width=None):
    v = v.astype(F32)[None, :]
    if width is not None and width > v.shape[1]:
        v = jnp.pad(v, ((0, 0), (0, width - v.shape[1])))
    return v


def kernel(x, positions, ev_norm, ev_w_in, ev_b_f, ev_qn_a, ev_kn_a, ev_qn_b, ev_kn_b, ev_w_out,
           od_norm, od_w_in, od_cq_norm, od_ckv_norm, od_w_uq, od_w_ukv, od_qn_c, od_kn_c,
           od_conv_w, od_conv_b, od_dt_bias, od_a_log, od_d_skip, od_gate_norm, od_w_out,
           mlp_norm, mlp_w1, mlp_w2):
    B, S, D = x.shape
    depth = mlp_w1.shape[0]
    ev_w, od_w = _pack_even_w(ev_w_in), _pack_odd_w(od_w_in)
    ev_wo, od_wo = ev_w_out.astype(BF16), od_w_out.astype(BF16)
    w1, w2 = mlp_w1.astype(BF16), mlp_w2.astype(BF16)
    cos = sin = None
    for layer in range(depth):
        i = layer // 2
        if layer % 2 == 0:
            bf = jnp.zeros((1, LANES), F32).at[0, MISC_F:MISC_F + N_HEADS].set(ev_b_f[i])
            qa, ka, va, qi, kd, wi, qb, kb, vb = _even_proj(
                x, _row(ev_norm[i]), ev_w[i], bf, _row(ev_qn_a[i]),
                _row(ev_kn_a[i]), _row(ev_qn_b[i]), _row(ev_kn_b[i]))
            o1 = _dsa(qa, ka, va, qi, kd, wi)
            o2 = _flash(qb, kb, vb)
            wo = ev_wo[i]
        else:
            if cos is None:
                cos, sin = _rope_tables(positions)
            q, k, v, z, xbc, dt = _odd_proj(
                x, _row(od_norm[i]), od_w[i], _row(od_cq_norm[i]),
                _row(od_ckv_norm[i]), _pack_wuq(od_w_uq[i]), od_w_ukv[i].astype(BF16),
                _row(od_qn_c[i], AUG), _row(od_kn_c[i], AUG), cos, sin)
            o1 = _flash(q, k, v)
            o2 = _ssd(xbc, z, dt, od_conv_w[i].astype(F32), _row(od_conv_b[i]),
                      _row(od_dt_bias[i], LANES), _row(od_a_log[i], LANES),
                      _row(jnp.repeat(od_d_skip[i], SSM_HEAD_DIM)), _row(od_gate_norm[i]))
            wo = od_wo[i]
        k1 = o1.shape[-1]
        x = _out_mlp(x.reshape(B * S, D), o1.reshape(B * S, k1), o2.reshape(B * S, -1),
                     wo[:k1], wo[k1:], _row(mlp_norm[layer]), w1[layer], w2[layer]).reshape(B, S, D)
    return x
```

```python
import functools

import jax
import jax.numpy as jnp
from jax import lax
from jax.experimental import pallas as pl
from jax.experimental.pallas import tpu as pltpu

F32 = jnp.float32
BF16 = jnp.bfloat16
EPS = 1e-6
NEG_INF = float("-inf")
F32_MAX = 3.4028234663852886e38
LOG2E = 1.4426950408889634

D_MODEL = 1024
HEAD_DIM = 128
N_HEADS = 4
N_IDX_HEADS = 8
IDX_DIM = 64
TOPK_MAX = 256
QK_NOPE = 128
QK_ROPE = 64
Q_LORA = 384
KV_LORA = 256
ROPE_THETA = 10000.0
SSM_INNER = 1024
SSM_HEAD_DIM = 64
SSM_HEADS = 16
SSM_GROUPS = 4
SSM_STATE = 128
CONV_WIDTH = 4
CONV_DIM = SSM_INNER + 2 * SSM_GROUPS * SSM_STATE
D_FF = 4 * D_MODEL

LANES = 128
AUG = 2 * LANES
VMEM_LIMIT = 56 * 1024 * 1024

TM_PROJ = 512
TM_EVEN = 1024
CUMSUM_BLOCK = 512
TM_MLP = 1024
TF_MLP = 1024
TQ_FLASH = 1024
TK_FLASH = 1024
TQ_DSA = 512
TK_DSA = 512
DSA_HEAD_GROUP = 2
SSD_CHUNK = 128
BISECT_GROUP = 4
BISECT_ROUND = 24
BISECT_MAX_ROUNDS = 8

EV_QA, EV_KA, EV_VA, EV_QI, EV_QB, EV_KB, EV_VB, EV_MISC, EV_END = (
    0, 512, 640, 768, 1280, 1792, 2304, 2816, 2944)
MISC_W = 64
MISC_F = 72
OD_CQ, OD_CKV, OD_KR, OD_DT, OD_Z, OD_XBC, OD_END = (0, 384, 640, 768, 896, 1920, 3968)


def _cparams(*sem):
    return pltpu.CompilerParams(dimension_semantics=sem, vmem_limit_bytes=VMEM_LIMIT)


def _resident(shape):
    nd = len(shape)
    return pl.BlockSpec(shape, lambda *_: (0,) * nd, pipeline_mode=pl.Buffered(1))


def _rms(x, g, n=None):
    n = x.shape[-1] if n is None else n
    ss = jnp.sum(x * x, axis=-1, keepdims=True)
    return x * lax.rsqrt(ss * (1.0 / n) + EPS) * g


def _dot(a, b):
    return jnp.dot(a, b, preferred_element_type=F32)


def _dot_nt(a, b):
    return lax.dot_general(a, b, (((1,), (1,)), ((), ())), preferred_element_type=F32)


def _dot_hi(a, b):
    return jnp.dot(a, b, preferred_element_type=F32, precision=lax.Precision.HIGHEST)


def _sigmoid(x):
    return 1.0 / (1.0 + jnp.exp(-x))


def _softplus(x):
    return jnp.maximum(x, 0.0) + jnp.log(1.0 + jnp.exp(-jnp.abs(x)))


def _lane_iota(shape):
    return lax.broadcasted_iota(jnp.int32, shape, len(shape) - 1)


def _tri(n):
    r = lax.broadcasted_iota(jnp.int32, (n, n), 0)
    c = lax.broadcasted_iota(jnp.int32, (n, n), 1)
    return r >= c


def _split3(c):
    hi = c.astype(BF16).astype(F32)
    mid = (c - hi).astype(BF16).astype(F32)
    lo = (c - hi - mid).astype(BF16).astype(F32)
    return hi, mid, lo


def _even_proj_kernel(x_ref, g_ref, w_ref, bf_ref, qna_ref, kna_ref, qnb_ref, knb_ref,
                      qa_ref, ka_ref, va_ref, qi_ref, kd_ref, wi_ref, qb_ref, kb_ref, vb_ref,
                      carry_ref):
    tm = x_ref.shape[1]

    @pl.when(pl.program_id(1) == 0)
    def _():
        carry_ref[...] = jnp.zeros_like(carry_ref)

    xn = _rms(x_ref[0], g_ref[...]).astype(BF16)

    def proj(lo, hi):
        return _dot(xn, w_ref[:, lo:hi])

    scale = HEAD_DIM ** -0.5 * LOG2E
    lane = _lane_iota((tm, LANES))
    ones_col = jnp.where(lane == 0, 1.0, 0.0).astype(BF16)
    qa = proj(EV_QA, EV_KA)
    for h in range(N_HEADS):
        sl = slice(h * HEAD_DIM, (h + 1) * HEAD_DIM)
        qa_ref[0, :, sl] = (_rms(qa[:, sl], qna_ref[...]) * scale).astype(BF16)
    ka_ref[0] = _rms(proj(EV_KA, EV_VA), kna_ref[...]).astype(BF16)
    va_ref[0, :, :HEAD_DIM] = proj(EV_VA, EV_QI).astype(BF16)
    va_ref[0, :, HEAD_DIM:] = ones_col
    qi_ref[0] = proj(EV_QI, EV_QB).astype(BF16)
    vb = proj(EV_VB, EV_MISC)
    for h in range(N_HEADS):
        vb_ref[0, :, h * AUG:h * AUG + HEAD_DIM] = vb[:, h * HEAD_DIM:(h + 1) * HEAD_DIM].astype(BF16)
        vb_ref[0, :, h * AUG + HEAD_DIM:(h + 1) * AUG] = ones_col

    misc = proj(EV_MISC, EV_END)
    kd_ref[0] = jnp.where(lane < IDX_DIM, misc, pltpu.roll(misc, IDX_DIM, 1)).astype(BF16)
    wi_ref[0] = misc.T[MISC_W:MISC_W + N_IDX_HEADS, :]

    zf = misc + bf_ref[...]
    logf = jnp.minimum(zf, 0.0) - jnp.log(1.0 + jnp.exp(-jnp.abs(zf)))
    logf = jnp.where((lane >= MISC_F) & (lane < MISC_F + N_HEADS), logf, 0.0)
    cb = min(tm, CUMSUM_BLOCK)
    tri = _tri(cb).astype(BF16)
    carry = carry_ref[...]
    blocks = []
    for r in range(tm // cb):
        parts = _split3(logf[r * cb:(r + 1) * cb])
        blocks.append(sum(_dot(tri, part.astype(BF16)) for part in parts) + carry)
        carry = blocks[-1][cb - 1:cb, :]
    csum = jnp.concatenate(blocks, axis=0)
    carry_ref[...] = carry

    qb = proj(EV_QB, EV_KB)
    kb = proj(EV_KB, EV_VB)
    one = jnp.ones((tm, LANES), F32)
    zero = jnp.zeros((tm, LANES), F32)
    for h in range(N_HEADS):
        sl = slice(h * HEAD_DIM, (h + 1) * HEAD_DIM)
        c = jnp.broadcast_to(csum[:, MISC_F + h:MISC_F + h + 1], (tm, LANES)) * LOG2E
        hi, mid, lo = _split3(c)
        aq = jnp.where(lane == 0, hi, jnp.where(lane == 1, mid, jnp.where(lane == 2, lo,
             jnp.where(lane < 6, one, zero))))
        ak = jnp.where(lane < 3, one, jnp.where(lane == 3, -hi, jnp.where(lane == 4, -mid,
             jnp.where(lane == 5, -lo, zero))))
        qb_ref[0, :, h * AUG:h * AUG + LANES] = (_rms(qb[:, sl], qnb_ref[...]) * scale).astype(BF16)
        qb_ref[0, :, h * AUG + LANES:(h + 1) * AUG] = aq.astype(BF16)
        kb_ref[0, :, h * AUG:h * AUG + LANES] = _rms(kb[:, sl], knb_ref[...]).astype(BF16)
        kb_ref[0, :, h * AUG + LANES:(h + 1) * AUG] = ak.astype(BF16)


def _even_proj(x, g, w, bf, qna, kna, qnb, knb):
    B, S, D = x.shape
    tm = min(TM_EVEN, S)
    tok = lambda c: pl.BlockSpec((1, tm, c), lambda b, j: (b, j, 0))
    row = lambda c: pl.BlockSpec((1, c), lambda b, j: (0, 0))
    out_cols = (512, 128, AUG, 512, 128, None, N_HEADS * AUG, N_HEADS * AUG, N_HEADS * AUG)
    out_specs = [tok(c) if c else pl.BlockSpec((1, N_IDX_HEADS, tm), lambda b, j: (b, 0, j))
                 for c in out_cols]
    out_shape = [jax.ShapeDtypeStruct((B, S, c), BF16) if c else
                 jax.ShapeDtypeStruct((B, N_IDX_HEADS, S), F32) for c in out_cols]
    return pl.pallas_call(
        _even_proj_kernel,
        grid=(B, S // tm),
        in_specs=[tok(D), row(D), _resident(w.shape), row(LANES), row(LANES), row(LANES),
                  row(LANES), row(LANES)],
        out_specs=out_specs,
        out_shape=out_shape,
        scratch_shapes=[pltpu.VMEM((1, LANES), F32)],
        compiler_params=_cparams("arbitrary", "arbitrary"),
        name="even_proj",
    )(x, g, w, bf, qna, kna, qnb, knb)


def _fold_max(mx, s):
    for g in range(s.shape[1] // LANES):
        mx = jnp.maximum(mx, s[:, g * LANES:(g + 1) * LANES])
    return mx


def _flash_kernel(q_ref, k_ref, v_ref, o_ref, lg_ref, acc_ref, *, tq, tk):
    i = pl.program_id(2)
    q = q_ref[0]
    half = tq // 2

    def kchunk(ref, j):
        return ref[0, pl.ds(pl.multiple_of(j * tk, tk), tk), :]

    def pass_a(j, mx):
        s = _dot_nt(q, kchunk(k_ref, j))
        lg_ref[j] = s
        return _fold_max(mx, s)

    mx = lax.fori_loop(0, i, pass_a, jnp.full((tq, LANES), NEG_INF, F32))
    kd = kchunk(k_ref, i)
    tri = _tri(half)
    s_top = jnp.where(tri, _dot_nt(q[:half], kd[:half]), NEG_INF)
    s_bot = _dot_nt(q[half:], kd)
    s_bot = jnp.concatenate([s_bot[:, :half], jnp.where(tri, s_bot[:, half:], NEG_INF)], axis=1)
    lg_ref[i, :half, :half] = s_top
    lg_ref[i, half:, :] = s_bot
    mx = jnp.concatenate([_fold_max(mx[:half], s_top), _fold_max(mx[half:], s_bot)], axis=0)
    m = jnp.max(mx, axis=-1, keepdims=True)

    acc_ref[...] = jnp.zeros_like(acc_ref)

    def pass_b(j, c):
        p = jnp.exp2(lg_ref[j] - m).astype(BF16)
        acc_ref[...] += _dot(p, kchunk(v_ref, j))
        return c

    lax.fori_loop(0, i, pass_b, 0)
    vd = kchunk(v_ref, i)
    p_top = jnp.exp2(lg_ref[i, :half, :half] - m[:half]).astype(BF16)
    p_bot = jnp.exp2(lg_ref[i, half:, :] - m[half:]).astype(BF16)
    acc_top = acc_ref[:half, :] + _dot(p_top, vd[:half])
    acc_bot = acc_ref[half:, :] + _dot(p_bot, vd)
    o_ref[0, :half, :] = (acc_top[:, :HEAD_DIM] / acc_top[:, HEAD_DIM:HEAD_DIM + 1]).astype(o_ref.dtype)
    o_ref[0, half:, :] = (acc_bot[:, :HEAD_DIM] / acc_bot[:, HEAD_DIM:HEAD_DIM + 1]).astype(o_ref.dtype)


def _flash(q, k, v):
    B, S, _ = q.shape
    tq, tk = min(TQ_FLASH, S), min(TK_FLASH, S)
    assert tq == tk, "the diagonal chunk is split on the query block's own key chunk"
    return pl.pallas_call(
        functools.partial(_flash_kernel, tq=tq, tk=tk),
        grid=(B, N_HEADS, S // tq),
        in_specs=[pl.BlockSpec((1, tq, AUG), lambda b, h, i: (b, i, h)),
                  pl.BlockSpec((1, S, AUG), lambda b, h, i: (b, 0, h)),
                  pl.BlockSpec((1, S, AUG), lambda b, h, i: (b, 0, h))],
        out_specs=pl.BlockSpec((1, tq, HEAD_DIM), lambda b, h, i: (b, i, h)),
        out_shape=jax.ShapeDtypeStruct((B, S, N_HEADS * HEAD_DIM), BF16),
        scratch_shapes=[pltpu.VMEM((S // tk, tq, tk), F32), pltpu.VMEM((tq, AUG), F32)],
        compiler_params=_cparams("arbitrary", "arbitrary", "arbitrary"),
        name="flash_attn",
    )(q, k, v)


def _dsa_kernel(qa_ref, ka_ref, va_ref, qi_ref, kd_ref, wi_ref, o_ref,
                sc_ref, qstk_ref, lg_ref, acc_ref, *, tkc, topk):
    tq = qa_ref.shape[1]
    i = pl.program_id(1)
    nkc = (i * tq + tq + tkc - 1) // tkc
    kf = float(topk)
    lane = _lane_iota((tq, LANES))
    keyi = lax.broadcasted_iota(jnp.int32, (tkc, tq), 0)
    qpos = i * tq + lax.broadcasted_iota(jnp.int32, (tkc, tq), 1)

    q8 = qi_ref[0]
    for j in range(N_IDX_HEADS):
        grp = q8[:, LANES * (j // 2):LANES * (j // 2 + 1)]
        keep = (lane < IDX_DIM) if j % 2 == 0 else (lane >= IDX_DIM)
        qstk_ref[tq * j:tq * (j + 1), :] = jnp.where(keep, grp, jnp.zeros_like(grp))
    wrows = [wi_ref[0, j:j + 1, :] for j in range(N_IDX_HEADS)]

    def chunk(ref, kc):
        return ref[0, pl.ds(pl.multiple_of(kc * tkc, tkc), tkc), :]

    def fold8(x, op):
        cols = []
        for g in range(x.shape[1] // LANES):
            xg = x[:, g * LANES:(g + 1) * LANES]
            rows = xg.shape[0]
            while rows > 8:
                rows //= 2
                xg = op(xg[:rows], xg[rows:])
            cols.append(xg)
        return jnp.concatenate(cols, axis=1)

    def p1(kc, carry, diagonal=False):
        rmax, rmin, npos, nnonneg = carry
        s8 = _dot_nt(chunk(kd_ref, kc), qstk_ref[...])
        score = wrows[0] * jnp.maximum(s8[:, 0:tq], 0.0)
        for j in range(1, N_IDX_HEADS):
            score = score + wrows[j] * jnp.maximum(s8[:, tq * j:tq * (j + 1)], 0.0)
        low = high = score
        if diagonal:
            adm = (kc * tkc + keyi) <= qpos
            low = jnp.where(adm, score, NEG_INF)
            high = jnp.where(adm, score, -NEG_INF)
        sc_ref[kc] = low
        rmax = jnp.maximum(rmax, fold8(low, jnp.maximum))
        rmin = jnp.minimum(rmin, fold8(high, jnp.minimum))
        npos = npos + fold8(jnp.where(low > 0.0, 1.0, 0.0), jnp.add)
        nnonneg = nnonneg + fold8(jnp.where(low >= 0.0, 1.0, 0.0), jnp.add)
        return rmax, rmin, npos, nnonneg

    zero8 = jnp.zeros((8, tq), F32)
    carry = lax.fori_loop(0, nkc - 1, p1, (jnp.full((8, tq), NEG_INF, F32),
                                           jnp.full((8, tq), -NEG_INF, F32), zero8, zero8))
    rmax, rmin, npos, nnonneg = p1(nkc - 1, carry, diagonal=True)
    rmax = jnp.max(rmax, axis=0, keepdims=True)
    rmin = jnp.min(rmin, axis=0, keepdims=True)
    c_gt0 = jnp.sum(npos, axis=0, keepdims=True)
    c_ge0 = jnp.sum(nnonneg, axis=0, keepdims=True)

    def row_count(pred):
        def cb(kc, c):
            return c + fold8(jnp.where(pred(sc_ref[kc], kc), 1.0, 0.0), jnp.add)
        return jnp.sum(lax.fori_loop(0, nkc, cb, zero8), axis=0, keepdims=True)

    def any_row(flag):
        return jnp.max(flag.astype(jnp.int32))

    def select_threshold():
        hi0 = rmax + (jnp.abs(rmax) * 1.2e-7 + 1e-37)
        n_adm = (qpos[0:1, :] + 1).astype(F32)
        above = c_ge0 >= kf
        pend0 = (above & (c_gt0 < kf)).astype(jnp.int32)
        lo0 = jnp.where(above, 0.0, rmin)
        cnt0 = jnp.where(above, c_ge0, n_adm)
        hi0 = jnp.where(above, hi0, 0.0)
        few = n_adm <= kf
        lo0 = jnp.where(few, -F32_MAX, lo0)
        cnt0 = jnp.where(few, kf, cnt0)
        pend0 = jnp.where(few, 0, pend0)

        def unresolved(cnt, pend):
            return any_row((cnt > kf + 1.0) & (pend == 0))

        def fold_two_min(x):
            c1, c2 = [], []
            for g in range(tq // LANES):
                xg = x[:, g * LANES:(g + 1) * LANES]
                rows = tkc // 2
                m1 = jnp.minimum(xg[:rows], xg[rows:])
                m2 = jnp.maximum(xg[:rows], xg[rows:])
                while rows > 8:
                    rows //= 2
                    a1, b1, a2, b2 = m1[:rows], m1[rows:], m2[:rows], m2[rows:]
                    m1, m2 = (jnp.minimum(a1, b1),
                              jnp.minimum(jnp.maximum(a1, b1), jnp.minimum(a2, b2)))
                c1.append(m1)
                c2.append(m2)
            return jnp.concatenate(c1, axis=1), jnp.concatenate(c2, axis=1)

        def snap(lo, cnt, pend):
            def two_min(kc, st):
                m1, m2 = st
                blk = sc_ref[kc]
                a1, a2 = fold_two_min(jnp.where(blk >= lo, blk, -NEG_INF))
                return (jnp.minimum(m1, a1),
                        jnp.minimum(jnp.maximum(m1, a1), jnp.minimum(m2, a2)))
            inf8 = jnp.full((8, tq), -NEG_INF, F32)
            m1, m2 = lax.fori_loop(0, nkc, two_min, (inf8, inf8))
            a1 = jnp.min(m1, axis=0, keepdims=True)
            at_min = m1 == a1
            others = jnp.minimum(jnp.min(jnp.where(at_min, -NEG_INF, m1), axis=0, keepdims=True),
                                 jnp.min(m2, axis=0, keepdims=True))
            n_min = jnp.sum(jnp.where(at_min, 1.0, 0.0), axis=0, keepdims=True)
            a2 = jnp.where(n_min > 1.0, a1, others)
            ok = (cnt == kf + 1.0) & (pend == 0) & (a2 > a1)
            return jnp.where(ok, a2, lo), jnp.where(ok, kf, cnt)

        def bis_cond(st):
            return (st[-1] > 0) & (st[0] < BISECT_ROUND)

        def bis_body(st):
            it, lo, hi, cnt, pend, _ = st
            for _ in range(BISECT_GROUP):
                mid = lo + (hi - lo) * 0.5
                c = row_count(lambda blk, kc, mid=mid: blk >= mid)
                up = c >= kf
                lo = jnp.where(up, mid, lo)
                hi = jnp.where(up, hi, mid)
                cnt = jnp.where(up, c, cnt)
            return it + BISECT_GROUP, lo, hi, cnt, pend, unresolved(cnt, pend)

        def resolve_ties(lo, cnt):
            def vmin(kc, v):
                blk = sc_ref[kc]
                return jnp.minimum(v, fold8(jnp.where(blk >= lo, blk, -NEG_INF), jnp.minimum))
            v = lax.fori_loop(0, nkc, vmin, jnp.full((8, tq), -NEG_INF, F32))
            v = jnp.min(v, axis=0, keepdims=True)
            cgt = row_count(lambda blk, kc: blk > v)
            tie = (cnt != kf) & (cgt < kf)
            need = kf - cgt
            tri = _tri(tkc).astype(BF16)

            def drop(kc, seen):
                blk = sc_ref[kc]
                eq = tie & (blk == v)
                rank = _dot(tri, jnp.where(eq, 1.0, 0.0).astype(BF16)) + seen
                sc_ref[kc] = jnp.where(eq & (rank > need), NEG_INF, blk)
                return rank[tkc - 1:tkc, :]
            lax.fori_loop(0, nkc, drop, jnp.zeros((1, tq), F32))
            return jnp.where(tie, v, lo), jnp.where(tie, kf, cnt)

        def round_cond(st):
            return (st[-1] > 0) & (st[0] < BISECT_MAX_ROUNDS)

        def round_body(st):
            r, lo, hi, cnt, pend, _ = st
            _, lo, hi, cnt, pend, _ = lax.while_loop(
                bis_cond, bis_body, (0, lo, hi, cnt, pend, unresolved(cnt, pend)))
            lo, cnt = lax.cond(any_row((cnt == kf + 1.0) & (pend == 0)) > 0, snap,
                               lambda a, b, c: (a, b), lo, cnt, pend)
            lo, cnt = lax.cond(any_row(cnt != kf) > 0, resolve_ties, lambda a, b: (a, b), lo, cnt)
            return r + 1, lo, hi, cnt, jnp.zeros_like(pend), any_row(cnt != kf)

        st = lax.while_loop(round_cond, round_body,
                            (0, lo0, hi0, cnt0, pend0, any_row(cnt0 != kf)))
        return st[1]

    thr = lax.cond((i + 1) * tq > topk, select_threshold,
                   lambda: jnp.full((1, tq), -F32_MAX, F32))

    qa = qa_ref[0]
    hg = DSA_HEAD_GROUP
    for g0 in range(0, N_HEADS, hg):
        qs = jnp.concatenate([qa[:, h * HEAD_DIM:(h + 1) * HEAD_DIM] for h in range(g0, g0 + hg)],
                             axis=0)

        def p3a(kc, mx, qs=qs, first=(g0 == 0)):
            s = _dot_nt(qs, chunk(ka_ref, kc))
            if first:
                sel = jnp.where(sc_ref[kc] >= thr, 0.0, NEG_INF).T
                sc_ref[kc] = sel
            else:
                sel = sc_ref[kc]
            parts = []
            for h in range(hg):
                rs = slice(h * tq, (h + 1) * tq)
                sh = s[rs] + sel
                lg_ref[kc, rs, :] = sh
                parts.append(_fold_max(mx[rs], sh))
            return jnp.concatenate(parts, axis=0)

        mx = lax.fori_loop(0, nkc, p3a, jnp.full((hg * tq, LANES), NEG_INF, F32))
        m = jnp.max(mx, axis=-1, keepdims=True)

        acc_ref[...] = jnp.zeros_like(acc_ref)

        def p3b(kc, c, m=m):
            p = jnp.exp2(lg_ref[kc] - m).astype(BF16)
            acc_ref[...] += _dot(p, chunk(va_ref, kc))
            return c

        lax.fori_loop(0, nkc, p3b, 0)
        for h in range(hg):
            acc = acc_ref[h * tq:(h + 1) * tq, :]
            o_ref[0, :, (g0 + h) * HEAD_DIM:(g0 + h + 1) * HEAD_DIM] = (
                acc[:, :HEAD_DIM] / acc[:, HEAD_DIM:HEAD_DIM + 1]).astype(o_ref.dtype)


def _dsa(qa, ka, va, qi, kd, wi):
    B, S, _ = qa.shape
    tq, tkc = TQ_DSA, min(TK_DSA, S)
    assert tq == tkc, "the transposed selection mask reuses the score scratch"
    topk = min(TOPK_MAX, S // 4)
    qblk = lambda c: pl.BlockSpec((1, tq, c), lambda b, i: (b, i, 0))
    full = lambda c: pl.BlockSpec((1, S, c), lambda b, i: (b, 0, 0))
    return pl.pallas_call(
        functools.partial(_dsa_kernel, tkc=tkc, topk=topk),
        grid=(B, S // tq),
        in_specs=[qblk(512), full(128), full(AUG), qblk(512), full(128),
                  pl.BlockSpec((1, N_IDX_HEADS, tq), lambda b, i: (b, 0, i))],
        out_specs=qblk(512),
        out_shape=jax.ShapeDtypeStruct((B, S, N_HEADS * HEAD_DIM), BF16),
        scratch_shapes=[pltpu.VMEM((S // tkc, tkc, tq), F32),
                        pltpu.VMEM((N_IDX_HEADS * tq, LANES), BF16),
                        pltpu.VMEM((S // tkc, DSA_HEAD_GROUP * tq, tkc), F32),
                        pltpu.VMEM((DSA_HEAD_GROUP * tq, AUG), F32)],
        compiler_params=_cparams("arbitrary", "arbitrary"),
        name="dsa_attn",
    )(qa, ka, va, qi, kd, wi)


def _rope_table_kernel(pos_ref, inv_ref, cos_ref, sin_ref):
    tm = pos_ref.shape[1]
    lane = _lane_iota((tm, LANES))
    ang = pos_ref[0].astype(F32) * inv_ref[...]
    half = QK_ROPE // 2
    cos_ref[0] = jnp.where(lane < QK_ROPE, jnp.cos(ang), 0.0)
    sn = jnp.sin(ang)
    sin_ref[0] = jnp.where(lane < half, -sn, jnp.where(lane < QK_ROPE, sn, 0.0))


def _rope_tables(positions):
    B, S = positions.shape
    tm = TM_PROJ
    half = QK_ROPE // 2
    inv = ROPE_THETA ** (-jnp.arange(half, dtype=F32) / half)
    inv = jnp.concatenate([inv, inv, jnp.zeros((LANES - QK_ROPE,), F32)])[None, :]
    return pl.pallas_call(
        _rope_table_kernel,
        grid=(B, S // tm),
        in_specs=[pl.BlockSpec((1, tm, 1), lambda b, j: (b, j, 0)),
                  pl.BlockSpec((1, LANES), lambda b, j: (0, 0))],
        out_specs=[pl.BlockSpec((1, tm, LANES), lambda b, j: (b, j, 0))] * 2,
        out_shape=[jax.ShapeDtypeStruct((B, S, LANES), F32)] * 2,
        compiler_params=_cparams("arbitrary", "arbitrary"),
        name="rope_tables",
    )(positions[:, :, None], inv)


def _rope(t, cos, sin):
    lane = _lane_iota(t.shape)
    half = QK_ROPE // 2
    swap = jnp.where(lane < half, pltpu.roll(t, LANES - half, 1), pltpu.roll(t, half, 1))
    return t * cos + swap * sin


def _odd_proj_kernel(x_ref, g_ref, w_ref, cqn_ref, ckvn_ref, wuq_ref, wukv_ref, qnc_ref, knc_ref,
                     cos_ref, sin_ref, q_ref, k_ref, v_ref, z_ref, xbc_ref, dt_ref):
    xn = _rms(x_ref[0], g_ref[...]).astype(BF16)

    def proj(lo, hi):
        return _dot(xn, w_ref[:, lo:hi])

    cos = cos_ref[0]
    sin = sin_ref[0]
    n_qk = QK_NOPE + QK_ROPE
    scale = n_qk ** -0.5 * LOG2E
    ones_col = jnp.where(_lane_iota(cos.shape) == 0, 1.0, 0.0).astype(BF16)

    cq = _rms(proj(OD_CQ, OD_CKV), cqn_ref[...]).astype(BF16)
    q = _dot(cq, wuq_ref[...])
    ckv = _rms(proj(OD_CKV, OD_KR), ckvn_ref[...]).astype(BF16)
    kv = _dot(ckv, wukv_ref[...])
    kr = proj(OD_KR, OD_DT)
    kr_ss = jnp.sum(kr * kr, axis=-1, keepdims=True)
    kr_rot = _rope(kr * knc_ref[:, LANES:], cos, sin)
    for h in range(N_HEADS):
        qh = q[:, h * AUG:(h + 1) * AUG]
        qh = _rms(qh, qnc_ref[...], n_qk) * scale
        q_ref[0, :, h * AUG:h * AUG + LANES] = qh[:, :LANES].astype(BF16)
        q_ref[0, :, h * AUG + LANES:(h + 1) * AUG] = _rope(qh[:, LANES:], cos, sin).astype(BF16)
        kn = kv[:, h * AUG:h * AUG + LANES]
        r = lax.rsqrt((jnp.sum(kn * kn, axis=-1, keepdims=True) + kr_ss) * (1.0 / n_qk) + EPS)
        k_ref[0, :, h * AUG:h * AUG + LANES] = (kn * r * knc_ref[:, :LANES]).astype(BF16)
        k_ref[0, :, h * AUG + LANES:(h + 1) * AUG] = (kr_rot * r).astype(BF16)
        v_ref[0, :, h * AUG:h * AUG + HEAD_DIM] = kv[:, h * AUG + LANES:(h + 1) * AUG].astype(BF16)
        v_ref[0, :, h * AUG + HEAD_DIM:(h + 1) * AUG] = ones_col

    dt_ref[0] = proj(OD_DT, OD_Z)
    z_ref[0] = proj(OD_Z, OD_XBC)
    xbc_ref[0] = proj(OD_XBC, OD_END)


def _odd_proj(x, g, w, cqn, ckvn, wuq, wukv, qnc, knc, cos, sin):
    B, S, D = x.shape
    tm = TM_PROJ
    tok = lambda c: pl.BlockSpec((1, tm, c), lambda b, j: (b, j, 0))
    row = lambda c: pl.BlockSpec((1, c), lambda b, j: (0, 0))
    out_cols = (N_HEADS * AUG, N_HEADS * AUG, N_HEADS * AUG, SSM_INNER, CONV_DIM, LANES)
    out_dt = (BF16, BF16, BF16, F32, F32, F32)
    return pl.pallas_call(
        _odd_proj_kernel,
        grid=(B, S // tm),
        in_specs=[tok(D), row(D), _resident(w.shape), row(Q_LORA), row(KV_LORA),
                  _resident(wuq.shape), _resident(wukv.shape), row(AUG), row(AUG),
                  tok(LANES), tok(LANES)],
        out_specs=[tok(c) for c in out_cols],
        out_shape=[jax.ShapeDtypeStruct((B, S, c), dt) for c, dt in zip(out_cols, out_dt)],
        compiler_params=_cparams("arbitrary", "arbitrary"),
        name="odd_proj",
    )(x, g, w, cqn, ckvn, wuq, wukv, qnc, knc, cos, sin)


def _ssd_kernel(xbc_ref, z_ref, dt_ref, ex_ref, cw_ref, cb_ref, dtb_ref, alog_ref, dskip_ref,
                gn_ref, y_ref, xe_ref, st_ref):
    q = xbc_ref.shape[1]
    halo = 8

    @pl.when(pl.program_id(1) == 0)
    def _():
        xe_ref[0:halo, :] = jnp.zeros((halo, CONV_DIM), F32)
        st_ref[...] = jnp.zeros_like(st_ref)

    xe_ref[halo:halo + q, :] = xbc_ref[0]
    conv = cb_ref[...]
    for w in range(CONV_WIDTH):
        off = halo - (CONV_WIDTH - 1) + w
        conv = conv + xe_ref[off:off + q, :] * cw_ref[w:w + 1, :]
    xe_ref[0:halo, :] = xe_ref[q:q + halo, :]
    act = conv * _sigmoid(conv)
    xs = act[:, :SSM_INNER]
    gs = SSM_STATE
    bm = [act[:, SSM_INNER + g * gs:SSM_INNER + (g + 1) * gs] for g in range(SSM_GROUPS)]
    cm = [act[:, SSM_INNER + (SSM_GROUPS + g) * gs:SSM_INNER + (SSM_GROUPS + g + 1) * gs]
          for g in range(SSM_GROUPS)]

    lane = _lane_iota((q, LANES))
    dt = _softplus(dt_ref[0] + dtb_ref[...])
    a = jnp.where(lane < SSM_HEADS, dt * (-jnp.exp(alog_ref[...])), 0.0)
    tri = _tri(q)
    trif = tri.astype(F32)
    acum = _dot_hi(trif, a)
    acum_t = lax.dot_general(a.T, trif, (((1,), (1,)), ((), ())), preferred_element_type=F32,
                             precision=lax.Precision.HIGHEST)
    last = acum[q - 1:q, :]
    dte = jnp.exp(last - acum)
    eac = jnp.exp(acum)

    stack = jnp.concatenate([dt, dte, eac], axis=0)
    s_hi = stack.astype(BF16)
    s_lo = (stack - s_hi.astype(F32)).astype(BF16)
    wide = _dot(jnp.concatenate([s_hi, s_lo], axis=1), ex_ref[...])
    dt_x, dte_x, eac_x = wide[0:q], wide[q:2 * q], wide[2 * q:3 * q]
    cdec_x = eac_x[q - 1:q, :]

    xd = xs * dt_x
    xdd = (xd * dte_x).astype(BF16)
    xd16 = xd.astype(BF16)
    zero16 = jnp.zeros((q, LANES), BF16)

    rpg = SSM_HEADS // SSM_GROUPS
    gw = rpg * SSM_HEAD_DIM
    y_parts = []
    new_states = []
    for g in range(SSM_GROUPS):
        b16 = bm[g].astype(BF16)
        c16 = cm[g].astype(BF16)
        cb = _dot_nt(c16, b16)
        st_old = st_ref[:, g * gw:(g + 1) * gw]
        y_off = _dot(c16, st_old.astype(BF16)) * eac_x[:, g * gw:(g + 1) * gw]
        y_diag = []
        for pp in range(rpg // 2):
            p = g * (rpg // 2) + pp
            xp = xd16[:, p * LANES:(p + 1) * LANES]
            acc = None
            for e in range(2):
                h = 2 * p + e
                diff = acum[:, h:h + 1] - acum_t[h:h + 1, :]
                lm = jnp.exp(jnp.where(tri, diff, NEG_INF))
                gmat = (cb * lm).astype(BF16)
                keep = (lane < SSM_HEAD_DIM) if e == 0 else (lane >= SSM_HEAD_DIM)
                part = _dot(gmat, jnp.where(keep, xp, zero16))
                acc = part if acc is None else acc + part
            y_diag.append(acc)
        y_parts.append(jnp.concatenate(y_diag, axis=1) + y_off)
        st_new = _dot(bm[g].T.astype(BF16), xdd[:, g * gw:(g + 1) * gw])
        new_states.append(st_new)
    for g in range(SSM_GROUPS):
        sl = slice(g * gw, (g + 1) * gw)
        st_ref[:, sl] = st_ref[:, sl] * cdec_x[:, sl] + new_states[g]

    y = jnp.concatenate(y_parts, axis=1) + dskip_ref[...] * xs
    zz = z_ref[0]
    y = y * (zz * _sigmoid(zz))
    for g in range(SSM_GROUPS):
        sl = slice(g * gw, (g + 1) * gw)
        y_ref[0, :, sl] = _rms(y[:, sl], gn_ref[:, sl]).astype(y_ref.dtype)


def _ssd(xbc, z, dt, cw, cb, dtb, alog, dskip, gn):
    B, S, _ = xbc.shape
    q = SSD_CHUNK
    tok = lambda c: pl.BlockSpec((1, q, c), lambda b, j: (b, j, 0))
    row = lambda r, c: pl.BlockSpec((r, c), lambda b, j: (0, 0))
    head = jnp.arange(LANES)[:, None]
    chan = jnp.arange(SSM_INNER)[None, :] // SSM_HEAD_DIM
    ex = jnp.tile((head == chan).astype(BF16), (2, 1))
    return pl.pallas_call(
        _ssd_kernel,
        grid=(B, S // q),
        in_specs=[tok(CONV_DIM), tok(SSM_INNER), tok(LANES), row(*ex.shape),
                  row(CONV_WIDTH, CONV_DIM), row(1, CONV_DIM), row(1, LANES), row(1, LANES),
                  row(1, SSM_INNER), row(1, SSM_INNER)],
        out_specs=tok(SSM_INNER),
        out_shape=jax.ShapeDtypeStruct((B, S, SSM_INNER), BF16),
        scratch_shapes=[pltpu.VMEM((q + 8, CONV_DIM), F32), pltpu.VMEM((SSM_STATE, SSM_INNER), F32)],
        compiler_params=_cparams("arbitrary", "arbitrary"),
        name="ssd_scan",
    )(xbc, z, dt, ex, cw, cb, dtb, alog, dskip, gn)


def _out_mlp_kernel(x_ref, o1_ref, o2_ref, wo_ref, g_ref, w1_ref, w2_ref, y_ref,
                    xn_ref, acc_ref, *, tf):
    k1 = o1_ref.shape[1]
    xnew = x_ref[...] + _dot(o1_ref[...], wo_ref[:k1, :]) + _dot(o2_ref[...], wo_ref[k1:, :])
    xn_ref[...] = _rms(xnew, g_ref[...]).astype(BF16)
    acc_ref[...] = xnew

    def body(c, carry):
        cs = pl.multiple_of(c * tf, tf)
        h = _dot(xn_ref[...], w1_ref[:, pl.ds(cs, tf)])
        a = jnp.square(jnp.maximum(h, 0.0)).astype(BF16)
        acc_ref[...] += _dot(a, w2_ref[pl.ds(cs, tf), :])
        return carry

    lax.fori_loop(0, w1_ref.shape[1] // tf, body, 0)
    y_ref[...] = acc_ref[...]


def _out_mlp(x, o1, o2, wo, g, w1, w2):
    T, D = x.shape
    tm = TM_MLP
    assert wo.shape[0] == o1.shape[1] + o2.shape[1]
    tok = lambda c: pl.BlockSpec((tm, c), lambda i: (i, 0))
    return pl.pallas_call(
        functools.partial(_out_mlp_kernel, tf=TF_MLP),
        grid=(T // tm,),
        in_specs=[tok(D), tok(o1.shape[1]), tok(o2.shape[1]), _resident(wo.shape),
                  pl.BlockSpec((1, D), lambda i: (0, 0)), _resident(w1.shape), _resident(w2.shape)],
        out_specs=tok(D),
        out_shape=jax.ShapeDtypeStruct((T, D), F32),
        scratch_shapes=[pltpu.VMEM((tm, D), BF16), pltpu.VMEM((tm, D), F32)],
        compiler_params=_cparams("arbitrary"),
        name="out_mlp",
    )(x, o1, o2, wo, g, w1, w2)


def _pack_even_w(w):
    w = w.astype(BF16)
    qa, ka, va, qi, ki, wi, qb, kb, vb, fb = jnp.split(
        w, [512, 640, 768, 1280, 1344, 1352, 1864, 2376, 2888], axis=-1)
    pad = jnp.zeros(w.shape[:-1] + (LANES - IDX_DIM - N_IDX_HEADS - N_HEADS,), w.dtype)
    return jnp.concatenate([qa, ka, va, qi, qb, kb, vb, ki, wi, fb, pad], axis=-1)


def _pack_odd_w(w):
    w = w.astype(BF16)
    cq, ckv, kr, z, xbc, dt = jnp.split(w, [384, 640, 704, 1728, 3776], axis=-1)
    zpad = lambda n: jnp.zeros(w.shape[:-1] + (n,), w.dtype)
    return jnp.concatenate([cq, ckv, kr, zpad(LANES - QK_ROPE), dt, zpad(LANES - SSM_HEADS), z, xbc],
                           axis=-1)


def _pack_wuq(w):
    n_qk = QK_NOPE + QK_ROPE
    w = w.reshape(w.shape[0], N_HEADS, n_qk)
    w = jnp.pad(w, ((0, 0), (0, 0), (0, AUG - n_qk)))
    return w.reshape(w.shape[0], N_HEADS * AUG).astype(BF16)


def _row(v, width=None):
    v = v.astype(F32)[None, :]
    if width is not None and width > v.shape[1]:
        v = jnp.pad(v, ((0, 0), (0, width - v.shape[1])))
    return v


def kernel(x, positions, ev_norm, ev_w_in, ev_b_f, ev_qn_a, ev_kn_a, ev_qn_b, ev_kn_b, ev_w_out,
           od_norm, od_w_in, od_cq_norm, od_ckv_norm, od_w_uq, od_w_ukv, od_qn_c, od_kn_c,
           od_conv_w, od_conv_b, od_dt_bias, od_a_log, od_d_skip, od_gate_norm, od_w_out,
           mlp_norm, mlp_w1, mlp_w2):
    B, S, D = x.shape
    depth = mlp_w1.shape[0]
    ev_w, od_w = _pack_even_w(ev_w_in), _pack_odd_w(od_w_in)
    ev_wo, od_wo = ev_w_out.astype(BF16), od_w_out.astype(BF16)
    w1, w2 = mlp_w1.astype(BF16), mlp_w2.astype(BF16)
    cos = sin = None
    for layer in range(depth):
        i = layer // 2
        if layer % 2 == 0:
            bf = jnp.zeros((1, LANES), F32).at[0, MISC_F:MISC_F + N_HEADS].set(ev_b_f[i])
            qa, ka, va, qi, kd, wi, qb, kb, vb = _even_proj(
                x, _row(ev_norm[i]), ev_w[i], bf, _row(ev_qn_a[i]),
                _row(ev_kn_a[i]), _row(ev_qn_b[i]), _row(ev_kn_b[i]))
            o1 = _dsa(qa, ka, va, qi, kd, wi)
            o2 = _flash(qb, kb, vb)
            wo = ev_wo[i]
        else:
            if cos is None:
                cos, sin = _rope_tables(positions)
            q, k, v, z, xbc, dt = _odd_proj(
                x, _row(od_norm[i]), od_w[i], _row(od_cq_norm[i]),
                _row(od_ckv_norm[i]), _pack_wuq(od_w_uq[i]), od_w_ukv[i].astype(BF16),
                _row(od_qn_c[i], AUG), _row(od_kn_c[i], AUG), cos, sin)
            o1 = _flash(q, k, v)
            o2 = _ssd(xbc, z, dt, od_conv_w[i].astype(F32), _row(od_conv_b[i]),
                      _row(od_dt_bias[i], LANES), _row(od_a_log[i], LANES),
                      _row(jnp.repeat(od_d_skip[i], SSM_HEAD_DIM)), _row(od_gate_norm[i]))
            wo = od_wo[i]
        x = _out_mlp(x.reshape(B * S, D), o1.reshape(B * S, -1), o2.reshape(B * S, -1),
                     wo, _row(mlp_norm[layer]), w1[layer], w2[layer]).reshape(B, S, D)
    return x
```

```python
import functools

import jax
import jax.numpy as jnp
from jax import lax
from jax.experimental import pallas as pl
from jax.experimental.pallas import tpu as pltpu

F32 = jnp.float32
BF16 = jnp.bfloat16
EPS = 1e-6
NEG_INF = float("-inf")
F32_MAX = 3.4028234663852886e38
LOG2E = 1.4426950408889634

D_MODEL = 1024
HEAD_DIM = 128
N_HEADS = 4
N_IDX_HEADS = 8
IDX_DIM = 64
TOPK_MAX = 256
QK_NOPE = 128
QK_ROPE = 64
Q_LORA = 384
KV_LORA = 256
ROPE_THETA = 10000.0
SSM_INNER = 1024
SSM_HEAD_DIM = 64
SSM_HEADS = 16
SSM_GROUPS = 4
SSM_STATE = 128
CONV_WIDTH = 4
CONV_DIM = SSM_INNER + 2 * SSM_GROUPS * SSM_STATE
D_FF = 4 * D_MODEL

LANES = 128
AUG = 2 * LANES
VMEM_LIMIT = 56 * 1024 * 1024

TM_PROJ = 512
TM_EVEN = 1024
CUMSUM_BLOCK = 512
TM_MLP = 1024
TF_MLP = 1024
TQ_FLASH = 1024
TK_FLASH = 1024
TQ_DSA = 512
TK_DSA = 512
DSA_HEAD_GROUP = 2
SSD_CHUNK = 128
BISECT_GROUP = 4
BISECT_ROUND = 24
BISECT_MAX_ROUNDS = 8

EV_QA, EV_KA, EV_VA, EV_QI, EV_QB, EV_KB, EV_VB, EV_MISC, EV_END = (
    0, 512, 640, 768, 1280, 1792, 2304, 2816, 2944)
MISC_W = 64
MISC_F = 72
OD_CQ, OD_CKV, OD_KR, OD_DT, OD_Z, OD_XBC, OD_END = (0, 384, 640, 768, 896, 1920, 3968)


def _cparams(*sem):
    return pltpu.CompilerParams(dimension_semantics=sem, vmem_limit_bytes=VMEM_LIMIT)


def _resident(shape):
    nd = len(shape)
    return pl.BlockSpec(shape, lambda *_: (0,) * nd, pipeline_mode=pl.Buffered(1))


def _rms(x, g, n=None):
    n = x.shape[-1] if n is None else n
    ss = jnp.sum(x * x, axis=-1, keepdims=True)
    return x * lax.rsqrt(ss * (1.0 / n) + EPS) * g


def _dot(a, b):
    return jnp.dot(a, b, preferred_element_type=F32)


def _dot_nt(a, b):
    return lax.dot_general(a, b, (((1,), (1,)), ((), ())), preferred_element_type=F32)


def _dot_hi(a, b):
    return jnp.dot(a, b, preferred_element_type=F32, precision=lax.Precision.HIGHEST)


def _sigmoid(x):
    return 1.0 / (1.0 + jnp.exp(-x))


def _softplus(x):
    return jnp.maximum(x, 0.0) + jnp.log(1.0 + jnp.exp(-jnp.abs(x)))


def _lane_iota(shape):
    return lax.broadcasted_iota(jnp.int32, shape, len(shape) - 1)


def _tri(n):
    r = lax.broadcasted_iota(jnp.int32, (n, n), 0)
    c = lax.broadcasted_iota(jnp.int32, (n, n), 1)
    return r >= c


def _split3(c):
    hi = c.astype(BF16).astype(F32)
    mid = (c - hi).astype(BF16).astype(F32)
    lo = (c - hi - mid).astype(BF16).astype(F32)
    return hi, mid, lo


def _even_proj_kernel(x_ref, g_ref, w_ref, bf_ref, qna_ref, kna_ref, qnb_ref, knb_ref,
                      qa_ref, ka_ref, va_ref, qi_ref, kd_ref, wi_ref, qb_ref, kb_ref, vb_ref,
                      carry_ref):
    tm = x_ref.shape[1]

    @pl.when(pl.program_id(1) == 0)
    def _():
        carry_ref[...] = jnp.zeros_like(carry_ref)

    xn = _rms(x_ref[0], g_ref[...]).astype(BF16)

    def proj(lo, hi):
        return _dot(xn, w_ref[:, lo:hi])

    scale = HEAD_DIM ** -0.5 * LOG2E
    lane = _lane_iota((tm, LANES))
    ones_col = jnp.where(lane == 0, 1.0, 0.0).astype(BF16)
    qa = proj(EV_QA, EV_KA)
    for h in range(N_HEADS):
        sl = slice(h * HEAD_DIM, (h + 1) * HEAD_DIM)
        qa_ref[0, :, sl] = (_rms(qa[:, sl], qna_ref[...]) * scale).astype(BF16)
    ka_ref[0] = _rms(proj(EV_KA, EV_VA), kna_ref[...]).astype(BF16)
    va_ref[0, :, :HEAD_DIM] = proj(EV_VA, EV_QI).astype(BF16)
    va_ref[0, :, HEAD_DIM:] = ones_col
    qi_ref[0] = proj(EV_QI, EV_QB).astype(BF16)
    vb = proj(EV_VB, EV_MISC)
    for h in range(N_HEADS):
        vb_ref[0, :, h * AUG:h * AUG + HEAD_DIM] = vb[:, h * HEAD_DIM:(h + 1) * HEAD_DIM].astype(BF16)
        vb_ref[0, :, h * AUG + HEAD_DIM:(h + 1) * AUG] = ones_col

    misc = proj(EV_MISC, EV_END)
    kd_ref[0] = jnp.where(lane < IDX_DIM, misc, pltpu.roll(misc, IDX_DIM, 1)).astype(BF16)
    wi_ref[0] = misc.T[MISC_W:MISC_W + N_IDX_HEADS, :]

    zf = misc + bf_ref[...]
    logf = jnp.minimum(zf, 0.0) - jnp.log(1.0 + jnp.exp(-jnp.abs(zf)))
    logf = jnp.where((lane >= MISC_F) & (lane < MISC_F + N_HEADS), logf, 0.0)
    cb = min(tm, CUMSUM_BLOCK)
    tri = _tri(cb).astype(BF16)
    carry = carry_ref[...]
    blocks = []
    for r in range(tm // cb):
        parts = _split3(logf[r * cb:(r + 1) * cb])
        blocks.append(sum(_dot(tri, part.astype(BF16)) for part in parts) + carry)
        carry = blocks[-1][cb - 1:cb, :]
    csum = jnp.concatenate(blocks, axis=0)
    carry_ref[...] = carry

    qb = proj(EV_QB, EV_KB)
    kb = proj(EV_KB, EV_VB)
    one = jnp.ones((tm, LANES), F32)
    zero = jnp.zeros((tm, LANES), F32)
    for h in range(N_HEADS):
        sl = slice(h * HEAD_DIM, (h + 1) * HEAD_DIM)
        c = jnp.broadcast_to(csum[:, MISC_F + h:MISC_F + h + 1], (tm, LANES)) * LOG2E
        hi, mid, lo = _split3(c)
        aq = jnp.where(lane == 0, hi, jnp.where(lane == 1, mid, jnp.where(lane == 2, lo,
             jnp.where(lane < 6, one, zero))))
        ak = jnp.where(lane < 3, one, jnp.where(lane == 3, -hi, jnp.where(lane == 4, -mid,
             jnp.where(lane == 5, -lo, zero))))
        qb_ref[0, :, h * AUG:h * AUG + LANES] = (_rms(qb[:, sl], qnb_ref[...]) * scale).astype(BF16)
        qb_ref[0, :, h * AUG + LANES:(h + 1) * AUG] = aq.astype(BF16)
        kb_ref[0, :, h * AUG:h * AUG + LANES] = _rms(kb[:, sl], knb_ref[...]).astype(BF16)
        kb_ref[0, :, h * AUG + LANES:(h + 1) * AUG] = ak.astype(BF16)


def _even_proj(x, g, w, bf, qna, kna, qnb, knb):
    B, S, D = x.shape
    tm = min(TM_EVEN, S)
    tok = lambda c: pl.BlockSpec((1, tm, c), lambda b, j: (b, j, 0))
    row = lambda c: pl.BlockSpec((1, c), lambda b, j: (0, 0))
    out_cols = (512, 128, AUG, 512, 128, None, N_HEADS * AUG, N_HEADS * AUG, N_HEADS * AUG)
    out_specs = [tok(c) if c else pl.BlockSpec((1, N_IDX_HEADS, tm), lambda b, j: (b, 0, j))
                 for c in out_cols]
    out_shape = [jax.ShapeDtypeStruct((B, S, c), BF16) if c else
                 jax.ShapeDtypeStruct((B, N_IDX_HEADS, S), F32) for c in out_cols]
    return pl.pallas_call(
        _even_proj_kernel,
        grid=(B, S // tm),
        in_specs=[tok(D), row(D), _resident(w.shape), row(LANES), row(LANES), row(LANES),
                  row(LANES), row(LANES)],
        out_specs=out_specs,
        out_shape=out_shape,
        scratch_shapes=[pltpu.VMEM((1, LANES), F32)],
        compiler_params=_cparams("arbitrary", "arbitrary"),
        name="even_proj",
    )(x, g, w, bf, qna, kna, qnb, knb)


def _fold_max(mx, s):
    for g in range(s.shape[1] // LANES):
        mx = jnp.maximum(mx, s[:, g * LANES:(g + 1) * LANES])
    return mx


def _flash_kernel(q_ref, k_ref, v_ref, o_ref, lg_ref, acc_ref, *, tq, tk):
    i = pl.program_id(2)
    q = q_ref[0]
    half = tq // 2

    def kchunk(ref, j):
        return ref[0, pl.ds(pl.multiple_of(j * tk, tk), tk), :]

    def pass_a(j, mx):
        s = _dot_nt(q, kchunk(k_ref, j))
        lg_ref[j] = s
        return _fold_max(mx, s)

    mx = lax.fori_loop(0, i, pass_a, jnp.full((tq, LANES), NEG_INF, F32))
    kd = kchunk(k_ref, i)
    tri = _tri(half)
    s_top = jnp.where(tri, _dot_nt(q[:half], kd[:half]), NEG_INF)
    s_bot = _dot_nt(q[half:], kd)
    s_bot = jnp.concatenate([s_bot[:, :half], jnp.where(tri, s_bot[:, half:], NEG_INF)], axis=1)
    lg_ref[i, :half, :half] = s_top
    lg_ref[i, half:, :] = s_bot
    mx = jnp.concatenate([_fold_max(mx[:half], s_top), _fold_max(mx[half:], s_bot)], axis=0)
    m = jnp.max(mx, axis=-1, keepdims=True)

    acc_ref[...] = jnp.zeros_like(acc_ref)

    def pass_b(j, c):
        p = jnp.exp2(lg_ref[j] - m).astype(BF16)
        acc_ref[...] += _dot(p, kchunk(v_ref, j))
        return c

    lax.fori_loop(0, i, pass_b, 0)
    vd = kchunk(v_ref, i)
    p_top = jnp.exp2(lg_ref[i, :half, :half] - m[:half]).astype(BF16)
    p_bot = jnp.exp2(lg_ref[i, half:, :] - m[half:]).astype(BF16)
    acc_top = acc_ref[:half, :] + _dot(p_top, vd[:half])
    acc_bot = acc_ref[half:, :] + _dot(p_bot, vd)
    o_ref[0, :half, :] = (acc_top[:, :HEAD_DIM] / acc_top[:, HEAD_DIM:HEAD_DIM + 1]).astype(o_ref.dtype)
    o_ref[0, half:, :] = (acc_bot[:, :HEAD_DIM] / acc_bot[:, HEAD_DIM:HEAD_DIM + 1]).astype(o_ref.dtype)


def _flash(q, k, v):
    B, S, _ = q.shape
    tq, tk = min(TQ_FLASH, S), min(TK_FLASH, S)
    assert tq == tk, "the diagonal chunk is split on the query block's own key chunk"
    return pl.pallas_call(
        functools.partial(_flash_kernel, tq=tq, tk=tk),
        grid=(B, N_HEADS, S // tq),
        in_specs=[pl.BlockSpec((1, tq, AUG), lambda b, h, i: (b, i, h)),
                  pl.BlockSpec((1, S, AUG), lambda b, h, i: (b, 0, h)),
                  pl.BlockSpec((1, S, AUG), lambda b, h, i: (b, 0, h))],
        out_specs=pl.BlockSpec((1, tq, HEAD_DIM), lambda b, h, i: (b, i, h)),
        out_shape=jax.ShapeDtypeStruct((B, S, N_HEADS * HEAD_DIM), BF16),
        scratch_shapes=[pltpu.VMEM((S // tk, tq, tk), F32), pltpu.VMEM((tq, AUG), F32)],
        compiler_params=_cparams("arbitrary", "arbitrary", "arbitrary"),
        name="flash_attn",
    )(q, k, v)


def _dsa_kernel(qa_ref, ka_ref, va_ref, qi_ref, kd_ref, wi_ref, o_ref,
                sc_ref, qstk_ref, lg_ref, acc_ref, *, tkc, topk):
    tq = qa_ref.shape[1]
    i = pl.program_id(1)
    nkc = (i * tq + tq + tkc - 1) // tkc
    kf = float(topk)
    lane = _lane_iota((tq, LANES))
    keyi = lax.broadcasted_iota(jnp.int32, (tkc, tq), 0)
    qpos = i * tq + lax.broadcasted_iota(jnp.int32, (tkc, tq), 1)

    q8 = qi_ref[0]
    for j in range(N_IDX_HEADS):
        grp = q8[:, LANES * (j // 2):LANES * (j // 2 + 1)]
        keep = (lane < IDX_DIM) if j % 2 == 0 else (lane >= IDX_DIM)
        qstk_ref[tq * j:tq * (j + 1), :] = jnp.where(keep, grp, jnp.zeros_like(grp))
    wrows = [wi_ref[0, j:j + 1, :] for j in range(N_IDX_HEADS)]

    def chunk(ref, kc):
        return ref[0, pl.ds(pl.multiple_of(kc * tkc, tkc), tkc), :]

    def fold8(x, op):
        cols = []
        for g in range(x.shape[1] // LANES):
            xg = x[:, g * LANES:(g + 1) * LANES]
            rows = xg.shape[0]
            while rows > 8:
                rows //= 2
                xg = op(xg[:rows], xg[rows:])
            cols.append(xg)
        return jnp.concatenate(cols, axis=1)

    def p1(kc, carry, diagonal=False):
        rmax, rmin, npos, nnonneg = carry
        s8 = _dot_nt(chunk(kd_ref, kc), qstk_ref[...])
        score = wrows[0] * jnp.maximum(s8[:, 0:tq], 0.0)
        for j in range(1, N_IDX_HEADS):
            score = score + wrows[j] * jnp.maximum(s8[:, tq * j:tq * (j + 1)], 0.0)
        low = high = score
        if diagonal:
            adm = (kc * tkc + keyi) <= qpos
            low = jnp.where(adm, score, NEG_INF)
            high = jnp.where(adm, score, -NEG_INF)
        sc_ref[kc] = low
        rmax = jnp.maximum(rmax, fold8(low, jnp.maximum))
        rmin = jnp.minimum(rmin, fold8(high, jnp.minimum))
        npos = npos + fold8(jnp.where(low > 0.0, 1.0, 0.0), jnp.add)
        nnonneg = nnonneg + fold8(jnp.where(low >= 0.0, 1.0, 0.0), jnp.add)
        return rmax, rmin, npos, nnonneg

    zero8 = jnp.zeros((8, tq), F32)
    carry = lax.fori_loop(0, nkc - 1, p1, (jnp.full((8, tq), NEG_INF, F32),
                                           jnp.full((8, tq), -NEG_INF, F32), zero8, zero8))
    rmax, rmin, npos, nnonneg = p1(nkc - 1, carry, diagonal=True)
    rmax = jnp.max(rmax, axis=0, keepdims=True)
    rmin = jnp.min(rmin, axis=0, keepdims=True)
    c_gt0 = jnp.sum(npos, axis=0, keepdims=True)
    c_ge0 = jnp.sum(nnonneg, axis=0, keepdims=True)

    def row_count(pred):
        def cb(kc, c):
            return c + fold8(jnp.where(pred(sc_ref[kc], kc), 1.0, 0.0), jnp.add)
        return jnp.sum(lax.fori_loop(0, nkc, cb, zero8), axis=0, keepdims=True)

    def any_row(flag):
        return jnp.max(flag.astype(jnp.int32))

    def select_threshold():
        hi0 = rmax + (jnp.abs(rmax) * 1.2e-7 + 1e-37)
        n_adm = (qpos[0:1, :] + 1).astype(F32)
        above = c_ge0 >= kf
        pend0 = (above & (c_gt0 < kf)).astype(jnp.int32)
        lo0 = jnp.where(above, 0.0, rmin)
        cnt0 = jnp.where(above, c_ge0, n_adm)
        hi0 = jnp.where(above, hi0, 0.0)
        few = n_adm <= kf
        lo0 = jnp.where(few, -F32_MAX, lo0)
        cnt0 = jnp.where(few, kf, cnt0)
        pend0 = jnp.where(few, 0, pend0)

        def unresolved(cnt, pend):
            return any_row((cnt > kf + 1.0) & (pend == 0))

        def fold_two_min(x):
            c1, c2 = [], []
            for g in range(tq // LANES):
                xg = x[:, g * LANES:(g + 1) * LANES]
                rows = tkc // 2
                m1 = jnp.minimum(xg[:rows], xg[rows:])
                m2 = jnp.maximum(xg[:rows], xg[rows:])
                while rows > 8:
                    rows //= 2
                    a1, b1, a2, b2 = m1[:rows], m1[rows:], m2[:rows], m2[rows:]
                    m1, m2 = (jnp.minimum(a1, b1),
                              jnp.minimum(jnp.maximum(a1, b1), jnp.minimum(a2, b2)))
                c1.append(m1)
                c2.append(m2)
            return jnp.concatenate(c1, axis=1), jnp.concatenate(c2, axis=1)

        def snap(lo, cnt, pend):
            def two_min(kc, st):
                m1, m2 = st
                blk = sc_ref[kc]
                a1, a2 = fold_two_min(jnp.where(blk >= lo, blk, -NEG_INF))
                return (jnp.minimum(m1, a1),
                        jnp.minimum(jnp.maximum(m1, a1), jnp.minimum(m2, a2)))
            inf8 = jnp.full((8, tq), -NEG_INF, F32)
            m1, m2 = lax.fori_loop(0, nkc, two_min, (inf8, inf8))
            a1 = jnp.min(m1, axis=0, keepdims=True)
            at_min = m1 == a1
            others = jnp.minimum(jnp.min(jnp.where(at_min, -NEG_INF, m1), axis=0, keepdims=True),
                                 jnp.min(m2, axis=0, keepdims=True))
            n_min = jnp.sum(jnp.where(at_min, 1.0, 0.0), axis=0, keepdims=True)
            a2 = jnp.where(n_min > 1.0, a1, others)
            ok = (cnt == kf + 1.0) & (pend == 0) & (a2 > a1)
            return jnp.where(ok, a2, lo), jnp.where(ok, kf, cnt)

        def bis_cond(st):
            return (st[-1] > 0) & (st[0] < BISECT_ROUND)

        def bis_body(st):
            it, lo, hi, cnt, pend, _ = st
            for _ in range(BISECT_GROUP):
                mid = lo + (hi - lo) * 0.5
                c = row_count(lambda blk, kc, mid=mid: blk >= mid)
                up = c >= kf
                lo = jnp.where(up, mid, lo)
                hi = jnp.where(up, hi, mid)
                cnt = jnp.where(up, c, cnt)
            return it + BISECT_GROUP, lo, hi, cnt, pend, unresolved(cnt, pend)

        def resolve_ties(lo, cnt):
            def vmin(kc, v):
                blk = sc_ref[kc]
                return jnp.minimum(v, fold8(jnp.where(blk >= lo, blk, -NEG_INF), jnp.minimum))
            v = lax.fori_loop(0, nkc, vmin, jnp.full((8, tq), -NEG_INF, F32))
            v = jnp.min(v, axis=0, keepdims=True)
            cgt = row_count(lambda blk, kc: blk > v)
            tie = (cnt != kf) & (cgt < kf)
            need = kf - cgt
            tri = _tri(tkc).astype(BF16)

            def drop(kc, seen):
                blk = sc_ref[kc]
                eq = tie & (blk == v)
                rank = _dot(tri, jnp.where(eq, 1.0, 0.0).astype(BF16)) + seen
                sc_ref[kc] = jnp.where(eq & (rank > need), NEG_INF, blk)
                return rank[tkc - 1:tkc, :]
            lax.fori_loop(0, nkc, drop, jnp.zeros((1, tq), F32))
            return jnp.where(tie, v, lo), jnp.where(tie, kf, cnt)

        def round_cond(st):
            return (st[-1] > 0) & (st[0] < BISECT_MAX_ROUNDS)

        def round_body(st):
            r, lo, hi, cnt, pend, _ = st
            _, lo, hi, cnt, pend, _ = lax.while_loop(
                bis_cond, bis_body, (0, lo, hi, cnt, pend, unresolved(cnt, pend)))
            lo, cnt = lax.cond(any_row((cnt == kf + 1.0) & (pend == 0)) > 0, snap,
                               lambda a, b, c: (a, b), lo, cnt, pend)
            lo, cnt = lax.cond(any_row(cnt != kf) > 0, resolve_ties, lambda a, b: (a, b), lo, cnt)
            return r + 1, lo, hi, cnt, jnp.zeros_like(pend), any_row(cnt != kf)

        st = lax.while_loop(round_cond, round_body,
                            (0, lo0, hi0, cnt0, pend0, any_row(cnt0 != kf)))
        return st[1]

    thr = lax.cond((i + 1) * tq > topk, select_threshold,
                   lambda: jnp.full((1, tq), -F32_MAX, F32))

    qa = qa_ref[0]
    hg = DSA_HEAD_GROUP
    for g0 in range(0, N_HEADS, hg):
        qs = jnp.concatenate([qa[:, h * HEAD_DIM:(h + 1) * HEAD_DIM] for h in range(g0, g0 + hg)],
                             axis=0)

        def p3a(kc, mx, qs=qs, first=(g0 == 0)):
            s = _dot_nt(qs, chunk(ka_ref, kc))
            if first:
                sel = jnp.where(sc_ref[kc] >= thr, 0.0, NEG_INF).T
                sc_ref[kc] = sel
            else:
                sel = sc_ref[kc]
            parts = []
            for h in range(hg):
                rs = slice(h * tq, (h + 1) * tq)
                sh = s[rs] + sel
                lg_ref[kc, rs, :] = sh
                parts.append(_fold_max(mx[rs], sh))
            return jnp.concatenate(parts, axis=0)

        mx = lax.fori_loop(0, nkc, p3a, jnp.full((hg * tq, LANES), NEG_INF, F32))
        m = jnp.max(mx, axis=-1, keepdims=True)

        acc_ref[...] = jnp.zeros_like(acc_ref)

        def p3b(kc, c, m=m):
            p = jnp.exp2(lg_ref[kc] - m).astype(BF16)
            acc_ref[...] += _dot(p, chunk(va_ref, kc))
            return c

        lax.fori_loop(0, nkc, p3b, 0)
        for h in range(hg):
            acc = acc_ref[h * tq:(h + 1) * tq, :]
            o_ref[0, :, (g0 + h) * HEAD_DIM:(g0 + h + 1) * HEAD_DIM] = (
                acc[:, :HEAD_DIM] / acc[:, HEAD_DIM:HEAD_DIM + 1]).astype(o_ref.dtype)


def _dsa(qa, ka, va, qi, kd, wi):
    B, S, _ = qa.shape
    tq, tkc = TQ_DSA, min(TK_DSA, S)
    assert tq == tkc, "the transposed selection mask reuses the score scratch"
    topk = min(TOPK_MAX, S // 4)
    qblk = lambda c: pl.BlockSpec((1, tq, c), lambda b, i: (b, i, 0))
    full = lambda c: pl.BlockSpec((1, S, c), lambda b, i: (b, 0, 0))
    return pl.pallas_call(
        functools.partial(_dsa_kernel, tkc=tkc, topk=topk),
        grid=(B, S // tq),
        in_specs=[qblk(512), full(128), full(AUG), qblk(512), full(128),
                  pl.BlockSpec((1, N_IDX_HEADS, tq), lambda b, i: (b, 0, i))],
        out_specs=qblk(512),
        out_shape=jax.ShapeDtypeStruct((B, S, N_HEADS * HEAD_DIM), BF16),
        scratch_shapes=[pltpu.VMEM((S // tkc, tkc, tq), F32),
                        pltpu.VMEM((N_IDX_HEADS * tq, LANES), BF16),
                        pltpu.VMEM((S // tkc, DSA_HEAD_GROUP * tq, tkc), F32),
                        pltpu.VMEM((DSA_HEAD_GROUP * tq, AUG), F32)],
        compiler_params=_cparams("arbitrary", "arbitrary"),
        name="dsa_attn",
    )(qa, ka, va, qi, kd, wi)


def _rope_table_kernel(pos_ref, inv_ref, cos_ref, sin_ref):
    tm = pos_ref.shape[1]
    lane = _lane_iota((tm, LANES))
    ang = pos_ref[0].astype(F32) * inv_ref[...]
    half = QK_ROPE // 2
    cos_ref[0] = jnp.where(lane < QK_ROPE, jnp.cos(ang), 0.0)
    sn = jnp.sin(ang)
    sin_ref[0] = jnp.where(lane < half, -sn, jnp.where(lane < QK_ROPE, sn, 0.0))


def _rope_tables(positions):
    B, S = positions.shape
    tm = TM_PROJ
    half = QK_ROPE // 2
    inv = ROPE_THETA ** (-jnp.arange(half, dtype=F32) / half)
    inv = jnp.concatenate([inv, inv, jnp.zeros((LANES - QK_ROPE,), F32)])[None, :]
    return pl.pallas_call(
        _rope_table_kernel,
        grid=(B, S // tm),
        in_specs=[pl.BlockSpec((1, tm, 1), lambda b, j: (b, j, 0)),
                  pl.BlockSpec((1, LANES), lambda b, j: (0, 0))],
        out_specs=[pl.BlockSpec((1, tm, LANES), lambda b, j: (b, j, 0))] * 2,
        out_shape=[jax.ShapeDtypeStruct((B, S, LANES), F32)] * 2,
        compiler_params=_cparams("arbitrary", "arbitrary"),
        name="rope_tables",
    )(positions[:, :, None], inv)


def _rope(t, cos, sin):
    lane = _lane_iota(t.shape)
    half = QK_ROPE // 2
    swap = jnp.where(lane < half, pltpu.roll(t, LANES - half, 1), pltpu.roll(t, half, 1))
    return t * cos + swap * sin


def _odd_proj_kernel(x_ref, g_ref, w_ref, cqn_ref, ckvn_ref, wuq_ref, wukv_ref, qnc_ref, knc_ref,
                     cos_ref, sin_ref, q_ref, k_ref, v_ref, z_ref, xbc_ref, dt_ref):
    xn = _rms(x_ref[0], g_ref[...]).astype(BF16)

    def proj(lo, hi):
        return _dot(xn, w_ref[:, lo:hi])

    cos = cos_ref[0]
    sin = sin_ref[0]
    n_qk = QK_NOPE + QK_ROPE
    scale = n_qk ** -0.5 * LOG2E
    ones_col = jnp.where(_lane_iota(cos.shape) == 0, 1.0, 0.0).astype(BF16)

    cq = _rms(proj(OD_CQ, OD_CKV), cqn_ref[...]).astype(BF16)
    q = _dot(cq, wuq_ref[...])
    ckv = _rms(proj(OD_CKV, OD_KR), ckvn_ref[...]).astype(BF16)
    kv = _dot(ckv, wukv_ref[...])
    kr = proj(OD_KR, OD_DT)
    kr_ss = jnp.sum(kr * kr, axis=-1, keepdims=True)
    kr_rot = _rope(kr * knc_ref[:, LANES:], cos, sin)
    for h in range(N_HEADS):
        qh = q[:, h * AUG:(h + 1) * AUG]
        qh = _rms(qh, qnc_ref[...], n_qk) * scale
        q_ref[0, :, h * AUG:h * AUG + LANES] = qh[:, :LANES].astype(BF16)
        q_ref[0, :, h * AUG + LANES:(h + 1) * AUG] = _rope(qh[:, LANES:], cos, sin).astype(BF16)
        kn = kv[:, h * AUG:h * AUG + LANES]
        r = lax.rsqrt((jnp.sum(kn * kn, axis=-1, keepdims=True) + kr_ss) * (1.0 / n_qk) + EPS)
        k_ref[0, :, h * AUG:h * AUG + LANES] = (kn * r * knc_ref[:, :LANES]).astype(BF16)
        k_ref[0, :, h * AUG + LANES:(h + 1) * AUG] = (kr_rot * r).astype(BF16)
        v_ref[0, :, h * AUG:h * AUG + HEAD_DIM] = kv[:, h * AUG + LANES:(h + 1) * AUG].astype(BF16)
        v_ref[0, :, h * AUG + HEAD_DIM:(h + 1) * AUG] = ones_col

    dt_ref[0] = proj(OD_DT, OD_Z)
    z_ref[0] = proj(OD_Z, OD_XBC)
    xbc_ref[0] = proj(OD_XBC, OD_END)


def _odd_proj(x, g, w, cqn, ckvn, wuq, wukv, qnc, knc, cos, sin):
    B, S, D = x.shape
    tm = TM_PROJ
    tok = lambda c: pl.BlockSpec((1, tm, c), lambda b, j: (b, j, 0))
    row = lambda c: pl.BlockSpec((1, c), lambda b, j: (0, 0))
    out_cols = (N_HEADS * AUG, N_HEADS * AUG, N_HEADS * AUG, SSM_INNER, CONV_DIM, LANES)
    out_dt = (BF16, BF16, BF16, F32, F32, F32)
    return pl.pallas_call(
        _odd_proj_kernel,
        grid=(B, S // tm),
        in_specs=[tok(D), row(D), _resident(w.shape), row(Q_LORA), row(KV_LORA),
                  _resident(wuq.shape), _resident(wukv.shape), row(AUG), row(AUG),
                  tok(LANES), tok(LANES)],
        out_specs=[tok(c) for c in out_cols],
        out_shape=[jax.ShapeDtypeStruct((B, S, c), dt) for c, dt in zip(out_cols, out_dt)],
        compiler_params=_cparams("arbitrary", "arbitrary"),
        name="odd_proj",
    )(x, g, w, cqn, ckvn, wuq, wukv, qnc, knc, cos, sin)


def _ssd_kernel(xbc_ref, z_ref, dt_ref, ex_ref, cw_ref, cb_ref, dtb_ref, alog_ref, dskip_ref,
                gn_ref, y_ref, xe_ref, st_ref):
    q = xbc_ref.shape[1]
    halo = 8

    @pl.when(pl.program_id(1) == 0)
    def _():
        xe_ref[0:halo, :] = jnp.zeros((halo, CONV_DIM), F32)
        st_ref[...] = jnp.zeros_like(st_ref)

    xe_ref[halo:halo + q, :] = xbc_ref[0]
    ext = xe_ref[...]
    conv = cb_ref[...] + ext[halo:] * cw_ref[CONV_WIDTH - 1:CONV_WIDTH, :]
    for s in range(1, CONV_WIDTH):
        w = CONV_WIDTH - 1 - s
        conv = conv + pltpu.roll(ext, s, 0)[halo:] * cw_ref[w:w + 1, :]
    xe_ref[0:halo, :] = ext[q:q + halo]
    act = conv * _sigmoid(conv)
    xs = act[:, :SSM_INNER]
    gs = SSM_STATE
    bm = [act[:, SSM_INNER + g * gs:SSM_INNER + (g + 1) * gs] for g in range(SSM_GROUPS)]
    cm = [act[:, SSM_INNER + (SSM_GROUPS + g) * gs:SSM_INNER + (SSM_GROUPS + g + 1) * gs]
          for g in range(SSM_GROUPS)]

    lane = _lane_iota((q, LANES))
    dt = _softplus(dt_ref[0] + dtb_ref[...])
    a = jnp.where(lane < SSM_HEADS, dt * (-jnp.exp(alog_ref[...])), 0.0)
    tri = _tri(q)
    trif = tri.astype(F32)
    acum = _dot_hi(trif, a)
    acum_t = lax.dot_general(a.T, trif, (((1,), (1,)), ((), ())), preferred_element_type=F32,
                             precision=lax.Precision.HIGHEST)
    last = acum[q - 1:q, :]
    dte = jnp.exp(last - acum)
    eac = jnp.exp(acum)

    stack = jnp.concatenate([dt, dte, eac], axis=0)
    s_hi = stack.astype(BF16)
    s_lo = (stack - s_hi.astype(F32)).astype(BF16)
    wide = _dot(jnp.concatenate([s_hi, s_lo], axis=1), ex_ref[...])
    dt_x, dte_x, eac_x = wide[0:q], wide[q:2 * q], wide[2 * q:3 * q]
    cdec_x = eac_x[q - 1:q, :]

    xd = xs * dt_x
    xdd = (xd * dte_x).astype(BF16)
    xd16 = xd.astype(BF16)
    zero16 = jnp.zeros((q, LANES), BF16)

    rpg = SSM_HEADS // SSM_GROUPS
    gw = rpg * SSM_HEAD_DIM
    y_parts = []
    new_states = []
    for g in range(SSM_GROUPS):
        b16 = bm[g].astype(BF16)
        c16 = cm[g].astype(BF16)
        cb = _dot_nt(c16, b16)
        st_old = st_ref[:, g * gw:(g + 1) * gw]
        y_off = _dot(c16, st_old.astype(BF16)) * eac_x[:, g * gw:(g + 1) * gw]
        y_diag = []
        for pp in range(rpg // 2):
            p = g * (rpg // 2) + pp
            xp = xd16[:, p * LANES:(p + 1) * LANES]
            acc = None
            for e in range(2):
                h = 2 * p + e
                diff = acum[:, h:h + 1] - acum_t[h:h + 1, :]
                lm = jnp.exp(jnp.where(tri, diff, NEG_INF))
                gmat = (cb * lm).astype(BF16)
                keep = (lane < SSM_HEAD_DIM) if e == 0 else (lane >= SSM_HEAD_DIM)
                part = _dot(gmat, jnp.where(keep, xp, zero16))
                acc = part if acc is None else acc + part
            y_diag.append(acc)
        y_parts.append(jnp.concatenate(y_diag, axis=1) + y_off)
        st_new = _dot(bm[g].T.astype(BF16), xdd[:, g * gw:(g + 1) * gw])
        new_states.append(st_new)
    for g in range(SSM_GROUPS):
        sl = slice(g * gw, (g + 1) * gw)
        st_ref[:, sl] = st_ref[:, sl] * cdec_x[:, sl] + new_states[g]

    y = jnp.concatenate(y_parts, axis=1) + dskip_ref[...] * xs
    zz = z_ref[0]
    y = y * (zz * _sigmoid(zz))
    for g in range(SSM_GROUPS):
        sl = slice(g * gw, (g + 1) * gw)
        y_ref[0, :, sl] = _rms(y[:, sl], gn_ref[:, sl]).astype(y_ref.dtype)


def _ssd(xbc, z, dt, cw, cb, dtb, alog, dskip, gn):
    B, S, _ = xbc.shape
    q = SSD_CHUNK
    tok = lambda c: pl.BlockSpec((1, q, c), lambda b, j: (b, j, 0))
    row = lambda r, c: pl.BlockSpec((r, c), lambda b, j: (0, 0))
    head = jnp.arange(LANES)[:, None]
    chan = jnp.arange(SSM_INNER)[None, :] // SSM_HEAD_DIM
    ex = jnp.tile((head == chan).astype(BF16), (2, 1))
    return pl.pallas_call(
        _ssd_kernel,
        grid=(B, S // q),
        in_specs=[tok(CONV_DIM), tok(SSM_INNER), tok(LANES), row(*ex.shape),
                  row(CONV_WIDTH, CONV_DIM), row(1, CONV_DIM), row(1, LANES), row(1, LANES),
                  row(1, SSM_INNER), row(1, SSM_INNER)],
        out_specs=tok(SSM_INNER),
        out_shape=jax.ShapeDtypeStruct((B, S, SSM_INNER), BF16),
        scratch_shapes=[pltpu.VMEM((q + 8, CONV_DIM), F32), pltpu.VMEM((SSM_STATE, SSM_INNER), F32)],
        compiler_params=_cparams("arbitrary", "arbitrary"),
        name="ssd_scan",
    )(xbc, z, dt, ex, cw, cb, dtb, alog, dskip, gn)


def _out_mlp_kernel(x_ref, o1_ref, o2_ref, wo_ref, g_ref, w1_ref, w2_ref, y_ref,
                    xn_ref, acc_ref, *, tf):
    k1 = o1_ref.shape[1]
    xnew = x_ref[...] + _dot(o1_ref[...], wo_ref[:k1, :]) + _dot(o2_ref[...], wo_ref[k1:, :])
    xn_ref[...] = _rms(xnew, g_ref[...]).astype(BF16)
    acc_ref[...] = xnew

    def body(c, carry):
        cs = pl.multiple_of(c * tf, tf)
        h = _dot(xn_ref[...], w1_ref[:, pl.ds(cs, tf)])
        a = jnp.square(jnp.maximum(h, 0.0)).astype(BF16)
        acc_ref[...] += _dot(a, w2_ref[pl.ds(cs, tf), :])
        return carry

    lax.fori_loop(0, w1_ref.shape[1] // tf, body, 0)
    y_ref[...] = acc_ref[...]


def _out_mlp(x, o1, o2, wo, g, w1, w2):
    T, D = x.shape
    tm = TM_MLP
    assert wo.shape[0] == o1.shape[1] + o2.shape[1]
    tok = lambda c: pl.BlockSpec((tm, c), lambda i: (i, 0))
    return pl.pallas_call(
        functools.partial(_out_mlp_kernel, tf=TF_MLP),
        grid=(T // tm,),
        in_specs=[tok(D), tok(o1.shape[1]), tok(o2.shape[1]), _resident(wo.shape),
                  pl.BlockSpec((1, D), lambda i: (0, 0)), _resident(w1.shape), _resident(w2.shape)],
        out_specs=tok(D),
        out_shape=jax.ShapeDtypeStruct((T, D), F32),
        scratch_shapes=[pltpu.VMEM((tm, D), BF16), pltpu.VMEM((tm, D), F32)],
        compiler_params=_cparams("arbitrary"),
        name="out_mlp",
    )(x, o1, o2, wo, g, w1, w2)


def _pack_even_w(w):
    w = w.astype(BF16)
    qa, ka, va, qi, ki, wi, qb, kb, vb, fb = jnp.split(
        w, [512, 640, 768, 1280, 1344, 1352, 1864, 2376, 2888], axis=-1)
    pad = jnp.zeros(w.shape[:-1] + (LANES - IDX_DIM - N_IDX_HEADS - N_HEADS,), w.dtype)
    return jnp.concatenate([qa, ka, va, qi, qb, kb, vb, ki, wi, fb, pad], axis=-1)


def _pack_odd_w(w):
    w = w.astype(BF16)
    cq, ckv, kr, z, xbc, dt = jnp.split(w, [384, 640, 704, 1728, 3776], axis=-1)
    zpad = lambda n: jnp.zeros(w.shape[:-1] + (n,), w.dtype)
    return jnp.concatenate([cq, ckv, kr, zpad(LANES - QK_ROPE), dt, zpad(LANES - SSM_HEADS), z, xbc],
                           axis=-1)


def _pack_wuq(w):
    n_qk = QK_NOPE + QK_ROPE
    w = w.reshape(w.shape[0], N_HEADS, n_qk)
    w = jnp.pad(w, ((0, 0), (0, 0), (0, AUG - n_qk)))
    return w.reshape(w.shape[0], N_HEADS * AUG).astype(BF16)


def _row(v, width=None):
    v = v.astype(F32)[None, :]
    if width is not None and width > v.shape[1]:
        v = jnp.pad(v, ((0, 0), (0, width - v.shape[1])))
    return v


def kernel(x, positions, ev_norm, ev_w_in, ev_b_f, ev_qn_a, ev_kn_a, ev_qn_b, ev_kn_b, ev_w_out,
           od_norm, od_w_in, od_cq_norm, od_ckv_norm, od_w_uq, od_w_ukv, od_qn_c, od_kn_c,
           od_conv_w, od_conv_b, od_dt_bias, od_a_log, od_d_skip, od_gate_norm, od_w_out,
           mlp_norm, mlp_w1, mlp_w2):
    B, S, D = x.shape
    depth = mlp_w1.shape[0]
    ev_w, od_w = _pack_even_w(ev_w_in), _pack_odd_w(od_w_in)
    ev_wo, od_wo = ev_w_out.astype(BF16), od_w_out.astype(BF16)
    w1, w2 = mlp_w1.astype(BF16), mlp_w2.astype(BF16)
    cos = sin = None
    for layer in range(depth):
        i = layer // 2
        if layer % 2 == 0:
            bf = jnp.zeros((1, LANES), F32).at[0, MISC_F:MISC_F + N_HEADS].set(ev_b_f[i])
            qa, ka, va, qi, kd, wi, qb, kb, vb = _even_proj(
                x, _row(ev_norm[i]), ev_w[i], bf, _row(ev_qn_a[i]),
                _row(ev_kn_a[i]), _row(ev_qn_b[i]), _row(ev_kn_b[i]))
            o1 = _dsa(qa, ka, va, qi, kd, wi)
            o2 = _flash(qb, kb, vb)
            wo = ev_wo[i]
        else:
            if cos is None:
                cos, sin = _rope_tables(positions)
            q, k, v, z, xbc, dt = _odd_proj(
                x, _row(od_norm[i]), od_w[i], _row(od_cq_norm[i]),
                _row(od_ckv_norm[i]), _pack_wuq(od_w_uq[i]), od_w_ukv[i].astype(BF16),
                _row(od_qn_c[i], AUG), _row(od_kn_c[i], AUG), cos, sin)
            o1 = _flash(q, k, v)
            o2 = _ssd(xbc, z, dt, od_conv_w[i].astype(F32), _row(od_conv_b[i]),
                      _row(od_dt_bias[i], LANES), _row(od_a_log[i], LANES),
                      _row(jnp.repeat(od_d_skip[i], SSM_HEAD_DIM)), _row(od_gate_norm[i]))
            wo = od_wo[i]
        x = _out_mlp(x.reshape(B * S, D), o1.reshape(B * S, -1), o2.reshape(B * S, -1),
                     wo, _row(mlp_norm[layer]), w1[layer], w2[layer]).reshape(B, S, D)
    return x
```

```python
import functools

import jax
import jax.numpy as jnp
from jax import lax
from jax.experimental import pallas as pl
from jax.experimental.pallas import tpu as pltpu

F32 = jnp.float32
BF16 = jnp.bfloat16
EPS = 1e-6
NEG_INF = float("-inf")
F32_MAX = 3.4028234663852886e38
LOG2E = 1.4426950408889634

D_MODEL = 1024
HEAD_DIM = 128
N_HEADS = 4
N_IDX_HEADS = 8
IDX_DIM = 64
TOPK_MAX = 256
QK_NOPE = 128
QK_ROPE = 64
Q_LORA = 384
KV_LORA = 256
ROPE_THETA = 10000.0
SSM_INNER = 1024
SSM_HEAD_DIM = 64
SSM_HEADS = 16
SSM_GROUPS = 4
SSM_STATE = 128
CONV_WIDTH = 4
CONV_DIM = SSM_INNER + 2 * SSM_GROUPS * SSM_STATE
D_FF = 4 * D_MODEL

LANES = 128
AUG = 2 * LANES
VMEM_LIMIT = 56 * 1024 * 1024

TM_PROJ = 512
TM_EVEN = 1024
CUMSUM_BLOCK = 512
TM_MLP = 1024
TF_MLP = 1024
TQ_FLASH = 1024
TK_FLASH = 1024
TQ_DSA = 512
TK_DSA = 512
DSA_HEAD_GROUP = 2
SSD_CHUNK = 128
BISECT_GROUP = 4
BISECT_ROUND = 24
BISECT_MAX_ROUNDS = 8

EV_QA, EV_KA, EV_VA, EV_QI, EV_QB, EV_KB, EV_VB, EV_MISC, EV_END = (
    0, 512, 640, 768, 1280, 1792, 2304, 2816, 2944)
MISC_W = 64
MISC_F = 72
OD_CQ, OD_CKV, OD_KR, OD_DT, OD_Z, OD_XBC, OD_END = (0, 384, 640, 768, 896, 1920, 3968)


def _cparams(*sem):
    return pltpu.CompilerParams(dimension_semantics=sem, vmem_limit_bytes=VMEM_LIMIT)


def _resident(shape):
    nd = len(shape)
    return pl.BlockSpec(shape, lambda *_: (0,) * nd, pipeline_mode=pl.Buffered(1))


def _rms(x, g, n=None):
    n = x.shape[-1] if n is None else n
    ss = jnp.sum(x * x, axis=-1, keepdims=True)
    return x * lax.rsqrt(ss * (1.0 / n) + EPS) * g


def _dot(a, b):
    return jnp.dot(a, b, preferred_element_type=F32)


def _dot_nt(a, b):
    return lax.dot_general(a, b, (((1,), (1,)), ((), ())), preferred_element_type=F32)


def _dot_hi(a, b):
    return jnp.dot(a, b, preferred_element_type=F32, precision=lax.Precision.HIGHEST)


def _sigmoid(x):
    return 1.0 / (1.0 + jnp.exp(-x))


def _softplus(x):
    return jnp.maximum(x, 0.0) + jnp.log(1.0 + jnp.exp(-jnp.abs(x)))


def _lane_iota(shape):
    return lax.broadcasted_iota(jnp.int32, shape, len(shape) - 1)


def _tri(n):
    r = lax.broadcasted_iota(jnp.int32, (n, n), 0)
    c = lax.broadcasted_iota(jnp.int32, (n, n), 1)
    return r >= c


def _split3(c):
    hi = c.astype(BF16).astype(F32)
    mid = (c - hi).astype(BF16).astype(F32)
    lo = (c - hi - mid).astype(BF16).astype(F32)
    return hi, mid, lo


def _even_proj_kernel(x_ref, g_ref, w_ref, bf_ref, qna_ref, kna_ref, qnb_ref, knb_ref,
                      qa_ref, ka_ref, va_ref, qi_ref, kd_ref, wi_ref, qb_ref, kb_ref, vb_ref,
                      carry_ref):
    tm = x_ref.shape[1]

    @pl.when(pl.program_id(1) == 0)
    def _():
        carry_ref[...] = jnp.zeros_like(carry_ref)

    xn = _rms(x_ref[0], g_ref[...]).astype(BF16)

    def proj(lo, hi):
        return _dot(xn, w_ref[:, lo:hi])

    scale = HEAD_DIM ** -0.5 * LOG2E
    lane = _lane_iota((tm, LANES))
    ones_col = jnp.where(lane == 0, 1.0, 0.0).astype(BF16)
    qa = proj(EV_QA, EV_KA)
    for h in range(N_HEADS):
        sl = slice(h * HEAD_DIM, (h + 1) * HEAD_DIM)
        qa_ref[0, :, sl] = (_rms(qa[:, sl], qna_ref[...]) * scale).astype(BF16)
    ka_ref[0] = _rms(proj(EV_KA, EV_VA), kna_ref[...]).astype(BF16)
    va_ref[0, :, :HEAD_DIM] = proj(EV_VA, EV_QI).astype(BF16)
    va_ref[0, :, HEAD_DIM:] = ones_col
    qi_ref[0] = proj(EV_QI, EV_QB).astype(BF16)
    vb = proj(EV_VB, EV_MISC)
    for h in range(N_HEADS):
        vb_ref[0, :, h * AUG:h * AUG + HEAD_DIM] = vb[:, h * HEAD_DIM:(h + 1) * HEAD_DIM].astype(BF16)
        vb_ref[0, :, h * AUG + HEAD_DIM:(h + 1) * AUG] = ones_col

    misc = proj(EV_MISC, EV_END)
    kd_ref[0] = jnp.where(lane < IDX_DIM, misc, pltpu.roll(misc, IDX_DIM, 1)).astype(BF16)
    wi_ref[0] = misc.T[MISC_W:MISC_W + N_IDX_HEADS, :]

    zf = misc + bf_ref[...]
    logf = jnp.minimum(zf, 0.0) - jnp.log(1.0 + jnp.exp(-jnp.abs(zf)))
    logf = jnp.where((lane >= MISC_F) & (lane < MISC_F + N_HEADS), logf, 0.0)
    cb = min(tm, CUMSUM_BLOCK)
    tri = _tri(cb).astype(BF16)
    carry = carry_ref[...]
    blocks = []
    for r in range(tm // cb):
        parts = _split3(logf[r * cb:(r + 1) * cb])
        blocks.append(sum(_dot(tri, part.astype(BF16)) for part in parts) + carry)
        carry = blocks[-1][cb - 1:cb, :]
    csum = jnp.concatenate(blocks, axis=0)
    carry_ref[...] = carry

    qb = proj(EV_QB, EV_KB)
    kb = proj(EV_KB, EV_VB)
    one = jnp.ones((tm, LANES), F32)
    zero = jnp.zeros((tm, LANES), F32)
    for h in range(N_HEADS):
        sl = slice(h * HEAD_DIM, (h + 1) * HEAD_DIM)
        c = jnp.broadcast_to(csum[:, MISC_F + h:MISC_F + h + 1], (tm, LANES)) * LOG2E
        hi, mid, lo = _split3(c)
        aq = jnp.where(lane == 0, hi, jnp.where(lane == 1, mid, jnp.where(lane == 2, lo,
             jnp.where(lane < 6, one, zero))))
        ak = jnp.where(lane < 3, one, jnp.where(lane == 3, -hi, jnp.where(lane == 4, -mid,
             jnp.where(lane == 5, -lo, zero))))
        qb_ref[0, :, h * AUG:h * AUG + LANES] = (_rms(qb[:, sl], qnb_ref[...]) * scale).astype(BF16)
        qb_ref[0, :, h * AUG + LANES:(h + 1) * AUG] = aq.astype(BF16)
        kb_ref[0, :, h * AUG:h * AUG + LANES] = _rms(kb[:, sl], knb_ref[...]).astype(BF16)
        kb_ref[0, :, h * AUG + LANES:(h + 1) * AUG] = ak.astype(BF16)


def _even_proj(x, g, w, bf, qna, kna, qnb, knb):
    B, S, D = x.shape
    tm = min(TM_EVEN, S)
    tok = lambda c: pl.BlockSpec((1, tm, c), lambda b, j: (b, j, 0))
    row = lambda c: pl.BlockSpec((1, c), lambda b, j: (0, 0))
    out_cols = (512, 128, AUG, 512, 128, None, N_HEADS * AUG, N_HEADS * AUG, N_HEADS * AUG)
    out_specs = [tok(c) if c else pl.BlockSpec((1, N_IDX_HEADS, tm), lambda b, j: (b, 0, j))
                 for c in out_cols]
    out_shape = [jax.ShapeDtypeStruct((B, S, c), BF16) if c else
                 jax.ShapeDtypeStruct((B, N_IDX_HEADS, S), F32) for c in out_cols]
    return pl.pallas_call(
        _even_proj_kernel,
        grid=(B, S // tm),
        in_specs=[tok(D), row(D), _resident(w.shape), row(LANES), row(LANES), row(LANES),
                  row(LANES), row(LANES)],
        out_specs=out_specs,
        out_shape=out_shape,
        scratch_shapes=[pltpu.VMEM((1, LANES), F32)],
        compiler_params=_cparams("arbitrary", "arbitrary"),
        name="even_proj",
    )(x, g, w, bf, qna, kna, qnb, knb)


def _fold_max(mx, s):
    for g in range(s.shape[1] // LANES):
        mx = jnp.maximum(mx, s[:, g * LANES:(g + 1) * LANES])
    return mx


def _flash_kernel(q_ref, k_ref, v_ref, o_ref, lg_ref, acc_ref, *, tq, tk):
    i = pl.program_id(2)
    q = q_ref[0]
    half = tq // 2

    def kchunk(ref, j):
        return ref[0, pl.ds(pl.multiple_of(j * tk, tk), tk), :]

    def pass_a(j, mx):
        s = _dot_nt(q, kchunk(k_ref, j))
        lg_ref[j] = s
        return _fold_max(mx, s)

    mx = lax.fori_loop(0, i, pass_a, jnp.full((tq, LANES), NEG_INF, F32))
    kd = kchunk(k_ref, i)
    tri = _tri(half)
    s_top = jnp.where(tri, _dot_nt(q[:half], kd[:half]), NEG_INF)
    s_bot = _dot_nt(q[half:], kd)
    s_bot = jnp.concatenate([s_bot[:, :half], jnp.where(tri, s_bot[:, half:], NEG_INF)], axis=1)
    lg_ref[i, :half, :half] = s_top
    lg_ref[i, half:, :] = s_bot
    mx = jnp.concatenate([_fold_max(mx[:half], s_top), _fold_max(mx[half:], s_bot)], axis=0)
    m = jnp.max(mx, axis=-1, keepdims=True)

    acc_ref[...] = jnp.zeros_like(acc_ref)

    def pass_b(j, c):
        p = jnp.exp2(lg_ref[j] - m).astype(BF16)
        acc_ref[...] += _dot(p, kchunk(v_ref, j))
        return c

    lax.fori_loop(0, i, pass_b, 0)
    vd = kchunk(v_ref, i)
    p_top = jnp.exp2(lg_ref[i, :half, :half] - m[:half]).astype(BF16)
    p_bot = jnp.exp2(lg_ref[i, half:, :] - m[half:]).astype(BF16)
    acc_top = acc_ref[:half, :] + _dot(p_top, vd[:half])
    acc_bot = acc_ref[half:, :] + _dot(p_bot, vd)
    o_ref[0, :half, :] = (acc_top[:, :HEAD_DIM] / acc_top[:, HEAD_DIM:HEAD_DIM + 1]).astype(o_ref.dtype)
    o_ref[0, half:, :] = (acc_bot[:, :HEAD_DIM] / acc_bot[:, HEAD_DIM:HEAD_DIM + 1]).astype(o_ref.dtype)


def _flash(q, k, v):
    B, S, _ = q.shape
    tq, tk = min(TQ_FLASH, S), min(TK_FLASH, S)
    assert tq == tk, "the diagonal chunk is split on the query block's own key chunk"
    return pl.pallas_call(
        functools.partial(_flash_kernel, tq=tq, tk=tk),
        grid=(B, N_HEADS, S // tq),
        in_specs=[pl.BlockSpec((1, tq, AUG), lambda b, h, i: (b, i, h)),
                  pl.BlockSpec((1, S, AUG), lambda b, h, i: (b, 0, h)),
                  pl.BlockSpec((1, S, AUG), lambda b, h, i: (b, 0, h))],
        out_specs=pl.BlockSpec((1, tq, HEAD_DIM), lambda b, h, i: (b, i, h)),
        out_shape=jax.ShapeDtypeStruct((B, S, N_HEADS * HEAD_DIM), BF16),
        scratch_shapes=[pltpu.VMEM((S // tk, tq, tk), F32), pltpu.VMEM((tq, AUG), F32)],
        compiler_params=_cparams("arbitrary", "arbitrary", "arbitrary"),
        name="flash_attn",
    )(q, k, v)


def _dsa_kernel(qa_ref, ka_ref, va_ref, qi_ref, kd_ref, wi_ref, o_ref,
                sc_ref, qstk_ref, lg_ref, acc_ref, *, tkc, topk):
    tq = qa_ref.shape[1]
    i = pl.program_id(1)
    nkc = (i * tq + tq + tkc - 1) // tkc
    kf = float(topk)
    lane = _lane_iota((tq, LANES))
    keyi = lax.broadcasted_iota(jnp.int32, (tkc, tq), 0)
    qpos = i * tq + lax.broadcasted_iota(jnp.int32, (tkc, tq), 1)

    q8 = qi_ref[0]
    for j in range(N_IDX_HEADS):
        grp = q8[:, LANES * (j // 2):LANES * (j // 2 + 1)]
        keep = (lane < IDX_DIM) if j % 2 == 0 else (lane >= IDX_DIM)
        qstk_ref[tq * j:tq * (j + 1), :] = jnp.where(keep, grp, jnp.zeros_like(grp))
    wrows = [wi_ref[0, j:j + 1, :] for j in range(N_IDX_HEADS)]

    def chunk(ref, kc):
        return ref[0, pl.ds(pl.multiple_of(kc * tkc, tkc), tkc), :]

    def fold8(x, op):
        cols = []
        for g in range(x.shape[1] // LANES):
            xg = x[:, g * LANES:(g + 1) * LANES]
            rows = xg.shape[0]
            while rows > 8:
                rows //= 2
                xg = op(xg[:rows], xg[rows:])
            cols.append(xg)
        return jnp.concatenate(cols, axis=1)

    def p1(kc, carry, diagonal=False):
        rmax, rmin, npos, nnonneg = carry
        s8 = _dot_nt(chunk(kd_ref, kc), qstk_ref[...])
        score = wrows[0] * jnp.maximum(s8[:, 0:tq], 0.0)
        for j in range(1, N_IDX_HEADS):
            score = score + wrows[j] * jnp.maximum(s8[:, tq * j:tq * (j + 1)], 0.0)
        low = high = score
        if diagonal:
            adm = (kc * tkc + keyi) <= qpos
            low = jnp.where(adm, score, NEG_INF)
            high = jnp.where(adm, score, -NEG_INF)
        sc_ref[kc] = low
        rmax = jnp.maximum(rmax, fold8(low, jnp.maximum))
        rmin = jnp.minimum(rmin, fold8(high, jnp.minimum))
        npos = npos + fold8(jnp.where(low > 0.0, 1.0, 0.0), jnp.add)
        nnonneg = nnonneg + fold8(jnp.where(low >= 0.0, 1.0, 0.0), jnp.add)
        return rmax, rmin, npos, nnonneg

    zero8 = jnp.zeros((8, tq), F32)
    carry = lax.fori_loop(0, nkc - 1, p1, (jnp.full((8, tq), NEG_INF, F32),
                                           jnp.full((8, tq), -NEG_INF, F32), zero8, zero8))
    rmax, rmin, npos, nnonneg = p1(nkc - 1, carry, diagonal=True)
    rmax = jnp.max(rmax, axis=0, keepdims=True)
    rmin = jnp.min(rmin, axis=0, keepdims=True)
    c_gt0 = jnp.sum(npos, axis=0, keepdims=True)
    c_ge0 = jnp.sum(nnonneg, axis=0, keepdims=True)

    def row_count(cmp, t):
        def cb(kc, c):
            return c + fold8(jnp.where(cmp(sc_ref[kc], t), 1.0, 0.0), jnp.add)
        c = lax.fori_loop(0, nkc - 1, cb, zero8)
        hk, hq = tkc // 2, tq // 2
        c = c + fold8(jnp.where(cmp(sc_ref[nkc - 1, :hk, :], t), 1.0, 0.0), jnp.add)
        late = fold8(jnp.where(cmp(sc_ref[nkc - 1, hk:, hq:], t[:, hq:]), 1.0, 0.0), jnp.add)
        c = c + jnp.concatenate([jnp.zeros((8, hq), F32), late], axis=1)
        return jnp.sum(c, axis=0, keepdims=True)

    def any_row(flag):
        return jnp.max(flag.astype(jnp.int32))

    def select_threshold():
        hi0 = rmax + (jnp.abs(rmax) * 1.2e-7 + 1e-37)
        n_adm = (qpos[0:1, :] + 1).astype(F32)
        above = c_ge0 >= kf
        pend0 = (above & (c_gt0 < kf)).astype(jnp.int32)
        lo0 = jnp.where(above, 0.0, rmin)
        cnt0 = jnp.where(above, c_ge0, n_adm)
        hi0 = jnp.where(above, hi0, 0.0)
        few = n_adm <= kf
        lo0 = jnp.where(few, -F32_MAX, lo0)
        cnt0 = jnp.where(few, kf, cnt0)
        pend0 = jnp.where(few, 0, pend0)

        def unresolved(cnt, pend):
            return any_row((cnt > kf + 1.0) & (pend == 0))

        def fold_two_min(x):
            c1, c2 = [], []
            for g in range(tq // LANES):
                xg = x[:, g * LANES:(g + 1) * LANES]
                rows = tkc // 2
                m1 = jnp.minimum(xg[:rows], xg[rows:])
                m2 = jnp.maximum(xg[:rows], xg[rows:])
                while rows > 8:
                    rows //= 2
                    a1, b1, a2, b2 = m1[:rows], m1[rows:], m2[:rows], m2[rows:]
                    m1, m2 = (jnp.minimum(a1, b1),
                              jnp.minimum(jnp.maximum(a1, b1), jnp.minimum(a2, b2)))
                c1.append(m1)
                c2.append(m2)
            return jnp.concatenate(c1, axis=1), jnp.concatenate(c2, axis=1)

        def snap(lo, cnt, pend):
            def two_min(kc, st):
                m1, m2 = st
                blk = sc_ref[kc]
                a1, a2 = fold_two_min(jnp.where(blk >= lo, blk, -NEG_INF))
                return (jnp.minimum(m1, a1),
                        jnp.minimum(jnp.maximum(m1, a1), jnp.minimum(m2, a2)))
            inf8 = jnp.full((8, tq), -NEG_INF, F32)
            m1, m2 = lax.fori_loop(0, nkc, two_min, (inf8, inf8))
            a1 = jnp.min(m1, axis=0, keepdims=True)
            at_min = m1 == a1
            others = jnp.minimum(jnp.min(jnp.where(at_min, -NEG_INF, m1), axis=0, keepdims=True),
                                 jnp.min(m2, axis=0, keepdims=True))
            n_min = jnp.sum(jnp.where(at_min, 1.0, 0.0), axis=0, keepdims=True)
            a2 = jnp.where(n_min > 1.0, a1, others)
            ok = (cnt == kf + 1.0) & (pend == 0) & (a2 > a1)
            return jnp.where(ok, a2, lo), jnp.where(ok, kf, cnt)

        def bis_cond(st):
            return (st[-1] > 0) & (st[0] < BISECT_ROUND)

        def bis_body(st):
            it, lo, hi, cnt, pend, _ = st
            for _ in range(BISECT_GROUP):
                mid = lo + (hi - lo) * 0.5
                c = row_count(jnp.greater_equal, mid)
                up = c >= kf
                lo = jnp.where(up, mid, lo)
                hi = jnp.where(up, hi, mid)
                cnt = jnp.where(up, c, cnt)
            return it + BISECT_GROUP, lo, hi, cnt, pend, unresolved(cnt, pend)

        def resolve_ties(lo, cnt):
            def vmin(kc, v):
                blk = sc_ref[kc]
                return jnp.minimum(v, fold8(jnp.where(blk >= lo, blk, -NEG_INF), jnp.minimum))
            v = lax.fori_loop(0, nkc, vmin, jnp.full((8, tq), -NEG_INF, F32))
            v = jnp.min(v, axis=0, keepdims=True)
            cgt = row_count(jnp.greater, v)
            tie = (cnt != kf) & (cgt < kf)
            need = kf - cgt
            tri = _tri(tkc).astype(BF16)

            def drop(kc, seen):
                blk = sc_ref[kc]
                eq = tie & (blk == v)
                rank = _dot(tri, jnp.where(eq, 1.0, 0.0).astype(BF16)) + seen
                sc_ref[kc] = jnp.where(eq & (rank > need), NEG_INF, blk)
                return rank[tkc - 1:tkc, :]
            lax.fori_loop(0, nkc, drop, jnp.zeros((1, tq), F32))
            return jnp.where(tie, v, lo), jnp.where(tie, kf, cnt)

        def round_cond(st):
            return (st[-1] > 0) & (st[0] < BISECT_MAX_ROUNDS)

        def round_body(st):
            r, lo, hi, cnt, pend, _ = st
            _, lo, hi, cnt, pend, _ = lax.while_loop(
                bis_cond, bis_body, (0, lo, hi, cnt, pend, unresolved(cnt, pend)))
            lo, cnt = lax.cond(any_row((cnt == kf + 1.0) & (pend == 0)) > 0, snap,
                               lambda a, b, c: (a, b), lo, cnt, pend)
            lo, cnt = lax.cond(any_row(cnt != kf) > 0, resolve_ties, lambda a, b: (a, b), lo, cnt)
            return r + 1, lo, hi, cnt, jnp.zeros_like(pend), any_row(cnt != kf)

        st = lax.while_loop(round_cond, round_body,
                            (0, lo0, hi0, cnt0, pend0, any_row(cnt0 != kf)))
        return st[1]

    thr = lax.cond((i + 1) * tq > topk, select_threshold,
                   lambda: jnp.full((1, tq), -F32_MAX, F32))

    qa = qa_ref[0]
    hg = DSA_HEAD_GROUP
    for g0 in range(0, N_HEADS, hg):
        qs = jnp.concatenate([qa[:, h * HEAD_DIM:(h + 1) * HEAD_DIM] for h in range(g0, g0 + hg)],
                             axis=0)

        def p3a(kc, mx, qs=qs, first=(g0 == 0)):
            s = _dot_nt(qs, chunk(ka_ref, kc))
            if first:
                sel = jnp.where(sc_ref[kc] >= thr, 0.0, NEG_INF).T
                sc_ref[kc] = sel
            else:
                sel = sc_ref[kc]
            parts = []
            for h in range(hg):
                rs = slice(h * tq, (h + 1) * tq)
                sh = s[rs] + sel
                lg_ref[kc, rs, :] = sh
                parts.append(_fold_max(mx[rs], sh))
            return jnp.concatenate(parts, axis=0)

        mx = lax.fori_loop(0, nkc, p3a, jnp.full((hg * tq, LANES), NEG_INF, F32))
        m = jnp.max(mx, axis=-1, keepdims=True)

        acc_ref[...] = jnp.zeros_like(acc_ref)

        def p3b(kc, c, m=m):
            p = jnp.exp2(lg_ref[kc] - m).astype(BF16)
            acc_ref[...] += _dot(p, chunk(va_ref, kc))
            return c

        lax.fori_loop(0, nkc, p3b, 0)
        for h in range(hg):
            acc = acc_ref[h * tq:(h + 1) * tq, :]
            o_ref[0, :, (g0 + h) * HEAD_DIM:(g0 + h + 1) * HEAD_DIM] = (
                acc[:, :HEAD_DIM] / acc[:, HEAD_DIM:HEAD_DIM + 1]).astype(o_ref.dtype)


def _dsa(qa, ka, va, qi, kd, wi):
    B, S, _ = qa.shape
    tq, tkc = TQ_DSA, min(TK_DSA, S)
    assert tq == tkc, "the transposed selection mask reuses the score scratch"
    topk = min(TOPK_MAX, S // 4)
    qblk = lambda c: pl.BlockSpec((1, tq, c), lambda b, i: (b, i, 0))
    full = lambda c: pl.BlockSpec((1, S, c), lambda b, i: (b, 0, 0))
    return pl.pallas_call(
        functools.partial(_dsa_kernel, tkc=tkc, topk=topk),
        grid=(B, S // tq),
        in_specs=[qblk(512), full(128), full(AUG), qblk(512), full(128),
                  pl.BlockSpec((1, N_IDX_HEADS, tq), lambda b, i: (b, 0, i))],
        out_specs=qblk(512),
        out_shape=jax.ShapeDtypeStruct((B, S, N_HEADS * HEAD_DIM), BF16),
        scratch_shapes=[pltpu.VMEM((S // tkc, tkc, tq), F32),
                        pltpu.VMEM((N_IDX_HEADS * tq, LANES), BF16),
                        pltpu.VMEM((S // tkc, DSA_HEAD_GROUP * tq, tkc), F32),
                        pltpu.VMEM((DSA_HEAD_GROUP * tq, AUG), F32)],
        compiler_params=_cparams("arbitrary", "arbitrary"),
        name="dsa_attn",
    )(qa, ka, va, qi, kd, wi)


def _rope_table_kernel(pos_ref, inv_ref, cos_ref, sin_ref):
    tm = pos_ref.shape[1]
    lane = _lane_iota((tm, LANES))
    ang = pos_ref[0].astype(F32) * inv_ref[...]
    half = QK_ROPE // 2
    cos_ref[0] = jnp.where(lane < QK_ROPE, jnp.cos(ang), 0.0)
    sn = jnp.sin(ang)
    sin_ref[0] = jnp.where(lane < half, -sn, jnp.where(lane < QK_ROPE, sn, 0.0))


def _rope_tables(positions):
    B, S = positions.shape
    tm = TM_PROJ
    half = QK_ROPE // 2
    inv = ROPE_THETA ** (-jnp.arange(half, dtype=F32) / half)
    inv = jnp.concatenate([inv, inv, jnp.zeros((LANES - QK_ROPE,), F32)])[None, :]
    return pl.pallas_call(
        _rope_table_kernel,
        grid=(B, S // tm),
        in_specs=[pl.BlockSpec((1, tm, 1), lambda b, j: (b, j, 0)),
                  pl.BlockSpec((1, LANES), lambda b, j: (0, 0))],
        out_specs=[pl.BlockSpec((1, tm, LANES), lambda b, j: (b, j, 0))] * 2,
        out_shape=[jax.ShapeDtypeStruct((B, S, LANES), F32)] * 2,
        compiler_params=_cparams("arbitrary", "arbitrary"),
        name="rope_tables",
    )(positions[:, :, None], inv)


def _rope(t, cos, sin):
    lane = _lane_iota(t.shape)
    half = QK_ROPE // 2
    swap = jnp.where(lane < half, pltpu.roll(t, LANES - half, 1), pltpu.roll(t, half, 1))
    return t * cos + swap * sin


def _odd_proj_kernel(x_ref, g_ref, w_ref, cqn_ref, ckvn_ref, wuq_ref, wukv_ref, qnc_ref, knc_ref,
                     cos_ref, sin_ref, q_ref, k_ref, v_ref, z_ref, xbc_ref, dt_ref):
    xn = _rms(x_ref[0], g_ref[...]).astype(BF16)

    def proj(lo, hi):
        return _dot(xn, w_ref[:, lo:hi])

    cos = cos_ref[0]
    sin = sin_ref[0]
    n_qk = QK_NOPE + QK_ROPE
    scale = n_qk ** -0.5 * LOG2E
    ones_col = jnp.where(_lane_iota(cos.shape) == 0, 1.0, 0.0).astype(BF16)

    cq = _rms(proj(OD_CQ, OD_CKV), cqn_ref[...]).astype(BF16)
    q = _dot(cq, wuq_ref[...])
    ckv = _rms(proj(OD_CKV, OD_KR), ckvn_ref[...]).astype(BF16)
    kv = _dot(ckv, wukv_ref[...])
    kr = proj(OD_KR, OD_DT)
    kr_ss = jnp.sum(kr * kr, axis=-1, keepdims=True)
    kr_rot = _rope(kr * knc_ref[:, LANES:], cos, sin)
    for h in range(N_HEADS):
        qh = q[:, h * AUG:(h + 1) * AUG]
        qh = _rms(qh, qnc_ref[...], n_qk) * scale
        q_ref[0, :, h * AUG:h * AUG + LANES] = qh[:, :LANES].astype(BF16)
        q_ref[0, :, h * AUG + LANES:(h + 1) * AUG] = _rope(qh[:, LANES:], cos, sin).astype(BF16)
        kn = kv[:, h * AUG:h * AUG + LANES]
        r = lax.rsqrt((jnp.sum(kn * kn, axis=-1, keepdims=True) + kr_ss) * (1.0 / n_qk) + EPS)
        k_ref[0, :, h * AUG:h * AUG + LANES] = (kn * r * knc_ref[:, :LANES]).astype(BF16)
        k_ref[0, :, h * AUG + LANES:(h + 1) * AUG] = (kr_rot * r).astype(BF16)
        v_ref[0, :, h * AUG:h * AUG + HEAD_DIM] = kv[:, h * AUG + LANES:(h + 1) * AUG].astype(BF16)
        v_ref[0, :, h * AUG + HEAD_DIM:(h + 1) * AUG] = ones_col

    dt_ref[0] = proj(OD_DT, OD_Z)
    z_ref[0] = proj(OD_Z, OD_XBC)
    xbc_ref[0] = proj(OD_XBC, OD_END)


def _odd_proj(x, g, w, cqn, ckvn, wuq, wukv, qnc, knc, cos, sin):
    B, S, D = x.shape
    tm = TM_PROJ
    tok = lambda c: pl.BlockSpec((1, tm, c), lambda b, j: (b, j, 0))
    row = lambda c: pl.BlockSpec((1, c), lambda b, j: (0, 0))
    out_cols = (N_HEADS * AUG, N_HEADS * AUG, N_HEADS * AUG, SSM_INNER, CONV_DIM, LANES)
    out_dt = (BF16, BF16, BF16, F32, F32, F32)
    return pl.pallas_call(
        _odd_proj_kernel,
        grid=(B, S // tm),
        in_specs=[tok(D), row(D), _resident(w.shape), row(Q_LORA), row(KV_LORA),
                  _resident(wuq.shape), _resident(wukv.shape), row(AUG), row(AUG),
                  tok(LANES), tok(LANES)],
        out_specs=[tok(c) for c in out_cols],
        out_shape=[jax.ShapeDtypeStruct((B, S, c), dt) for c, dt in zip(out_cols, out_dt)],
        compiler_params=_cparams("arbitrary", "arbitrary"),
        name="odd_proj",
    )(x, g, w, cqn, ckvn, wuq, wukv, qnc, knc, cos, sin)


def _ssd_kernel(xbc_ref, z_ref, dt_ref, ex_ref, cw_ref, cb_ref, dtb_ref, alog_ref, dskip_ref,
                gn_ref, y_ref, xe_ref, st_ref):
    q = xbc_ref.shape[1]
    halo = 8

    @pl.when(pl.program_id(1) == 0)
    def _():
        xe_ref[0:halo, :] = jnp.zeros((halo, CONV_DIM), F32)
        st_ref[...] = jnp.zeros_like(st_ref)

    xe_ref[halo:halo + q, :] = xbc_ref[0]
    ext = xe_ref[...]
    conv = cb_ref[...] + ext[halo:] * cw_ref[CONV_WIDTH - 1:CONV_WIDTH, :]
    for s in range(1, CONV_WIDTH):
        w = CONV_WIDTH - 1 - s
        conv = conv + pltpu.roll(ext, s, 0)[halo:] * cw_ref[w:w + 1, :]
    xe_ref[0:halo, :] = ext[q:q + halo]
    act = conv * _sigmoid(conv)
    xs = act[:, :SSM_INNER]
    gs = SSM_STATE
    bm = [act[:, SSM_INNER + g * gs:SSM_INNER + (g + 1) * gs] for g in range(SSM_GROUPS)]
    cm = [act[:, SSM_INNER + (SSM_GROUPS + g) * gs:SSM_INNER + (SSM_GROUPS + g + 1) * gs]
          for g in range(SSM_GROUPS)]

    lane = _lane_iota((q, LANES))
    dt = _softplus(dt_ref[0] + dtb_ref[...])
    a = jnp.where(lane < SSM_HEADS, dt * (-jnp.exp(alog_ref[...])), 0.0)
    tri = _tri(q)
    trif = tri.astype(F32)
    acum = _dot_hi(trif, a)
    acum_t = lax.dot_general(a.T, trif, (((1,), (1,)), ((), ())), preferred_element_type=F32,
                             precision=lax.Precision.HIGHEST)
    last = acum[q - 1:q, :]
    dte = jnp.exp(last - acum)
    eac = jnp.exp(acum)

    stack = jnp.concatenate([dt, dte, eac], axis=0)
    s_hi = stack.astype(BF16)
    s_lo = (stack - s_hi.astype(F32)).astype(BF16)
    wide = _dot(jnp.concatenate([s_hi, s_lo], axis=1), ex_ref[...])
    dt_x, dte_x, eac_x = wide[0:q], wide[q:2 * q], wide[2 * q:3 * q]
    cdec_x = eac_x[q - 1:q, :]

    xd = xs * dt_x
    xdd = (xd * dte_x).astype(BF16)
    xd16 = xd.astype(BF16)
    zero16 = jnp.zeros((q, LANES), BF16)

    rpg = SSM_HEADS // SSM_GROUPS
    gw = rpg * SSM_HEAD_DIM
    y_parts = []
    new_states = []
    for g in range(SSM_GROUPS):
        b16 = bm[g].astype(BF16)
        c16 = cm[g].astype(BF16)
        cb = _dot_nt(c16, b16)
        st_old = st_ref[:, g * gw:(g + 1) * gw]
        y_off = _dot(c16, st_old.astype(BF16)) * eac_x[:, g * gw:(g + 1) * gw]
        y_diag = []
        for pp in range(rpg // 2):
            p = g * (rpg // 2) + pp
            xp = xd16[:, p * LANES:(p + 1) * LANES]
            acc = None
            for e in range(2):
                h = 2 * p + e
                diff = acum[:, h:h + 1] - acum_t[h:h + 1, :]
                lm = jnp.exp(jnp.where(tri, diff, NEG_INF))
                gmat = (cb * lm).astype(BF16)
                keep = (lane < SSM_HEAD_DIM) if e == 0 else (lane >= SSM_HEAD_DIM)
                part = _dot(gmat, jnp.where(keep, xp, zero16))
                acc = part if acc is None else acc + part
            y_diag.append(acc)
        y_parts.append(jnp.concatenate(y_diag, axis=1) + y_off)
        st_new = _dot(bm[g].T.astype(BF16), xdd[:, g * gw:(g + 1) * gw])
        new_states.append(st_new)
    for g in range(SSM_GROUPS):
        sl = slice(g * gw, (g + 1) * gw)
        st_ref[:, sl] = st_ref[:, sl] * cdec_x[:, sl] + new_states[g]

    y = jnp.concatenate(y_parts, axis=1) + dskip_ref[...] * xs
    zz = z_ref[0]
    y = y * (zz * _sigmoid(zz))
    for g in range(SSM_GROUPS):
        sl = slice(g * gw, (g + 1) * gw)
        y_ref[0, :, sl] = _rms(y[:, sl], gn_ref[:, sl]).astype(y_ref.dtype)


def _ssd(xbc, z, dt, cw, cb, dtb, alog, dskip, gn):
    B, S, _ = xbc.shape
    q = SSD_CHUNK
    tok = lambda c: pl.BlockSpec((1, q, c), lambda b, j: (b, j, 0))
    row = lambda r, c: pl.BlockSpec((r, c), lambda b, j: (0, 0))
    head = jnp.arange(LANES)[:, None]
    chan = jnp.arange(SSM_INNER)[None, :] // SSM_HEAD_DIM
    ex = jnp.tile((head == chan).astype(BF16), (2, 1))
    return pl.pallas_call(
        _ssd_kernel,
        grid=(B, S // q),
        in_specs=[tok(CONV_DIM), tok(SSM_INNER), tok(LANES), row(*ex.shape),
                  row(CONV_WIDTH, CONV_DIM), row(1, CONV_DIM), row(1, LANES), row(1, LANES),
                  row(1, SSM_INNER), row(1, SSM_INNER)],
        out_specs=tok(SSM_INNER),
        out_shape=jax.ShapeDtypeStruct((B, S, SSM_INNER), BF16),
        scratch_shapes=[pltpu.VMEM((q + 8, CONV_DIM), F32), pltpu.VMEM((SSM_STATE, SSM_INNER), F32)],
        compiler_params=_cparams("arbitrary", "arbitrary"),
        name="ssd_scan",
    )(xbc, z, dt, ex, cw, cb, dtb, alog, dskip, gn)


def _out_mlp_kernel(x_ref, o1_ref, o2_ref, wo_ref, g_ref, w1_ref, w2_ref, y_ref,
                    xn_ref, acc_ref, *, tf):
    k1 = o1_ref.shape[1]
    xnew = x_ref[...] + _dot(o1_ref[...], wo_ref[:k1, :]) + _dot(o2_ref[...], wo_ref[k1:, :])
    xn_ref[...] = _rms(xnew, g_ref[...]).astype(BF16)
    acc_ref[...] = xnew

    def body(c, carry):
        cs = pl.multiple_of(c * tf, tf)
        h = _dot(xn_ref[...], w1_ref[:, pl.ds(cs, tf)])
        a = jnp.square(jnp.maximum(h, 0.0)).astype(BF16)
        acc_ref[...] += _dot(a, w2_ref[pl.ds(cs, tf), :])
        return carry

    lax.fori_loop(0, w1_ref.shape[1] // tf, body, 0)
    y_ref[...] = acc_ref[...]


def _out_mlp(x, o1, o2, wo, g, w1, w2):
    T, D = x.shape
    tm = TM_MLP
    assert wo.shape[0] == o1.shape[1] + o2.shape[1]
    tok = lambda c: pl.BlockSpec((tm, c), lambda i: (i, 0))
    return pl.pallas_call(
        functools.partial(_out_mlp_kernel, tf=TF_MLP),
        grid=(T // tm,),
        in_specs=[tok(D), tok(o1.shape[1]), tok(o2.shape[1]), _resident(wo.shape),
                  pl.BlockSpec((1, D), lambda i: (0, 0)), _resident(w1.shape), _resident(w2.shape)],
        out_specs=tok(D),
        out_shape=jax.ShapeDtypeStruct((T, D), F32),
        scratch_shapes=[pltpu.VMEM((tm, D), BF16), pltpu.VMEM((tm, D), F32)],
        compiler_params=_cparams("arbitrary"),
        name="out_mlp",
    )(x, o1, o2, wo, g, w1, w2)


def _pack_even_w(w):
    w = w.astype(BF16)
    qa, ka, va, qi, ki, wi, qb, kb, vb, fb = jnp.split(
        w, [512, 640, 768, 1280, 1344, 1352, 1864, 2376, 2888], axis=-1)
    pad = jnp.zeros(w.shape[:-1] + (LANES - IDX_DIM - N_IDX_HEADS - N_HEADS,), w.dtype)
    return jnp.concatenate([qa, ka, va, qi, qb, kb, vb, ki, wi, fb, pad], axis=-1)


def _pack_odd_w(w):
    w = w.astype(BF16)
    cq, ckv, kr, z, xbc, dt = jnp.split(w, [384, 640, 704, 1728, 3776], axis=-1)
    zpad = lambda n: jnp.zeros(w.shape[:-1] + (n,), w.dtype)
    return jnp.concatenate([cq, ckv, kr, zpad(LANES - QK_ROPE), dt, zpad(LANES - SSM_HEADS), z, xbc],
                           axis=-1)


def _pack_wuq(w):
    n_qk = QK_NOPE + QK_ROPE
    w = w.reshape(w.shape[0], N_HEADS, n_qk)
    w = jnp.pad(w, ((0, 0), (0, 0), (0, AUG - n_qk)))
    return w.reshape(w.shape[0], N_HEADS * AUG).astype(BF16)


def _row(v, width=None):
    v = v.astype(F32)[None, :]
    if width is not None and width > v.shape[1]:
        v = jnp.pad(v, ((0, 0), (0, width - v.shape[1])))
    return v


def kernel(x, positions, ev_norm, ev_w_in, ev_b_f, ev_qn_a, ev_kn_a, ev_qn_b, ev_kn_b, ev_w_out,
           od_norm, od_w_in, od_cq_norm, od_ckv_norm, od_w_uq, od_w_ukv, od_qn_c, od_kn_c,
           od_conv_w, od_conv_b, od_dt_bias, od_a_log, od_d_skip, od_gate_norm, od_w_out,
           mlp_norm, mlp_w1, mlp_w2):
    B, S, D = x.shape
    depth = mlp_w1.shape[0]
    ev_w, od_w = _pack_even_w(ev_w_in), _pack_odd_w(od_w_in)
    ev_wo, od_wo = ev_w_out.astype(BF16), od_w_out.astype(BF16)
    w1, w2 = mlp_w1.astype(BF16), mlp_w2.astype(BF16)
    cos = sin = None
    for layer in range(depth):
        i = layer // 2
        if layer % 2 == 0:
            bf = jnp.zeros((1, LANES), F32).at[0, MISC_F:MISC_F + N_HEADS].set(ev_b_f[i])
            qa, ka, va, qi, kd, wi, qb, kb, vb = _even_proj(
                x, _row(ev_norm[i]), ev_w[i], bf, _row(ev_qn_a[i]),
                _row(ev_kn_a[i]), _row(ev_qn_b[i]), _row(ev_kn_b[i]))
            o1 = _dsa(qa, ka, va, qi, kd, wi)
            o2 = _flash(qb, kb, vb)
            wo = ev_wo[i]
        else:
            if cos is None:
                cos, sin = _rope_tables(positions)
            q, k, v, z, xbc, dt = _odd_proj(
                x, _row(od_norm[i]), od_w[i], _row(od_cq_norm[i]),
                _row(od_ckv_norm[i]), _pack_wuq(od_w_uq[i]), od_w_ukv[i].astype(BF16),
                _row(od_qn_c[i], AUG), _row(od_kn_c[i], AUG), cos, sin)
            o1 = _flash(q, k, v)
            o2 = _ssd(xbc, z, dt, od_conv_w[i].astype(F32), _row(od_conv_b[i]),
                      _row(od_dt_bias[i], LANES), _row(od_a_log[i], LANES),
                      _row(jnp.repeat(od_d_skip[i], SSM_HEAD_DIM)), _row(od_gate_norm[i]))
            wo = od_wo[i]
        x = _out_mlp(x.reshape(B * S, D), o1.reshape(B * S, -1), o2.reshape(B * S, -1),
                     wo, _row(mlp_norm[layer]), w1[layer], w2[layer]).reshape(B, S, D)
    return x
```

```python
import functools

import jax
import jax.numpy as jnp
from jax import lax
from jax.experimental import pallas as pl
from jax.experimental.pallas import tpu as pltpu

F32 = jnp.float32
BF16 = jnp.bfloat16
EPS = 1e-6
NEG_INF = float("-inf")
F32_MAX = 3.4028234663852886e38
LOG2E = 1.4426950408889634

D_MODEL = 1024
HEAD_DIM = 128
N_HEADS = 4
N_IDX_HEADS = 8
IDX_DIM = 64
TOPK_MAX = 256
QK_NOPE = 128
QK_ROPE = 64
Q_LORA = 384
KV_LORA = 256
ROPE_THETA = 10000.0
SSM_INNER = 1024
SSM_HEAD_DIM = 64
SSM_HEADS = 16
SSM_GROUPS = 4
SSM_STATE = 128
CONV_WIDTH = 4
CONV_DIM = SSM_INNER + 2 * SSM_GROUPS * SSM_STATE
D_FF = 4 * D_MODEL

LANES = 128
AUG = 2 * LANES
VMEM_LIMIT = 56 * 1024 * 1024

TM_PROJ = 512
TM_EVEN = 1024
CUMSUM_BLOCK = 512
TM_MLP = 1024
TF_MLP = 1024
TQ_FLASH = 1024
TK_FLASH = 1024
TQ_DSA = 512
TK_DSA = 512
DSA_HEAD_GROUP = 2
SSD_CHUNK = 128
BISECT_GROUP = 4
BISECT_ROUND = 24
BISECT_MAX_ROUNDS = 8

EV_QA, EV_KA, EV_VA, EV_QI, EV_QB, EV_KB, EV_VB, EV_MISC, EV_END = (
    0, 512, 640, 768, 1280, 1792, 2304, 2816, 2944)
MISC_W = 64
MISC_F = 72
OD_CQ, OD_CKV, OD_KR, OD_DT, OD_Z, OD_XBC, OD_END = (0, 384, 640, 768, 896, 1920, 3968)


def _cparams(*sem):
    return pltpu.CompilerParams(dimension_semantics=sem, vmem_limit_bytes=VMEM_LIMIT)


def _resident(shape):
    nd = len(shape)
    return pl.BlockSpec(shape, lambda *_: (0,) * nd, pipeline_mode=pl.Buffered(1))


def _rms(x, g, n=None):
    n = x.shape[-1] if n is None else n
    ss = jnp.sum(x * x, axis=-1, keepdims=True)
    return x * lax.rsqrt(ss * (1.0 / n) + EPS) * g


def _dot(a, b):
    return jnp.dot(a, b, preferred_element_type=F32)


def _dot_nt(a, b):
    return lax.dot_general(a, b, (((1,), (1,)), ((), ())), preferred_element_type=F32)


def _dot_hi(a, b):
    return jnp.dot(a, b, preferred_element_type=F32, precision=lax.Precision.HIGHEST)


def _sigmoid(x):
    return 1.0 / (1.0 + jnp.exp(-x))


def _softplus(x):
    return jnp.maximum(x, 0.0) + jnp.log(1.0 + jnp.exp(-jnp.abs(x)))


def _lane_iota(shape):
    return lax.broadcasted_iota(jnp.int32, shape, len(shape) - 1)


def _tri(n):
    r = lax.broadcasted_iota(jnp.int32, (n, n), 0)
    c = lax.broadcasted_iota(jnp.int32, (n, n), 1)
    return r >= c


def _split3(c):
    hi = c.astype(BF16).astype(F32)
    mid = (c - hi).astype(BF16).astype(F32)
    lo = (c - hi - mid).astype(BF16).astype(F32)
    return hi, mid, lo


def _even_proj_kernel(x_ref, g_ref, w_ref, bf_ref, qna_ref, kna_ref, qnb_ref, knb_ref,
                      qa_ref, ka_ref, va_ref, qi_ref, kd_ref, wi_ref, qb_ref, kb_ref, vb_ref,
                      carry_ref):
    tm = x_ref.shape[1]

    @pl.when(pl.program_id(1) == 0)
    def _():
        carry_ref[...] = jnp.zeros_like(carry_ref)

    xn = _rms(x_ref[0], g_ref[...]).astype(BF16)

    def proj(lo, hi):
        return _dot(xn, w_ref[:, lo:hi])

    scale = HEAD_DIM ** -0.5 * LOG2E
    lane = _lane_iota((tm, LANES))
    ones_col = jnp.where(lane == 0, 1.0, 0.0).astype(BF16)
    qa = proj(EV_QA, EV_KA)
    for h in range(N_HEADS):
        sl = slice(h * HEAD_DIM, (h + 1) * HEAD_DIM)
        qa_ref[0, :, sl] = (_rms(qa[:, sl], qna_ref[...]) * scale).astype(BF16)
    ka_ref[0] = _rms(proj(EV_KA, EV_VA), kna_ref[...]).astype(BF16)
    va_ref[0, :, :HEAD_DIM] = proj(EV_VA, EV_QI).astype(BF16)
    va_ref[0, :, HEAD_DIM:] = ones_col
    qi_ref[0] = proj(EV_QI, EV_QB).astype(BF16)
    vb = proj(EV_VB, EV_MISC)
    for h in range(N_HEADS):
        vb_ref[0, :, h * AUG:h * AUG + HEAD_DIM] = vb[:, h * HEAD_DIM:(h + 1) * HEAD_DIM].astype(BF16)
        vb_ref[0, :, h * AUG + HEAD_DIM:(h + 1) * AUG] = ones_col

    misc = proj(EV_MISC, EV_END)
    kd_ref[0] = jnp.where(lane < IDX_DIM, misc, pltpu.roll(misc, IDX_DIM, 1)).astype(BF16)
    wi_ref[0] = misc.T[MISC_W:MISC_W + N_IDX_HEADS, :]

    zf = misc + bf_ref[...]
    logf = jnp.minimum(zf, 0.0) - jnp.log(1.0 + jnp.exp(-jnp.abs(zf)))
    logf = jnp.where((lane >= MISC_F) & (lane < MISC_F + N_HEADS), logf, 0.0)
    cb = min(tm, CUMSUM_BLOCK)
    tri = _tri(cb).astype(BF16)
    carry = carry_ref[...]
    blocks = []
    for r in range(tm // cb):
        parts = _split3(logf[r * cb:(r + 1) * cb])
        blocks.append(sum(_dot(tri, part.astype(BF16)) for part in parts) + carry)
        carry = blocks[-1][cb - 1:cb, :]
    csum = jnp.concatenate(blocks, axis=0)
    carry_ref[...] = carry

    qb = proj(EV_QB, EV_KB)
    kb = proj(EV_KB, EV_VB)
    one = jnp.ones((tm, LANES), F32)
    zero = jnp.zeros((tm, LANES), F32)
    for h in range(N_HEADS):
        sl = slice(h * HEAD_DIM, (h + 1) * HEAD_DIM)
        c = jnp.broadcast_to(csum[:, MISC_F + h:MISC_F + h + 1], (tm, LANES)) * LOG2E
        hi, mid, lo = _split3(c)
        aq = jnp.where(lane == 0, hi, jnp.where(lane == 1, mid, jnp.where(lane == 2, lo,
             jnp.where(lane < 6, one, zero))))
        ak = jnp.where(lane < 3, one, jnp.where(lane == 3, -hi, jnp.where(lane == 4, -mid,
             jnp.where(lane == 5, -lo, zero))))
        qb_ref[0, :, h * AUG:h * AUG + LANES] = (_rms(qb[:, sl], qnb_ref[...]) * scale).astype(BF16)
        qb_ref[0, :, h * AUG + LANES:(h + 1) * AUG] = aq.astype(BF16)
        kb_ref[0, :, h * AUG:h * AUG + LANES] = _rms(kb[:, sl], knb_ref[...]).astype(BF16)
        kb_ref[0, :, h * AUG + LANES:(h + 1) * AUG] = ak.astype(BF16)


def _even_proj(x, g, w, bf, qna, kna, qnb, knb):
    B, S, D = x.shape
    tm = min(TM_EVEN, S)
    tok = lambda c: pl.BlockSpec((1, tm, c), lambda b, j: (b, j, 0))
    row = lambda c: pl.BlockSpec((1, c), lambda b, j: (0, 0))
    out_cols = (512, 128, AUG, 512, 128, None, N_HEADS * AUG, N_HEADS * AUG, N_HEADS * AUG)
    out_specs = [tok(c) if c else pl.BlockSpec((1, N_IDX_HEADS, tm), lambda b, j: (b, 0, j))
                 for c in out_cols]
    out_shape = [jax.ShapeDtypeStruct((B, S, c), BF16) if c else
                 jax.ShapeDtypeStruct((B, N_IDX_HEADS, S), F32) for c in out_cols]
    return pl.pallas_call(
        _even_proj_kernel,
        grid=(B, S // tm),
        in_specs=[tok(D), row(D), _resident(w.shape), row(LANES), row(LANES), row(LANES),
                  row(LANES), row(LANES)],
        out_specs=out_specs,
        out_shape=out_shape,
        scratch_shapes=[pltpu.VMEM((1, LANES), F32)],
        compiler_params=_cparams("arbitrary", "arbitrary"),
        name="even_proj",
    )(x, g, w, bf, qna, kna, qnb, knb)


def _fold_max(mx, s):
    for g in range(s.shape[1] // LANES):
        mx = jnp.maximum(mx, s[:, g * LANES:(g + 1) * LANES])
    return mx


def _flash_kernel(q_ref, k_ref, v_ref, o_ref, lg_ref, acc_ref, *, tq, tk):
    i = pl.program_id(2)
    q = q_ref[0]
    half = tq // 2

    def kchunk(ref, j):
        return ref[0, pl.ds(pl.multiple_of(j * tk, tk), tk), :]

    def pass_a(j, mx):
        s = _dot_nt(q, kchunk(k_ref, j))
        lg_ref[j] = s
        return _fold_max(mx, s)

    mx = lax.fori_loop(0, i, pass_a, jnp.full((tq, LANES), NEG_INF, F32))
    kd = kchunk(k_ref, i)
    tri = _tri(half)
    s_top = jnp.where(tri, _dot_nt(q[:half], kd[:half]), NEG_INF)
    s_bot = _dot_nt(q[half:], kd)
    s_bot = jnp.concatenate([s_bot[:, :half], jnp.where(tri, s_bot[:, half:], NEG_INF)], axis=1)
    lg_ref[i, :half, :half] = s_top
    lg_ref[i, half:, :] = s_bot
    mx = jnp.concatenate([_fold_max(mx[:half], s_top), _fold_max(mx[half:], s_bot)], axis=0)
    m = jnp.max(mx, axis=-1, keepdims=True)

    acc_ref[...] = jnp.zeros_like(acc_ref)

    def pass_b(j, c):
        p = jnp.exp2(lg_ref[j] - m).astype(BF16)
        acc_ref[...] += _dot(p, kchunk(v_ref, j))
        return c

    lax.fori_loop(0, i, pass_b, 0)
    vd = kchunk(v_ref, i)
    p_top = jnp.exp2(lg_ref[i, :half, :half] - m[:half]).astype(BF16)
    p_bot = jnp.exp2(lg_ref[i, half:, :] - m[half:]).astype(BF16)
    acc_top = acc_ref[:half, :] + _dot(p_top, vd[:half])
    acc_bot = acc_ref[half:, :] + _dot(p_bot, vd)
    o_ref[0, :half, :] = (acc_top[:, :HEAD_DIM] / acc_top[:, HEAD_DIM:HEAD_DIM + 1]).astype(o_ref.dtype)
    o_ref[0, half:, :] = (acc_bot[:, :HEAD_DIM] / acc_bot[:, HEAD_DIM:HEAD_DIM + 1]).astype(o_ref.dtype)


def _flash(q, k, v):
    B, S, _ = q.shape
    tq, tk = min(TQ_FLASH, S), min(TK_FLASH, S)
    assert tq == tk, "the diagonal chunk is split on the query block's own key chunk"
    return pl.pallas_call(
        functools.partial(_flash_kernel, tq=tq, tk=tk),
        grid=(B, N_HEADS, S // tq),
        in_specs=[pl.BlockSpec((1, tq, AUG), lambda b, h, i: (b, i, h)),
                  pl.BlockSpec((1, S, AUG), lambda b, h, i: (b, 0, h)),
                  pl.BlockSpec((1, S, AUG), lambda b, h, i: (b, 0, h))],
        out_specs=pl.BlockSpec((1, tq, HEAD_DIM), lambda b, h, i: (b, i, h)),
        out_shape=jax.ShapeDtypeStruct((B, S, N_HEADS * HEAD_DIM), BF16),
        scratch_shapes=[pltpu.VMEM((S // tk, tq, tk), F32), pltpu.VMEM((tq, AUG), F32)],
        compiler_params=_cparams("arbitrary", "arbitrary", "arbitrary"),
        name="flash_attn",
    )(q, k, v)


def _dsa_kernel(qa_ref, ka_ref, va_ref, qi_ref, kd_ref, wi_ref, o_ref,
                sc_ref, qstk_ref, lg_ref, acc_ref, *, tkc, topk):
    tq = qa_ref.shape[1]
    i = pl.program_id(1)
    nkc = (i * tq + tq + tkc - 1) // tkc
    kf = float(topk)
    lane = _lane_iota((tq, LANES))
    keyi = lax.broadcasted_iota(jnp.int32, (tkc, tq), 0)
    qpos = i * tq + lax.broadcasted_iota(jnp.int32, (tkc, tq), 1)

    q8 = qi_ref[0]
    for j in range(N_IDX_HEADS):
        grp = q8[:, LANES * (j // 2):LANES * (j // 2 + 1)]
        keep = (lane < IDX_DIM) if j % 2 == 0 else (lane >= IDX_DIM)
        qstk_ref[tq * j:tq * (j + 1), :] = jnp.where(keep, grp, jnp.zeros_like(grp))
    wrows = [wi_ref[0, j:j + 1, :] for j in range(N_IDX_HEADS)]

    def chunk(ref, kc):
        return ref[0, pl.ds(pl.multiple_of(kc * tkc, tkc), tkc), :]

    def fold8(x, op):
        cols = []
        for g in range(x.shape[1] // LANES):
            xg = x[:, g * LANES:(g + 1) * LANES]
            rows = xg.shape[0]
            while rows > 8:
                rows //= 2
                xg = op(xg[:rows], xg[rows:])
            cols.append(xg)
        return jnp.concatenate(cols, axis=1)

    def p1(kc, carry, diagonal=False):
        rmax, rmin, npos, nnonneg = carry
        s8 = _dot_nt(chunk(kd_ref, kc), qstk_ref[...])
        score = wrows[0] * jnp.maximum(s8[:, 0:tq], 0.0)
        for j in range(1, N_IDX_HEADS):
            score = score + wrows[j] * jnp.maximum(s8[:, tq * j:tq * (j + 1)], 0.0)
        low = high = score
        if diagonal:
            adm = (kc * tkc + keyi) <= qpos
            low = jnp.where(adm, score, NEG_INF)
            high = jnp.where(adm, score, -NEG_INF)
        sc_ref[kc] = low
        rmax = jnp.maximum(rmax, fold8(low, jnp.maximum))
        rmin = jnp.minimum(rmin, fold8(high, jnp.minimum))
        npos = npos + fold8(jnp.where(low > 0.0, 1.0, 0.0), jnp.add)
        nnonneg = nnonneg + fold8(jnp.where(low >= 0.0, 1.0, 0.0), jnp.add)
        return rmax, rmin, npos, nnonneg

    zero8 = jnp.zeros((8, tq), F32)
    carry = lax.fori_loop(0, nkc - 1, p1, (jnp.full((8, tq), NEG_INF, F32),
                                           jnp.full((8, tq), -NEG_INF, F32), zero8, zero8))
    rmax, rmin, npos, nnonneg = p1(nkc - 1, carry, diagonal=True)
    rmax = jnp.max(rmax, axis=0, keepdims=True)
    rmin = jnp.min(rmin, axis=0, keepdims=True)
    c_gt0 = jnp.sum(npos, axis=0, keepdims=True)
    c_ge0 = jnp.sum(nnonneg, axis=0, keepdims=True)

    def row_count(cmp, t):
        def cb(kc, c):
            return c + fold8(jnp.where(cmp(sc_ref[kc], t), 1.0, 0.0), jnp.add)
        c = lax.fori_loop(0, nkc - 1, cb, zero8)
        hk, hq = tkc // 2, tq // 2
        c = c + fold8(jnp.where(cmp(sc_ref[nkc - 1, :hk, :], t), 1.0, 0.0), jnp.add)
        late = fold8(jnp.where(cmp(sc_ref[nkc - 1, hk:, hq:], t[:, hq:]), 1.0, 0.0), jnp.add)
        c = c + jnp.concatenate([jnp.zeros((8, hq), F32), late], axis=1)
        return jnp.sum(c, axis=0, keepdims=True)

    def any_row(flag):
        return jnp.max(flag.astype(jnp.int32))

    def select_threshold():
        hi0 = rmax + (jnp.abs(rmax) * 1.2e-7 + 1e-37)
        n_adm = (qpos[0:1, :] + 1).astype(F32)
        above = c_ge0 >= kf
        pend0 = (above & (c_gt0 < kf)).astype(jnp.int32)
        lo0 = jnp.where(above, 0.0, rmin)
        cnt0 = jnp.where(above, c_ge0, n_adm)
        hi0 = jnp.where(above, hi0, 0.0)
        few = n_adm <= kf
        lo0 = jnp.where(few, -F32_MAX, lo0)
        cnt0 = jnp.where(few, kf, cnt0)
        pend0 = jnp.where(few, 0, pend0)

        def unresolved(cnt, pend):
            return any_row((cnt > kf + 1.0) & (pend == 0))

        def fold_two_min(x):
            c1, c2 = [], []
            for g in range(tq // LANES):
                xg = x[:, g * LANES:(g + 1) * LANES]
                rows = tkc // 2
                m1 = jnp.minimum(xg[:rows], xg[rows:])
                m2 = jnp.maximum(xg[:rows], xg[rows:])
                while rows > 8:
                    rows //= 2
                    a1, b1, a2, b2 = m1[:rows], m1[rows:], m2[:rows], m2[rows:]
                    m1, m2 = (jnp.minimum(a1, b1),
                              jnp.minimum(jnp.maximum(a1, b1), jnp.minimum(a2, b2)))
                c1.append(m1)
                c2.append(m2)
            return jnp.concatenate(c1, axis=1), jnp.concatenate(c2, axis=1)

        def snap(lo, cnt, pend):
            def two_min(kc, st):
                m1, m2 = st
                blk = sc_ref[kc]
                a1, a2 = fold_two_min(jnp.where(blk >= lo, blk, -NEG_INF))
                return (jnp.minimum(m1, a1),
                        jnp.minimum(jnp.maximum(m1, a1), jnp.minimum(m2, a2)))
            inf8 = jnp.full((8, tq), -NEG_INF, F32)
            m1, m2 = lax.fori_loop(0, nkc, two_min, (inf8, inf8))
            a1 = jnp.min(m1, axis=0, keepdims=True)
            at_min = m1 == a1
            others = jnp.minimum(jnp.min(jnp.where(at_min, -NEG_INF, m1), axis=0, keepdims=True),
                                 jnp.min(m2, axis=0, keepdims=True))
            n_min = jnp.sum(jnp.where(at_min, 1.0, 0.0), axis=0, keepdims=True)
            a2 = jnp.where(n_min > 1.0, a1, others)
            ok = (cnt == kf + 1.0) & (pend == 0) & (a2 > a1)
            return jnp.where(ok, a2, lo), jnp.where(ok, kf, cnt)

        def bis_cond(st):
            return (st[-1] > 0) & (st[0] < BISECT_ROUND)

        def bis_body(st):
            it, lo, hi, cnt, pend, _ = st
            for _ in range(BISECT_GROUP):
                mid = lo + (hi - lo) * 0.5
                c = row_count(jnp.greater_equal, mid)
                up = c >= kf
                lo = jnp.where(up, mid, lo)
                hi = jnp.where(up, hi, mid)
                cnt = jnp.where(up, c, cnt)
            return it + BISECT_GROUP, lo, hi, cnt, pend, unresolved(cnt, pend)

        def resolve_ties(lo, cnt, pend):
            def boundary_value():
                def vmin(kc, v):
                    blk = sc_ref[kc]
                    return jnp.minimum(v, fold8(jnp.where(blk >= lo, blk, -NEG_INF), jnp.minimum))
                v = lax.fori_loop(0, nkc, vmin, jnp.full((8, tq), -NEG_INF, F32))
                v = jnp.min(v, axis=0, keepdims=True)
                return v, row_count(jnp.greater, v)

            only_zero_ties = any_row((cnt != kf) & (pend == 0)) == 0
            v, cgt = lax.cond(only_zero_ties, lambda: (jnp.zeros((1, tq), F32), c_gt0),
                              boundary_value)
            tie = (cnt != kf) & (cgt < kf)
            need = kf - cgt
            tri = _tri(tkc).astype(BF16)

            def drop(kc, seen):
                blk = sc_ref[kc]
                eq = tie & (blk == v)
                rank = _dot(tri, jnp.where(eq, 1.0, 0.0).astype(BF16)) + seen
                sc_ref[kc] = jnp.where(eq & (rank > need), NEG_INF, blk)
                return rank[tkc - 1:tkc, :]
            lax.fori_loop(0, nkc, drop, jnp.zeros((1, tq), F32))
            return jnp.where(tie, v, lo), jnp.where(tie, kf, cnt)

        def round_cond(st):
            return (st[-1] > 0) & (st[0] < BISECT_MAX_ROUNDS)

        def round_body(st):
            r, lo, hi, cnt, pend, _ = st
            _, lo, hi, cnt, pend, _ = lax.while_loop(
                bis_cond, bis_body, (0, lo, hi, cnt, pend, unresolved(cnt, pend)))
            lo, cnt = lax.cond(any_row((cnt == kf + 1.0) & (pend == 0)) > 0, snap,
                               lambda a, b, c: (a, b), lo, cnt, pend)
            lo, cnt = lax.cond(any_row(cnt != kf) > 0, resolve_ties, lambda a, b, c: (a, b),
                               lo, cnt, pend)
            return r + 1, lo, hi, cnt, jnp.zeros_like(pend), any_row(cnt != kf)

        st = lax.while_loop(round_cond, round_body,
                            (0, lo0, hi0, cnt0, pend0, any_row(cnt0 != kf)))
        return st[1]

    thr = lax.cond((i + 1) * tq > topk, select_threshold,
                   lambda: jnp.full((1, tq), -F32_MAX, F32))

    qa = qa_ref[0]
    hg = DSA_HEAD_GROUP
    for g0 in range(0, N_HEADS, hg):
        qs = jnp.concatenate([qa[:, h * HEAD_DIM:(h + 1) * HEAD_DIM] for h in range(g0, g0 + hg)],
                             axis=0)

        def p3a(kc, mx, qs=qs, first=(g0 == 0)):
            s = _dot_nt(qs, chunk(ka_ref, kc))
            if first:
                sel = jnp.where(sc_ref[kc] >= thr, 0.0, NEG_INF).T
                sc_ref[kc] = sel
            else:
                sel = sc_ref[kc]
            parts = []
            for h in range(hg):
                rs = slice(h * tq, (h + 1) * tq)
                sh = s[rs] + sel
                lg_ref[kc, rs, :] = sh
                parts.append(_fold_max(mx[rs], sh))
            return jnp.concatenate(parts, axis=0)

        mx = lax.fori_loop(0, nkc, p3a, jnp.full((hg * tq, LANES), NEG_INF, F32))
        m = jnp.max(mx, axis=-1, keepdims=True)

        acc_ref[...] = jnp.zeros_like(acc_ref)

        def p3b(kc, c, m=m):
            p = jnp.exp2(lg_ref[kc] - m).astype(BF16)
            acc_ref[...] += _dot(p, chunk(va_ref, kc))
            return c

        lax.fori_loop(0, nkc, p3b, 0)
        for h in range(hg):
            acc = acc_ref[h * tq:(h + 1) * tq, :]
            o_ref[0, :, (g0 + h) * HEAD_DIM:(g0 + h + 1) * HEAD_DIM] = (
                acc[:, :HEAD_DIM] / acc[:, HEAD_DIM:HEAD_DIM + 1]).astype(o_ref.dtype)


def _dsa(qa, ka, va, qi, kd, wi):
    B, S, _ = qa.shape
    tq, tkc = TQ_DSA, min(TK_DSA, S)
    assert tq == tkc, "the transposed selection mask reuses the score scratch"
    topk = min(TOPK_MAX, S // 4)
    qblk = lambda c: pl.BlockSpec((1, tq, c), lambda b, i: (b, i, 0))
    full = lambda c: pl.BlockSpec((1, S, c), lambda b, i: (b, 0, 0))
    return pl.pallas_call(
        functools.partial(_dsa_kernel, tkc=tkc, topk=topk),
        grid=(B, S // tq),
        in_specs=[qblk(512), full(128), full(AUG), qblk(512), full(128),
                  pl.BlockSpec((1, N_IDX_HEADS, tq), lambda b, i: (b, 0, i))],
        out_specs=qblk(512),
        out_shape=jax.ShapeDtypeStruct((B, S, N_HEADS * HEAD_DIM), BF16),
        scratch_shapes=[pltpu.VMEM((S // tkc, tkc, tq), F32),
                        pltpu.VMEM((N_IDX_HEADS * tq, LANES), BF16),
                        pltpu.VMEM((S // tkc, DSA_HEAD_GROUP * tq, tkc), F32),
                        pltpu.VMEM((DSA_HEAD_GROUP * tq, AUG), F32)],
        compiler_params=_cparams("arbitrary", "arbitrary"),
        name="dsa_attn",
    )(qa, ka, va, qi, kd, wi)


def _rope_table_kernel(pos_ref, inv_ref, cos_ref, sin_ref):
    tm = pos_ref.shape[1]
    lane = _lane_iota((tm, LANES))
    ang = pos_ref[0].astype(F32) * inv_ref[...]
    half = QK_ROPE // 2
    cos_ref[0] = jnp.where(lane < QK_ROPE, jnp.cos(ang), 0.0)
    sn = jnp.sin(ang)
    sin_ref[0] = jnp.where(lane < half, -sn, jnp.where(lane < QK_ROPE, sn, 0.0))


def _rope_tables(positions):
    B, S = positions.shape
    tm = TM_PROJ
    half = QK_ROPE // 2
    inv = ROPE_THETA ** (-jnp.arange(half, dtype=F32) / half)
    inv = jnp.concatenate([inv, inv, jnp.zeros((LANES - QK_ROPE,), F32)])[None, :]
    return pl.pallas_call(
        _rope_table_kernel,
        grid=(B, S // tm),
        in_specs=[pl.BlockSpec((1, tm, 1), lambda b, j: (b, j, 0)),
                  pl.BlockSpec((1, LANES), lambda b, j: (0, 0))],
        out_specs=[pl.BlockSpec((1, tm, LANES), lambda b, j: (b, j, 0))] * 2,
        out_shape=[jax.ShapeDtypeStruct((B, S, LANES), F32)] * 2,
        compiler_params=_cparams("arbitrary", "arbitrary"),
        name="rope_tables",
    )(positions[:, :, None], inv)


def _rope(t, cos, sin):
    lane = _lane_iota(t.shape)
    half = QK_ROPE // 2
    swap = jnp.where(lane < half, pltpu.roll(t, LANES - half, 1), pltpu.roll(t, half, 1))
    return t * cos + swap * sin


def _odd_proj_kernel(x_ref, g_ref, w_ref, cqn_ref, ckvn_ref, wuq_ref, wukv_ref, qnc_ref, knc_ref,
                     cos_ref, sin_ref, q_ref, k_ref, v_ref, z_ref, xbc_ref, dt_ref):
    xn = _rms(x_ref[0], g_ref[...]).astype(BF16)

    def proj(lo, hi):
        return _dot(xn, w_ref[:, lo:hi])

    cos = cos_ref[0]
    sin = sin_ref[0]
    n_qk = QK_NOPE + QK_ROPE
    scale = n_qk ** -0.5 * LOG2E
    ones_col = jnp.where(_lane_iota(cos.shape) == 0, 1.0, 0.0).astype(BF16)

    cq = _rms(proj(OD_CQ, OD_CKV), cqn_ref[...]).astype(BF16)
    q = _dot(cq, wuq_ref[...])
    ckv = _rms(proj(OD_CKV, OD_KR), ckvn_ref[...]).astype(BF16)
    kv = _dot(ckv, wukv_ref[...])
    kr = proj(OD_KR, OD_DT)
    kr_ss = jnp.sum(kr * kr, axis=-1, keepdims=True)
    kr_rot = _rope(kr * knc_ref[:, LANES:], cos, sin)
    for h in range(N_HEADS):
        qh = q[:, h * AUG:(h + 1) * AUG]
        qh = _rms(qh, qnc_ref[...], n_qk) * scale
        q_ref[0, :, h * AUG:h * AUG + LANES] = qh[:, :LANES].astype(BF16)
        q_ref[0, :, h * AUG + LANES:(h + 1) * AUG] = _rope(qh[:, LANES:], cos, sin).astype(BF16)
        kn = kv[:, h * AUG:h * AUG + LANES]
        r = lax.rsqrt((jnp.sum(kn * kn, axis=-1, keepdims=True) + kr_ss) * (1.0 / n_qk) + EPS)
        k_ref[0, :, h * AUG:h * AUG + LANES] = (kn * r * knc_ref[:, :LANES]).astype(BF16)
        k_ref[0, :, h * AUG + LANES:(h + 1) * AUG] = (kr_rot * r).astype(BF16)
        v_ref[0, :, h * AUG:h * AUG + HEAD_DIM] = kv[:, h * AUG + LANES:(h + 1) * AUG].astype(BF16)
        v_ref[0, :, h * AUG + HEAD_DIM:(h + 1) * AUG] = ones_col

    dt_ref[0] = proj(OD_DT, OD_Z)
    z_ref[0] = proj(OD_Z, OD_XBC)
    xbc_ref[0] = proj(OD_XBC, OD_END)


def _odd_proj(x, g, w, cqn, ckvn, wuq, wukv, qnc, knc, cos, sin):
    B, S, D = x.shape
    tm = TM_PROJ
    tok = lambda c: pl.BlockSpec((1, tm, c), lambda b, j: (b, j, 0))
    row = lambda c: pl.BlockSpec((1, c), lambda b, j: (0, 0))
    out_cols = (N_HEADS * AUG, N_HEADS * AUG, N_HEADS * AUG, SSM_INNER, CONV_DIM, LANES)
    out_dt = (BF16, BF16, BF16, F32, F32, F32)
    return pl.pallas_call(
        _odd_proj_kernel,
        grid=(B, S // tm),
        in_specs=[tok(D), row(D), _resident(w.shape), row(Q_LORA), row(KV_LORA),
                  _resident(wuq.shape), _resident(wukv.shape), row(AUG), row(AUG),
                  tok(LANES), tok(LANES)],
        out_specs=[tok(c) for c in out_cols],
        out_shape=[jax.ShapeDtypeStruct((B, S, c), dt) for c, dt in zip(out_cols, out_dt)],
        compiler_params=_cparams("arbitrary", "arbitrary"),
        name="odd_proj",
    )(x, g, w, cqn, ckvn, wuq, wukv, qnc, knc, cos, sin)


def _ssd_kernel(xbc_ref, z_ref, dt_ref, ex_ref, cw_ref, cb_ref, dtb_ref, alog_ref, dskip_ref,
                gn_ref, y_ref, xe_ref, st_ref):
    q = xbc_ref.shape[1]
    halo = 8

    @pl.when(pl.program_id(1) == 0)
    def _():
        xe_ref[0:halo, :] = jnp.zeros((halo, CONV_DIM), F32)
        st_ref[...] = jnp.zeros_like(st_ref)

    xe_ref[halo:halo + q, :] = xbc_ref[0]
    ext = xe_ref[...]
    conv = cb_ref[...] + ext[halo:] * cw_ref[CONV_WIDTH - 1:CONV_WIDTH, :]
    for s in range(1, CONV_WIDTH):
        w = CONV_WIDTH - 1 - s
        conv = conv + pltpu.roll(ext, s, 0)[halo:] * cw_ref[w:w + 1, :]
    xe_ref[0:halo, :] = ext[q:q + halo]
    act = conv * _sigmoid(conv)
    xs = act[:, :SSM_INNER]
    gs = SSM_STATE
    bm = [act[:, SSM_INNER + g * gs:SSM_INNER + (g + 1) * gs] for g in range(SSM_GROUPS)]
    cm = [act[:, SSM_INNER + (SSM_GROUPS + g) * gs:SSM_INNER + (SSM_GROUPS + g + 1) * gs]
          for g in range(SSM_GROUPS)]

    lane = _lane_iota((q, LANES))
    dt = _softplus(dt_ref[0] + dtb_ref[...])
    a = jnp.where(lane < SSM_HEADS, dt * (-jnp.exp(alog_ref[...])), 0.0)
    tri = _tri(q)
    trif = tri.astype(F32)
    acum = _dot_hi(trif, a)
    acum_t = lax.dot_general(a.T, trif, (((1,), (1,)), ((), ())), preferred_element_type=F32,
                             precision=lax.Precision.HIGHEST)
    last = acum[q - 1:q, :]
    dte = jnp.exp(last - acum)
    eac = jnp.exp(acum)

    stack = jnp.concatenate([dt, dte, eac], axis=0)
    s_hi = stack.astype(BF16)
    s_lo = (stack - s_hi.astype(F32)).astype(BF16)
    wide = _dot(jnp.concatenate([s_hi, s_lo], axis=1), ex_ref[...])
    dt_x, dte_x, eac_x = wide[0:q], wide[q:2 * q], wide[2 * q:3 * q]
    cdec_x = eac_x[q - 1:q, :]

    xd = xs * dt_x
    xdd = (xd * dte_x).astype(BF16)
    xd16 = xd.astype(BF16)
    zero16 = jnp.zeros((q, LANES), BF16)

    rpg = SSM_HEADS // SSM_GROUPS
    gw = rpg * SSM_HEAD_DIM
    y_parts = []
    new_states = []
    for g in range(SSM_GROUPS):
        b16 = bm[g].astype(BF16)
        c16 = cm[g].astype(BF16)
        cb = _dot_nt(c16, b16)
        st_old = st_ref[:, g * gw:(g + 1) * gw]
        y_off = _dot(c16, st_old.astype(BF16)) * eac_x[:, g * gw:(g + 1) * gw]
        y_diag = []
        for pp in range(rpg // 2):
            p = g * (rpg // 2) + pp
            xp = xd16[:, p * LANES:(p + 1) * LANES]
            acc = None
            for e in range(2):
                h = 2 * p + e
                diff = acum[:, h:h + 1] - acum_t[h:h + 1, :]
                lm = jnp.exp(jnp.where(tri, diff, NEG_INF))
                gmat = (cb * lm).astype(BF16)
                keep = (lane < SSM_HEAD_DIM) if e == 0 else (lane >= SSM_HEAD_DIM)
                part = _dot(gmat, jnp.where(keep, xp, zero16))
                acc = part if acc is None else acc + part
            y_diag.append(acc)
        y_parts.append(jnp.concatenate(y_diag, axis=1) + y_off)
        st_new = _dot(bm[g].T.astype(BF16), xdd[:, g * gw:(g + 1) * gw])
        new_states.append(st_new)
    for g in range(SSM_GROUPS):
        sl = slice(g * gw, (g + 1) * gw)
        st_ref[:, sl] = st_ref[:, sl] * cdec_x[:, sl] + new_states[g]

    y = jnp.concatenate(y_parts, axis=1) + dskip_ref[...] * xs
    zz = z_ref[0]
    y = y * (zz * _sigmoid(zz))
    for g in range(SSM_GROUPS):
        sl = slice(g * gw, (g + 1) * gw)
        y_ref[0, :, sl] = _rms(y[:, sl], gn_ref[:, sl]).astype(y_ref.dtype)


def _ssd(xbc, z, dt, cw, cb, dtb, alog, dskip, gn):
    B, S, _ = xbc.shape
    q = SSD_CHUNK
    tok = lambda c: pl.BlockSpec((1, q, c), lambda b, j: (b, j, 0))
    row = lambda r, c: pl.BlockSpec((r, c), lambda b, j: (0, 0))
    head = jnp.arange(LANES)[:, None]
    chan = jnp.arange(SSM_INNER)[None, :] // SSM_HEAD_DIM
    ex = jnp.tile((head == chan).astype(BF16), (2, 1))
    return pl.pallas_call(
        _ssd_kernel,
        grid=(B, S // q),
        in_specs=[tok(CONV_DIM), tok(SSM_INNER), tok(LANES), row(*ex.shape),
                  row(CONV_WIDTH, CONV_DIM), row(1, CONV_DIM), row(1, LANES), row(1, LANES),
                  row(1, SSM_INNER), row(1, SSM_INNER)],
        out_specs=tok(SSM_INNER),
        out_shape=jax.ShapeDtypeStruct((B, S, SSM_INNER), BF16),
        scratch_shapes=[pltpu.VMEM((q + 8, CONV_DIM), F32), pltpu.VMEM((SSM_STATE, SSM_INNER), F32)],
        compiler_params=_cparams("arbitrary", "arbitrary"),
        name="ssd_scan",
    )(xbc, z, dt, ex, cw, cb, dtb, alog, dskip, gn)


def _out_mlp_kernel(x_ref, o1_ref, o2_ref, wo_ref, g_ref, w1_ref, w2_ref, y_ref,
                    xn_ref, acc_ref, *, tf):
    k1 = o1_ref.shape[1]
    xnew = x_ref[...] + _dot(o1_ref[...], wo_ref[:k1, :]) + _dot(o2_ref[...], wo_ref[k1:, :])
    xn_ref[...] = _rms(xnew, g_ref[...]).astype(BF16)
    acc_ref[...] = xnew

    def body(c, carry):
        cs = pl.multiple_of(c * tf, tf)
        h = _dot(xn_ref[...], w1_ref[:, pl.ds(cs, tf)])
        a = jnp.square(jnp.maximum(h, 0.0)).astype(BF16)
        acc_ref[...] += _dot(a, w2_ref[pl.ds(cs, tf), :])
        return carry

    lax.fori_loop(0, w1_ref.shape[1] // tf, body, 0)
    y_ref[...] = acc_ref[...]


def _out_mlp(x, o1, o2, wo, g, w1, w2):
    T, D = x.shape
    tm = TM_MLP
    assert wo.shape[0] == o1.shape[1] + o2.shape[1]
    tok = lambda c: pl.BlockSpec((tm, c), lambda i: (i, 0))
    return pl.pallas_call(
        functools.partial(_out_mlp_kernel, tf=TF_MLP),
        grid=(T // tm,),
        in_specs=[tok(D), tok(o1.shape[1]), tok(o2.shape[1]), _resident(wo.shape),
                  pl.BlockSpec((1, D), lambda i: (0, 0)), _resident(w1.shape), _resident(w2.shape)],
        out_specs=tok(D),
        out_shape=jax.ShapeDtypeStruct((T, D), F32),
        scratch_shapes=[pltpu.VMEM((tm, D), BF16), pltpu.VMEM((tm, D), F32)],
        compiler_params=_cparams("arbitrary"),
        name="out_mlp",
    )(x, o1, o2, wo, g, w1, w2)


def _pack_even_w(w):
    w = w.astype(BF16)
    qa, ka, va, qi, ki, wi, qb, kb, vb, fb = jnp.split(
        w, [512, 640, 768, 1280, 1344, 1352, 1864, 2376, 2888], axis=-1)
    pad = jnp.zeros(w.shape[:-1] + (LANES - IDX_DIM - N_IDX_HEADS - N_HEADS,), w.dtype)
    return jnp.concatenate([qa, ka, va, qi, qb, kb, vb, ki, wi, fb, pad], axis=-1)


def _pack_odd_w(w):
    w = w.astype(BF16)
    cq, ckv, kr, z, xbc, dt = jnp.split(w, [384, 640, 704, 1728, 3776], axis=-1)
    zpad = lambda n: jnp.zeros(w.shape[:-1] + (n,), w.dtype)
    return jnp.concatenate([cq, ckv, kr, zpad(LANES - QK_ROPE), dt, zpad(LANES - SSM_HEADS), z, xbc],
                           axis=-1)


def _pack_wuq(w):
    n_qk = QK_NOPE + QK_ROPE
    w = w.reshape(w.shape[0], N_HEADS, n_qk)
    w = jnp.pad(w, ((0, 0), (0, 0), (0, AUG - n_qk)))
    return w.reshape(w.shape[0], N_HEADS * AUG).astype(BF16)


def _row(v, width=None):
    v = v.astype(F32)[None, :]
    if width is not None and width > v.shape[1]:
        v = jnp.pad(v, ((0, 0), (0, width - v.shape[1])))
    return v


def kernel(x, positions, ev_norm, ev_w_in, ev_b_f, ev_qn_a, ev_kn_a, ev_qn_b, ev_kn_b, ev_w_out,
           od_norm, od_w_in, od_cq_norm, od_ckv_norm, od_w_uq, od_w_ukv, od_qn_c, od_kn_c,
           od_conv_w, od_conv_b, od_dt_bias, od_a_log, od_d_skip, od_gate_norm, od_w_out,
           mlp_norm, mlp_w1, mlp_w2):
    B, S, D = x.shape
    depth = mlp_w1.shape[0]
    ev_w, od_w = _pack_even_w(ev_w_in), _pack_odd_w(od_w_in)
    ev_wo, od_wo = ev_w_out.astype(BF16), od_w_out.astype(BF16)
    w1, w2 = mlp_w1.astype(BF16), mlp_w2.astype(BF16)
    cos = sin = None
    for layer in range(depth):
        i = layer // 2
        if layer % 2 == 0:
            bf = jnp.zeros((1, LANES), F32).at[0, MISC_F:MISC_F + N_HEADS].set(ev_b_f[i])
            qa, ka, va, qi, kd, wi, qb, kb, vb = _even_proj(
                x, _row(ev_norm[i]), ev_w[i], bf, _row(ev_qn_a[i]),
                _row(ev_kn_a[i]), _row(ev_qn_b[i]), _row(ev_kn_b[i]))
            o1 = _dsa(qa, ka, va, qi, kd, wi)
            o2 = _flash(qb, kb, vb)
            wo = ev_wo[i]
        else:
            if cos is None:
                cos, sin = _rope_tables(positions)
            q, k, v, z, xbc, dt = _odd_proj(
                x, _row(od_norm[i]), od_w[i], _row(od_cq_norm[i]),
                _row(od_ckv_norm[i]), _pack_wuq(od_w_uq[i]), od_w_ukv[i].astype(BF16),
                _row(od_qn_c[i], AUG), _row(od_kn_c[i], AUG), cos, sin)
            o1 = _flash(q, k, v)
            o2 = _ssd(xbc, z, dt, od_conv_w[i].astype(F32), _row(od_conv_b[i]),
                      _row(od_dt_bias[i], LANES), _row(od_a_log[i], LANES),
                      _row(jnp.repeat(od_d_skip[i], SSM_HEAD_DIM)), _row(od_gate_norm[i]))
            wo = od_wo[i]
        x = _out_mlp(x.reshape(B * S, D), o1.reshape(B * S, -1), o2.reshape(B * S, -1),
                     wo, _row(mlp_norm[layer]), w1[layer], w2[layer]).reshape(B, S, D)
    return x
```
